```python
import math
import jax, jax.numpy as jnp
from jax import lax
import numpy as np

D_MODEL = 1024
BATCH = 4
SEQ = 8192
DEPTH = 1
DEC_BATCH = 16
DEC_SEQ = 16
PAST_LEN = 1024

CHUNK = 64
N_META = 16
PREFIX = 128
Q_BLOCK = 128
RET_HEADS = 4
RET_DK = D_MODEL // 8
RET_DV = 2 * RET_DK
RET_QK_W = RET_HEADS * RET_DK
RET_V_W = RET_HEADS * RET_DV
ROPE_BASE = 10000.0
RET_EPS = 1e-6
DIFF_HEADS = 8
DIFF_DH = D_MODEL // 16
DIFF_DV = 2 * DIFF_DH
DIFF_W = DIFF_HEADS * DIFF_DV
DIFF_EPS = 1e-5
N_BUCKETS = 32
MAX_DISTANCE = 128
D_FF = ((8 * D_MODEL + 3 * 256 - 1) // (3 * 256)) * 256
NORM_EPS = 1e-6
NEG_INF = -1e30
IN_WIDTHS = (RET_QK_W, RET_QK_W, RET_V_W, RET_V_W,
             DIFF_HEADS * 2 * DIFF_DH, DIFF_HEADS * 2 * DIFF_DH, DIFF_W,
             D_MODEL, D_MODEL)
W_IN = sum(IN_WIDTHS)

kernel_name = 'hybrid_retention_diffattn_stream'


def rmsnorm(x, g=None, eps=NORM_EPS):
    xf = x.astype(jnp.float32)
    y = xf * lax.rsqrt(jnp.mean(xf * xf, axis=-1, keepdims=True) + eps)
    if g is not None:
        y = y * g.astype(jnp.float32)
    return y.astype(x.dtype)


def in_proj(xn, w_in):
    offsets = np.cumsum(IN_WIDTHS)[:-1].tolist()
    return jnp.split(xn @ w_in, offsets, axis=-1)


def rotary(x, pos):
    half = x.shape[-1] // 2
    freq = 1.0 / (ROPE_BASE ** jnp.linspace(0.0, 1.0, half, dtype=jnp.float32))
    ang = pos.astype(jnp.float32)[:, None] * freq[None, :]
    cos = jnp.cos(ang)[None, :, None, :]
    sin = jnp.sin(ang)[None, :, None, :]
    x1, x2 = x[..., :half], x[..., half:]
    return jnp.concatenate([x1 * cos - x2 * sin, x1 * sin + x2 * cos], axis=-1).astype(x.dtype)


def retention_log_decay():
    return jnp.log(1.0 - 2.0 ** (-5.0 - jnp.arange(RET_HEADS, dtype=jnp.float32)))


def retention_chunk(S, q, k, v):
    L = q.shape[1]
    lg = retention_log_decay()
    i = jnp.arange(L, dtype=jnp.float32)
    qf, kf, vf, Sf = (t.astype(jnp.float32) for t in (q, k, v, S))
    decay_in = jnp.exp(lg[:, None, None] * jnp.abs(i[:, None] - i[None, :]))
    s = jnp.einsum('blhd,bmhd->bhlm', qf, kf) * decay_in[None]
    o = jnp.einsum('bhlm,bmhe->blhe', s, vf)
    q_dec = jnp.exp(lg[None, :] * (i[:, None] + 1.0))
    o = o + jnp.einsum('blhd,bhde->blhe', qf * q_dec[None, :, :, None], Sf)
    k_dec = jnp.exp(lg[None, :] * (L - 1.0 - i)[:, None])
    S_new = jnp.exp(lg * L)[None, :, None, None] * Sf + jnp.einsum('bmhd,bmhe->bhde', kf * k_dec[None, :, :, None], vf)
    return o.astype(q.dtype), S_new.astype(S.dtype)


def t5_bucket(rel):
    nb = N_BUCKETS // 2
    max_exact = nb // 2
    ret = jnp.where(rel > 0, nb, 0)
    n = jnp.abs(rel)
    nf = jnp.maximum(n, max_exact).astype(jnp.float32)
    large = max_exact + (jnp.log(nf / max_exact) / math.log(MAX_DISTANCE / max_exact) * (nb - max_exact)).astype(jnp.int32)
    large = jnp.minimum(large, nb - 1)
    return ret + jnp.where(n < max_exact, n, large)


def diff_lambda(lq1, lk1, lq2, lk2, lam_init):
    f = lambda a, b: jnp.exp(jnp.sum(a.astype(jnp.float32) * b.astype(jnp.float32)))
    return f(lq1, lk1) - f(lq2, lk2) + lam_init


def diff_softmax_mix(q, k, v, bias, mask, lam):
    s = jnp.einsum('bqhcd,bkhcd->bchqk', q, k).astype(jnp.float32) * (DIFF_DH ** -0.5) + bias.astype(jnp.float32)
    s = jnp.where(mask, s, NEG_INF)
    p = jax.nn.softmax(s, axis=-1)
    a = p[:, 0] - lam * p[:, 1]
    return jnp.einsum('bhqk,bkhe->bqhe', a.astype(v.dtype), v)


def layer_tail(h, o_ret, rg, o_diff, gr, gd, lam_init, lw):
    (_, _, _, _, _, _, subg, w_rb, w_db, w_o, n2, w_up, w_down) = lw
    B, L, _ = h.shape
    y_ret = (rmsnorm(o_ret, None, RET_EPS) * jax.nn.silu(rg.reshape(B, L, RET_HEADS, RET_DV))).reshape(B, L, RET_V_W)
    y_diff = (rmsnorm(o_diff, subg, DIFF_EPS) * (1.0 - lam_init)).reshape(B, L, DIFF_W)
    merged = jax.nn.sigmoid(gr) * (y_ret @ w_rb) + jax.nn.sigmoid(gd) * (y_diff @ w_db)
    h = h + merged @ w_o
    gate, up = jnp.split(rmsnorm(h, n2) @ w_up, 2, axis=-1)
    return h + (jax.nn.silu(gate) * up) @ w_down


def prompt_layer(h, lw, lam_init, rel_bias):
    (n1, w_in, lq1, lk1, lq2, lk2) = lw[:6]
    B, LT, _ = h.shape
    xn = rmsnorm(h, n1)
    rq, rk, rv, rg, dq, dk, dv, gr, gd = in_proj(xn, w_in)
    idx = jnp.arange(LT)
    pos = idx - PREFIX
    valid = idx >= PREFIX - N_META
    q = rotary(rq.reshape(B, LT, RET_HEADS, RET_DK), pos) * (RET_DK ** -0.5)
    k = rotary(rk.reshape(B, LT, RET_HEADS, RET_DK), pos) * valid[None, :, None, None].astype(h.dtype)
    v = rv.reshape(B, LT, RET_HEADS, RET_DV)
    nc = LT // CHUNK
    chunks = lambda t: jnp.moveaxis(t.reshape((B, nc, CHUNK) + t.shape[2:]), 1, 0)

    def ret_step(S, qkv):
        o, S = retention_chunk(S, *qkv)
        return S, o

    S0 = jnp.zeros((B, RET_HEADS, RET_DK, RET_DV), h.dtype)
    S_fin, o_ret = lax.scan(ret_step, S0, (chunks(q), chunks(k), chunks(v)))
    o_ret = jnp.moveaxis(o_ret, 0, 1).reshape(B, LT, RET_HEADS, RET_DV)
    lam = diff_lambda(lq1, lk1, lq2, lk2, lam_init)
    qd = dq.reshape(B, LT, DIFF_HEADS, 2, DIFF_DH)
    kd = dk.reshape(B, LT, DIFF_HEADS, 2, DIFF_DH)
    vd = dv.reshape(B, LT, DIFF_HEADS, DIFF_DV)
    kchunk = jnp.floor_divide(pos, CHUNK)

    def attn_block(bi):
        start = bi * Q_BLOCK
        qb = lax.dynamic_slice_in_dim(qd, start, Q_BLOCK, axis=1)
        qpos = start + jnp.arange(Q_BLOCK) - PREFIX
        mask = (kchunk[None, :] <= jnp.floor_divide(qpos, CHUNK)[:, None]) & valid[None, :]
        bias = rel_bias.T[:, t5_bucket(pos[None, :] - qpos[:, None])]
        return diff_softmax_mix(qb, kd, vd, bias, mask, lam)

    o_diff = lax.map(attn_block, jnp.arange(LT // Q_BLOCK))
    o_diff = jnp.moveaxis(o_diff, 0, 1).reshape(B, LT, DIFF_HEADS, DIFF_DV)
    h = layer_tail(h, o_ret, rg, o_diff, gr, gd, lam_init, lw)
    keep = PREFIX - N_META
    k_rows = dk[:, keep:].reshape(B, LT - keep, DIFF_HEADS, 2 * DIFF_DH)
    v_rows = vd[:, keep:]
    return h, k_rows, v_rows, S_fin


def sample_layer(h, k_cache, v_cache, S, lw, lam_init, rel_bias):
    (n1, w_in, lq1, lk1, lq2, lk2) = lw[:6]
    B, L, _ = h.shape
    P = k_cache.shape[1]
    xn = rmsnorm(h, n1)
    rq, rk, rv, rg, dq, dk, dv, gr, gd = in_proj(xn, w_in)
    pos = P + jnp.arange(L)
    q = rotary(rq.reshape(B, L, RET_HEADS, RET_DK), pos) * (RET_DK ** -0.5)
    k = rotary(rk.reshape(B, L, RET_HEADS, RET_DK), pos)
    v = rv.reshape(B, L, RET_HEADS, RET_DV)
    o_ret, S_new = retention_chunk(S, q, k, v)
    lam = diff_lambda(lq1, lk1, lq2, lk2, lam_init)
    qd = dq.reshape(B, L, DIFF_HEADS, 2, DIFF_DH)
    kd = dk.reshape(B, L, DIFF_HEADS, 2, DIFF_DH)
    vd = dv.reshape(B, L, DIFF_HEADS, DIFF_DV)
    k_all = jnp.concatenate([k_cache.reshape(B, P, DIFF_HEADS, 2, DIFF_DH), kd], axis=1)
    v_all = jnp.concatenate([v_cache, vd], axis=1)
    kpos = jnp.arange(P + L)
    mask = jnp.floor_divide(kpos, CHUNK)[None, :] <= jnp.floor_divide(pos, CHUNK)[:, None]
    bias = rel_bias.T[:, t5_bucket(kpos[None, :] - pos[:, None])]
    o_diff = diff_softmax_mix(qd, k_all, v_all, bias, mask, lam)
    h = layer_tail(h, o_ret, rg, o_diff, gr, gd, lam_init, lw)
    return h, dk.reshape(B, L, DIFF_HEADS, 2 * DIFF_DH), vd, S_new


def setup_inputs(seed: int = 0) -> dict:
    key = jax.random.key(seed)
    ks = jax.random.split(key, 21)
    nrm = lambda k, shape, s: jax.random.normal(k, shape, jnp.float32) * s
    return {
        'x_prompt': nrm(ks[0], (BATCH, SEQ, D_MODEL), 1.0),
        'x_sample': nrm(ks[1], (DEC_BATCH, DEC_SEQ, D_MODEL), 1.0),
        'cache_k': nrm(ks[2], (DEPTH, DEC_BATCH, PAST_LEN, DIFF_HEADS, 2 * DIFF_DH), 1.0),
        'cache_v': nrm(ks[3], (DEPTH, DEC_BATCH, PAST_LEN, DIFF_HEADS, DIFF_DV), 1.0),
        'state_ret': nrm(ks[4], (DEPTH, DEC_BATCH, RET_HEADS, RET_DK, RET_DV), 4.0),
        'meta_tokens': nrm(ks[5], (N_META, D_MODEL), 1.0),
        'rel_bias': nrm(ks[6], (N_BUCKETS, DIFF_HEADS), 0.5),
        'norm1_g': 1.0 + nrm(ks[7], (DEPTH, D_MODEL), 0.02),
        'w_in': nrm(ks[8], (DEPTH, D_MODEL, W_IN), D_MODEL ** -0.5),
        'lambda_q1': nrm(ks[9], (DEPTH, DIFF_DH), 0.1),
        'lambda_k1': nrm(ks[10], (DEPTH, DIFF_DH), 0.1),
        'lambda_q2': nrm(ks[11], (DEPTH, DIFF_DH), 0.1),
        'lambda_k2': nrm(ks[12], (DEPTH, DIFF_DH), 0.1),
        'diff_subln_g': 1.0 + nrm(ks[13], (DEPTH, DIFF_DV), 0.02),
        'w_ret_branch': nrm(ks[14], (DEPTH, RET_V_W, D_MODEL), RET_V_W ** -0.5),
        'w_diff_branch': nrm(ks[15], (DEPTH, DIFF_W, D_MODEL), DIFF_W ** -0.5),
        'w_o': nrm(ks[16], (DEPTH, D_MODEL, D_MODEL), D_MODEL ** -0.5),
        'norm2_g': 1.0 + nrm(ks[17], (DEPTH, D_MODEL), 0.02),
        'w_ffn_up': nrm(ks[18], (DEPTH, D_MODEL, 2 * D_FF), D_MODEL ** -0.5),
        'w_ffn_down': nrm(ks[19], (DEPTH, D_FF, D_MODEL), D_FF ** -0.5),
        'normf_g': 1.0 + nrm(ks[20], (D_MODEL,), 0.02),
    }


def reference(x_prompt, x_sample, cache_k, cache_v, state_ret, meta_tokens, rel_bias, norm1_g, w_in,
              lambda_q1, lambda_k1, lambda_q2, lambda_k2, diff_subln_g, w_ret_branch, w_diff_branch,
              w_o, norm2_g, w_ffn_up, w_ffn_down, normf_g):
    B = x_prompt.shape[0]
    pad = jnp.zeros((B, PREFIX - N_META, D_MODEL), x_prompt.dtype)
    meta = jnp.broadcast_to(meta_tokens.astype(x_prompt.dtype)[None], (B, N_META, D_MODEL))
    h_p = jnp.concatenate([pad, meta, x_prompt], axis=1)
    h_s = x_sample
    kp, vp, sp, ksm, vsm, ssm = [], [], [], [], [], []
    for l in range(DEPTH):
        lw = (norm1_g[l], w_in[l], lambda_q1[l], lambda_k1[l], lambda_q2[l], lambda_k2[l], diff_subln_g[l],
              w_ret_branch[l], w_diff_branch[l], w_o[l], norm2_g[l], w_ffn_up[l], w_ffn_down[l])
        lam_init = 0.8 - 0.6 * math.exp(-0.3 * l)
        h_p, k_new, v_new, s_new = prompt_layer(h_p, lw, lam_init, rel_bias)
        kp.append(k_new)
        vp.append(v_new)
        sp.append(s_new)
        h_s, k_new, v_new, s_new = sample_layer(h_s, cache_k[l], cache_v[l], state_ret[l], lw, lam_init, rel_bias)
        ksm.append(k_new)
        vsm.append(v_new)
        ssm.append(s_new)
    y_prompt = rmsnorm(h_p[:, PREFIX:], normf_g)
    y_sample = rmsnorm(h_s, normf_g)
    return (y_prompt, y_sample, jnp.stack(kp), jnp.stack(vp), jnp.stack(sp), jnp.stack(ksm), jnp.stack(vsm), jnp.stack(ssm))
```

```python
import functools
import math

import numpy as np
import jax
import jax.numpy as jnp
from jax import lax
from jax.experimental import pallas as pl
from jax.experimental.pallas import tpu as pltpu

F32 = jnp.float32
BF16 = jnp.bfloat16

D_MODEL = 1024
CHUNK = 64
N_META = 16
RET_HEADS = 4
RET_DK = D_MODEL // 8
RET_DV = 2 * RET_DK
RET_QK_W = RET_HEADS * RET_DK
RET_V_W = RET_HEADS * RET_DV
ROPE_BASE = 10000.0
RET_EPS = 1e-6
DIFF_HEADS = 8
DIFF_DH = D_MODEL // 16
DIFF_DV = 2 * DIFF_DH
DIFF_W = DIFF_HEADS * DIFF_DV
DIFF_EPS = 1e-5
N_BUCKETS = 32
MAX_DISTANCE = 128
D_FF = ((8 * D_MODEL + 3 * 256 - 1) // (3 * 256)) * 256
NORM_EPS = 1e-6
NEG_INF = -1e30
LAM_INIT = 0.8 - 0.6 * math.exp(-0.3 * 0)

OFF_RQ = 0
OFF_RK = OFF_RQ + RET_QK_W
OFF_RV = OFF_RK + RET_QK_W
OFF_RG = OFF_RV + RET_V_W
OFF_DQ = OFF_RG + RET_V_W
OFF_DK = OFF_DQ + DIFF_W
OFF_DV = OFF_DK + DIFF_W
OFF_GR = OFF_DV + DIFF_W
OFF_GD = OFF_GR + D_MODEL
W_IN = OFF_GD + D_MODEL

LANES = 128
VMEM_LIMIT_BYTES = 56 * 1024 * 1024
TM_PROJ = 256
TM_TAIL = 256
T_RET = 128
TQ = 256
TK = 256
FF_CHUNK = 512

NT_DIMS = (((1,), (1,)), ((), ()))
TN_DIMS = (((0,), (0,)), ((), ()))


def _cparams(n_axes):
    return pltpu.CompilerParams(dimension_semantics=("arbitrary",) * n_axes,
                                vmem_limit_bytes=VMEM_LIMIT_BYTES)


def _resident(shape):
    nd = len(shape)
    return pl.BlockSpec(shape, lambda *_: (0,) * nd, pipeline_mode=pl.Buffered(1))


def _t5_bucket_np(rel):
    nb = N_BUCKETS // 2
    max_exact = nb // 2
    ret = np.where(rel > 0, nb, 0)
    n = np.abs(rel)
    nf = np.maximum(n, max_exact).astype(np.float64)
    large = max_exact + (np.log(nf / max_exact) / math.log(MAX_DISTANCE / max_exact)
                         * (nb - max_exact)).astype(np.int32)
    large = np.minimum(large, nb - 1)
    return (ret + np.where(n < max_exact, n, large)).astype(np.int32)


def _bias_idx(qpos, kpos, n_real):
    qpos = np.asarray(qpos)[:, None]
    kpos = np.asarray(kpos)[None, :]
    vis = (np.floor_divide(kpos, CHUNK) <= np.floor_divide(qpos, CHUNK))
    vis = vis & (np.arange(kpos.shape[1])[None, :] < n_real)
    return np.where(vis, _t5_bucket_np(kpos - qpos), -1).astype(np.int32)


def _rotary_tables(pos):
    half = RET_DK // 2
    freq = 1.0 / (ROPE_BASE ** jnp.linspace(0.0, 1.0, half, dtype=F32))
    ang = jnp.asarray(pos, F32)[:, None] * freq[None, :]
    cos, sin = jnp.cos(ang), jnp.sin(ang)
    return jnp.concatenate([cos, cos], axis=1), jnp.concatenate([-sin, sin], axis=1)


def _retention_tables(t):
    gam = 1.0 - 2.0 ** (-5.0 - np.arange(RET_HEADS, dtype=np.float64))
    lg = np.log(gam)[:, None, None]
    n = np.arange(t)[:, None]
    m = np.arange(t)[None, :]
    cn, cm = n // CHUNK, m // CHUNK
    expo = np.where(cm == cn, np.abs(n - m), n - m).astype(np.float64)[None]
    dmat = np.where((cm <= cn)[None], np.exp(lg * expo), 0.0)
    qdec = np.exp(lg[:, :, 0] * (np.arange(t)[None, :] + 1.0))[:, :, None]
    kdec = np.exp(lg[:, :, 0] * (t - 1.0 - np.arange(t)[None, :]))[:, :, None]
    sdec = np.exp(lg[:, 0, 0] * t)
    return (jnp.asarray(dmat, F32), jnp.asarray(qdec, F32), jnp.asarray(kdec, F32),
            tuple(float(s) for s in sdec))


def _bias_kernel(rb_ref, *refs, n_tiles, shifted):
    idx_refs, out_refs = refs[:n_tiles], refs[n_tiles:]
    h = pl.program_id(0)
    far = rb_ref[N_BUCKETS // 2 - 1, h]
    for idx_ref, out_ref, sh in zip(idx_refs, out_refs, shifted):
        idx = idx_ref[...]
        acc = jnp.full(idx.shape, NEG_INF, F32)
        for b in range(N_BUCKETS):
            val = rb_ref[b, h] - far if sh else rb_ref[b, h]
            acc = jnp.where(idx == b, val, acc)
        out_ref[0] = acc


def _bias_tiles(rel_bias, idx_list, shifted):
    n = len(idx_list)
    in_specs = [pl.BlockSpec(memory_space=pltpu.SMEM)]
    in_specs += [pl.BlockSpec(ix.shape, lambda h: (0, 0)) for ix in idx_list]
    out_specs = [pl.BlockSpec((1,) + ix.shape, lambda h: (h, 0, 0)) for ix in idx_list]
    out_shape = [jax.ShapeDtypeStruct((DIFF_HEADS,) + ix.shape, F32) for ix in idx_list]
    return pl.pallas_call(
        functools.partial(_bias_kernel, n_tiles=n, shifted=tuple(shifted)),
        grid=(DIFF_HEADS,), in_specs=in_specs, out_specs=out_specs, out_shape=out_shape,
        compiler_params=_cparams(1), name="bias_tiles",
    )(rel_bias, *[jnp.asarray(ix) for ix in idx_list])


def _inproj_kernel(x_ref, g_ref, w_ref, cs_ref, sn_ref,
                   rq_ref, rk_ref, rv_ref, rg_ref, dq_ref, dkb_ref, dkf_ref,
                   dvb_ref, dvf_ref, gr_ref, gd_ref):
    x = x_ref[...]
    xn = x * lax.rsqrt(jnp.mean(x * x, axis=-1, keepdims=True) + NORM_EPS) * g_ref[...]
    xn = xn.astype(BF16)

    def proj(lo, n):
        return jnp.dot(xn, w_ref[:, lo:lo + n], preferred_element_type=F32)

    cs = cs_ref[...]
    sn = sn_ref[...]

    def rotary(u):
        return u * cs + pltpu.roll(u, RET_DK // 2, 1) * sn

    uq = proj(OFF_RQ, RET_QK_W)
    uk = proj(OFF_RK, RET_QK_W)
    for hh in range(RET_HEADS):
        sl = slice(hh * RET_DK, (hh + 1) * RET_DK)
        rq_ref[:, sl] = (rotary(uq[:, sl]) * (RET_DK ** -0.5)).astype(BF16)
        rk_ref[:, sl] = rotary(uk[:, sl]).astype(BF16)
    rv_ref[...] = proj(OFF_RV, RET_V_W).astype(BF16)
    rg_ref[...] = proj(OFF_RG, RET_V_W)
    dq_ref[...] = (proj(OFF_DQ, DIFF_W) * (DIFF_DH ** -0.5)).astype(BF16)
    dk = proj(OFF_DK, DIFF_W)
    dkf_ref[...] = dk
    dkb_ref[...] = dk.astype(BF16)
    dv = proj(OFF_DV, DIFF_W)
    dvf_ref[...] = dv
    dvb_ref[...] = dv.astype(BF16)
    gr_ref[...] = proj(OFF_GR, D_MODEL)
    gd_ref[...] = proj(OFF_GD, D_MODEL)


def _in_proj(x2d, g, w_bf, cs, sn, tm):
    rows = x2d.shape[0]
    n_pos = cs.shape[0] // tm
    row = lambda w: pl.BlockSpec((tm, w), lambda i: (i, 0))
    pos = pl.BlockSpec((tm, LANES), lambda i: (i % n_pos, 0))
    outs = [(RET_QK_W, BF16), (RET_QK_W, BF16), (RET_V_W, BF16), (RET_V_W, F32),
            (DIFF_W, BF16), (DIFF_W, BF16), (DIFF_W, F32), (DIFF_W, BF16), (DIFF_W, F32),
            (D_MODEL, F32), (D_MODEL, F32)]
    return pl.pallas_call(
        _inproj_kernel, grid=(rows // tm,),
        in_specs=[row(D_MODEL), _resident((1, D_MODEL)), _resident((D_MODEL, W_IN)), pos, pos],
        out_specs=[row(w) for w, _ in outs],
        out_shape=[jax.ShapeDtypeStruct((rows, w), dt) for w, dt in outs],
        compiler_params=_cparams(1), name="in_proj",
    )(x2d, g, w_bf, cs, sn)


def _retention_kernel(rq_ref, rk_ref, rv_ref, rg_ref, s0_ref, dm_ref, qd_ref, kd_ref,
                      y_ref, sfin_ref, s_scr, *, sdec, with_output):
    t = pl.program_id(1)

    @pl.when(t == 0)
    def _():
        s_scr[...] = s0_ref[0]

    for hh in range(RET_HEADS):
        qs = slice(hh * RET_DK, (hh + 1) * RET_DK)
        vs = slice(hh * RET_DV, (hh + 1) * RET_DV)
        q = rq_ref[:, qs]
        k = rk_ref[:, qs]
        v = rv_ref[:, vs]
        state = s_scr[hh]
        if with_output:
            s = lax.dot_general(q, k, NT_DIMS, preferred_element_type=F32) * dm_ref[hh]
            o = jnp.dot(s.astype(BF16), v, preferred_element_type=F32)
            qd = (q.astype(F32) * qd_ref[hh]).astype(BF16)
            o = o + jnp.dot(qd, state.astype(BF16), preferred_element_type=F32)
            y = o * lax.rsqrt(jnp.mean(o * o, axis=-1, keepdims=True) + RET_EPS)
            g = rg_ref[:, vs]
            y_ref[:, vs] = (y * (g * jax.nn.sigmoid(g))).astype(BF16)
        else:
            y_ref[:, vs] = jnp.zeros((y_ref.shape[0], RET_DV), BF16)
        kd = (k.astype(F32) * kd_ref[hh]).astype(BF16)
        kv = lax.dot_general(kd, v, TN_DIMS, preferred_element_type=F32)
        s_scr[hh] = sdec[hh] * state + kv

    @pl.when(t == pl.num_programs(1) - 1)
    def _():
        sfin_ref[0] = s_scr[...]


def _retention(rq, rk, rv, rg, s0, n_seq, t, with_output=True):
    rows = rq.shape[0]
    nt = rows // n_seq // t
    dmat, qdec, kdec, sdec = _retention_tables(t)
    row = lambda w: pl.BlockSpec((t, w), lambda b, i: (b * nt + i, 0))
    s_shape = (1, RET_HEADS, RET_DK, RET_DV)
    if s0.shape[0] == 1:
        s0_spec = pl.BlockSpec(s_shape, lambda b, i: (0, 0, 0, 0))
    else:
        s0_spec = pl.BlockSpec(s_shape, lambda b, i: (b, 0, 0, 0))
    y, sfin = pl.pallas_call(
        functools.partial(_retention_kernel, sdec=sdec, with_output=with_output),
        grid=(n_seq, nt),
        in_specs=[row(RET_QK_W), row(RET_QK_W), row(RET_V_W), row(RET_V_W), s0_spec,
                  _resident(dmat.shape), _resident(qdec.shape), _resident(kdec.shape)],
        out_specs=[row(RET_V_W), pl.BlockSpec(s_shape, lambda b, i: (b, 0, 0, 0))],
        out_shape=[jax.ShapeDtypeStruct((rows, RET_V_W), BF16),
                   jax.ShapeDtypeStruct((n_seq,) + s_shape[1:], F32)],
        scratch_shapes=[pltpu.VMEM(s_shape[1:], F32)],
        compiler_params=_cparams(2), name="retention",
    )(rq, rk, rv, rg, s0, dmat, qdec, kdec)
    return y, sfin


def _lambda(lam_ref):
    lp = lam_ref[...]
    a = jnp.exp(jnp.sum(lp[0:1] * lp[1:2], axis=-1, keepdims=True))
    b = jnp.exp(jnp.sum(lp[2:3] * lp[3:4], axis=-1, keepdims=True))
    return a - b + LAM_INIT


def _stack_maps(q):
    qf = q.astype(F32)
    lane = lax.broadcasted_iota(jnp.int32, q.shape, 1)
    return jnp.concatenate([jnp.where(lane < DIFF_DH, qf, 0.0),
                            jnp.where(lane < DIFF_DH, 0.0, qf)], axis=0).astype(BF16)


def _mix_and_norm(o_all, lam, subg, n):
    o = o_all[:n] - lam * o_all[n:]
    y = o * lax.rsqrt(jnp.mean(o * o, axis=-1, keepdims=True) + DIFF_EPS) * subg
    return (y * (1.0 - LAM_INIT)).astype(BF16)


def _diff_attn_kernel(q_ref, k_ref, v_ref, mk_ref, mv_ref, tdiag_ref, tsub_ref, tmeta_ref,
                      lam_ref, subg_ref, y_ref, qq_scr, m_scr, l_scr, acc_scr):
    i = pl.program_id(2)
    qq_scr[...] = _stack_maps(q_ref[...])

    def scores(k_blk, tile):
        s = lax.dot_general(qq_scr[...], k_blk, NT_DIMS, preferred_element_type=F32)
        if tile is not None:
            s = (s.reshape(2, TQ, s.shape[-1]) + tile[None]).reshape(2 * TQ, s.shape[-1])
        return s

    def update(k_blk, v_blk, tile):
        s = scores(k_blk, tile)
        m_prev = m_scr[...]
        m_new = jnp.maximum(m_prev, jnp.max(s, axis=1, keepdims=True))
        alpha = jnp.exp(m_prev - m_new)
        p = jnp.exp(s - m_new)
        l_scr[...] = alpha * l_scr[...] + jnp.sum(p, axis=1, keepdims=True)
        acc_scr[...] = alpha * acc_scr[...] + jnp.dot(p.astype(BF16), v_blk,
                                                      preferred_element_type=F32)
        m_scr[...] = m_new

    s = scores(mk_ref[...], tmeta_ref[0, jnp.minimum(i, 1)])
    m0 = jnp.max(s, axis=1, keepdims=True)
    p = jnp.exp(s - m0)
    m_scr[...] = m0
    l_scr[...] = jnp.sum(p, axis=1, keepdims=True)
    acc_scr[...] = jnp.dot(p.astype(BF16), mv_ref[...], preferred_element_type=F32)

    def far_block(j, carry):
        off = pl.multiple_of(j * TK, TK)
        update(k_ref[pl.ds(off, TK), :], v_ref[pl.ds(off, TK), :], None)
        return carry

    lax.fori_loop(0, jnp.maximum(i - 1, 0), far_block, 0)

    @pl.when(i >= 1)
    def _():
        off = pl.multiple_of((i - 1) * TK, TK)
        update(k_ref[pl.ds(off, TK), :], v_ref[pl.ds(off, TK), :], tsub_ref[0])

    off = pl.multiple_of(i * TK, TK)
    update(k_ref[pl.ds(off, TK), :], v_ref[pl.ds(off, TK), :], tdiag_ref[0])

    o_all = acc_scr[...] / l_scr[...]
    y_ref[...] = _mix_and_norm(o_all, _lambda(lam_ref), subg_ref[...], TQ)


def _diff_attn(dq, dk, dv, mk, mv, tdiag, tsub, tmeta, lam_p, subg, n_seq):
    rows = dq.shape[0]
    seq = rows // n_seq
    nq = seq // TQ
    head = lambda b, h, i: (0, h)
    tile3 = lambda b, h, i: (h, 0, 0)
    return pl.pallas_call(
        _diff_attn_kernel, grid=(n_seq, DIFF_HEADS, nq),
        in_specs=[pl.BlockSpec((TQ, LANES), lambda b, h, i: (b * nq + i, h)),
                  pl.BlockSpec((seq, LANES), lambda b, h, i: (b, h)),
                  pl.BlockSpec((seq, LANES), lambda b, h, i: (b, h)),
                  pl.BlockSpec((LANES, LANES), head),
                  pl.BlockSpec((LANES, LANES), head),
                  pl.BlockSpec((1, TQ, TK), tile3),
                  pl.BlockSpec((1, TQ, TK), tile3),
                  pl.BlockSpec((1, 2, TQ, LANES), lambda b, h, i: (h, 0, 0, 0)),
                  pl.BlockSpec((4, DIFF_DH), lambda b, h, i: (0, 0)),
                  pl.BlockSpec((1, DIFF_DV), lambda b, h, i: (0, 0))],
        out_specs=pl.BlockSpec((TQ, LANES), lambda b, h, i: (b * nq + i, h)),
        out_shape=jax.ShapeDtypeStruct((rows, DIFF_W), BF16),
        scratch_shapes=[pltpu.VMEM((2 * TQ, LANES), BF16), pltpu.VMEM((2 * TQ, 1), F32),
                        pltpu.VMEM((2 * TQ, 1), F32), pltpu.VMEM((2 * TQ, LANES), F32)],
        compiler_params=_cparams(3), name="diff_attn",
    )(dq, dk, dv, mk, mv, tdiag, tsub, tmeta, lam_p, subg)


def _sample_attn_kernel(q_ref, ck_ref, cv_ref, nk_ref, nv_ref, tile_ref, lam_ref, subg_ref,
                        y_ref, k_scr, v_scr, *, past, n_new):
    pad = k_scr.shape[0] - past - n_new
    k_scr[0:past] = ck_ref[...].astype(BF16)
    v_scr[0:past] = cv_ref[...].astype(BF16)
    k_scr[past:past + n_new] = nk_ref[...]
    v_scr[past:past + n_new] = nv_ref[...]
    k_scr[past + n_new:] = jnp.zeros((pad, LANES), BF16)
    v_scr[past + n_new:] = jnp.zeros((pad, LANES), BF16)
    qq = _stack_maps(q_ref[...])
    s = lax.dot_general(qq, k_scr[...], NT_DIMS, preferred_element_type=F32)
    s = (s.reshape(2, n_new, s.shape[-1]) + tile_ref[0][None]).reshape(2 * n_new, s.shape[-1])
    m = jnp.max(s, axis=1, keepdims=True)
    p = jnp.exp(s - m)
    l = jnp.sum(p, axis=1, keepdims=True)
    o_all = jnp.dot(p.astype(BF16), v_scr[...], preferred_element_type=F32) / l
    y_ref[...] = _mix_and_norm(o_all, _lambda(lam_ref), subg_ref[...], n_new)


def _sample_attn(dq, ck, cv, nk, nv, tile, lam_p, subg, n_seq, past, n_new):
    kpad = tile.shape[-1]
    return pl.pallas_call(
        functools.partial(_sample_attn_kernel, past=past, n_new=n_new),
        grid=(n_seq, DIFF_HEADS),
        in_specs=[pl.BlockSpec((n_new, LANES), lambda b, h: (b, h)),
                  pl.BlockSpec((past, LANES), lambda b, h: (b, h)),
                  pl.BlockSpec((past, LANES), lambda b, h: (b, h)),
                  pl.BlockSpec((n_new, LANES), lambda b, h: (b, h)),
                  pl.BlockSpec((n_new, LANES), lambda b, h: (b, h)),
                  pl.BlockSpec((1, n_new, kpad), lambda b, h: (h, 0, 0)),
                  pl.BlockSpec((4, DIFF_DH), lambda b, h: (0, 0)),
                  pl.BlockSpec((1, DIFF_DV), lambda b, h: (0, 0))],
        out_specs=pl.BlockSpec((n_new, LANES), lambda b, h: (b, h)),
        out_shape=jax.ShapeDtypeStruct((n_seq * n_new, DIFF_W), BF16),
        scratch_shapes=[pltpu.VMEM((kpad, LANES), BF16), pltpu.VMEM((kpad, LANES), BF16)],
        compiler_params=_cparams(2), name="sample_attn",
    )(dq, ck, cv, nk, nv, tile, lam_p, subg)


def _tail_kernel(h_ref, yr_ref, yd_ref, gr_ref, gd_ref, wrb_ref, wdb_ref, wo_ref, n2_ref,
                 wup_ref, wdn_ref, nf_ref, out_ref):
    a = jnp.dot(yr_ref[...], wrb_ref[...], preferred_element_type=F32)
    b = jnp.dot(yd_ref[...], wdb_ref[...], preferred_element_type=F32)
    merged = jax.nn.sigmoid(gr_ref[...]) * a + jax.nn.sigmoid(gd_ref[...]) * b
    h = h_ref[...] + jnp.dot(merged.astype(BF16), wo_ref[...], preferred_element_type=F32)
    xn = h * lax.rsqrt(jnp.mean(h * h, axis=-1, keepdims=True) + NORM_EPS) * n2_ref[...]
    xn = xn.astype(BF16)
    acc = jnp.zeros_like(h)
    for lo in range(0, D_FF, FF_CHUNK):
        n = min(FF_CHUNK, D_FF - lo)
        gate = jnp.dot(xn, wup_ref[:, lo:lo + n], preferred_element_type=F32)
        up = jnp.dot(xn, wup_ref[:, D_FF + lo:D_FF + lo + n], preferred_element_type=F32)
        act = (gate * jax.nn.sigmoid(gate) * up).astype(BF16)
        acc = acc + jnp.dot(act, wdn_ref[lo:lo + n, :], preferred_element_type=F32)
    h = h + acc
    out_ref[...] = h * lax.rsqrt(jnp.mean(h * h, axis=-1, keepdims=True) + NORM_EPS) * nf_ref[...]


def _tail(h2d, y_ret, y_diff, gr, gd, wrb, wdb, wo, n2, wup, wdn, nf, tm):
    rows = h2d.shape[0]
    row = pl.BlockSpec((tm, D_MODEL), lambda i: (i, 0))
    return pl.pallas_call(
        _tail_kernel, grid=(rows // tm,),
        in_specs=[row, row, row, row, row,
                  _resident(wrb.shape), _resident(wdb.shape), _resident(wo.shape),
                  _resident(n2.shape), _resident(wup.shape), _resident(wdn.shape),
                  _resident(nf.shape)],
        out_specs=row,
        out_shape=jax.ShapeDtypeStruct((rows, D_MODEL), F32),
        compiler_params=_cparams(1), name="tail",
    )(h2d, y_ret, y_diff, gr, gd, wrb, wdb, wo, n2, wup, wdn, nf)


def kernel(x_prompt, x_sample, cache_k, cache_v, state_ret, meta_tokens, rel_bias, norm1_g, w_in,
           lambda_q1, lambda_k1, lambda_q2, lambda_k2, diff_subln_g, w_ret_branch, w_diff_branch,
           w_o, norm2_g, w_ffn_up, w_ffn_down, normf_g):
    assert w_in.shape[0] == 1, "single-layer step only"
    bsz, seq, _ = x_prompt.shape
    dbsz, dseq, _ = x_sample.shape
    past = cache_k.shape[2]
    assert seq % TQ == 0 and seq % T_RET == 0 and (bsz * seq) % TM_PROJ == 0
    assert dseq <= CHUNK and past % CHUNK == 0 and meta_tokens.shape[0] == N_META

    w_in_bf = w_in[0].astype(BF16)
    g1 = norm1_g[0][None, :]
    lam_p = jnp.stack([lambda_q1[0], lambda_k1[0], lambda_q2[0], lambda_k2[0]])
    subg = diff_subln_g[0][None, :]
    wts = (w_ret_branch[0].astype(BF16), w_diff_branch[0].astype(BF16), w_o[0].astype(BF16),
           norm2_g[0][None, :], w_ffn_up[0].astype(BF16), w_ffn_down[0].astype(BF16),
           normf_g[None, :])

    kpad = ((past + dseq + LANES - 1) // LANES) * LANES
    idx_diag = _bias_idx(np.arange(TQ), np.arange(TK), TK)
    idx_sub = _bias_idx(np.arange(TQ) + TK, np.arange(TK), TK)
    idx_meta = np.concatenate([_bias_idx(np.arange(TQ) + b * TQ, np.arange(LANES) - N_META, N_META)
                               for b in range(2)], axis=0)
    idx_samp = _bias_idx(past + np.arange(dseq), np.arange(kpad), past + dseq)
    t_diag, t_sub, t_meta, t_samp = _bias_tiles(rel_bias, [idx_diag, idx_sub, idx_meta, idx_samp],
                                                [True, True, True, False])
    t_meta = t_meta.reshape(DIFF_HEADS, 2, TQ, LANES)

    cs_m, sn_m = _rotary_tables(np.arange(-N_META, 0))
    m_out = _in_proj(meta_tokens, g1, w_in_bf, cs_m, sn_m, N_META)
    zero_state = jnp.zeros((1, RET_HEADS, RET_DK, RET_DV), F32)
    _, s_meta = _retention(m_out[0], m_out[1], m_out[2], m_out[3], zero_state, 1, N_META,
                           with_output=False)
    pad_rows = ((0, LANES - N_META), (0, 0))
    mk = jnp.pad(m_out[5], pad_rows)
    mv = jnp.pad(m_out[7], pad_rows)

    cs_p, sn_p = _rotary_tables(np.arange(seq))
    x2d = x_prompt.reshape(bsz * seq, D_MODEL)
    (rq, rk, rv, rg, dq, dkb, dkf, dvb, dvf, gr, gd) = _in_proj(x2d, g1, w_in_bf, cs_p, sn_p, TM_PROJ)
    y_ret, s_fin = _retention(rq, rk, rv, rg, s_meta, bsz, T_RET)
    y_diff = _diff_attn(dq, dkb, dvb, mk, mv, t_diag, t_sub, t_meta, lam_p, subg, bsz)
    y_prompt = _tail(x2d, y_ret, y_diff, gr, gd, *wts, TM_TAIL).reshape(bsz, seq, D_MODEL)
    k_rows = jnp.concatenate([jnp.broadcast_to(m_out[6][None], (bsz, N_META, DIFF_W)),
                              dkf.reshape(bsz, seq, DIFF_W)], axis=1)
    v_rows = jnp.concatenate([jnp.broadcast_to(m_out[8][None], (bsz, N_META, DIFF_W)),
                              dvf.reshape(bsz, seq, DIFF_W)], axis=1)

    cs_s, sn_s = _rotary_tables(np.tile(past + np.arange(dseq), dbsz))
    xs2d = x_sample.reshape(dbsz * dseq, D_MODEL)
    (rq, rk, rv, rg, dq, dkb, dkf_s, dvb, dvf_s, gr, gd) = _in_proj(xs2d, g1, w_in_bf, cs_s, sn_s,
                                                                     dbsz * dseq)
    y_ret, s_samp = _retention(rq, rk, rv, rg, state_ret[0], dbsz, dseq)
    ck = cache_k[0].reshape(dbsz * past, DIFF_W)
    cv = cache_v[0].reshape(dbsz * past, DIFF_W)
    y_diff = _sample_attn(dq, ck, cv, dkb, dvb, t_samp, lam_p, subg, dbsz, past, dseq)
    y_sample = _tail(xs2d, y_ret, y_diff, gr, gd, *wts, dbsz * dseq).reshape(dbsz, dseq, D_MODEL)

    heads = (DIFF_HEADS, DIFF_DV)
    return (y_prompt, y_sample,
            k_rows.reshape((1, bsz, N_META + seq) + heads),
            v_rows.reshape((1, bsz, N_META + seq) + heads),
            s_fin[None],
            dkf_s.reshape((1, dbsz, dseq) + heads),
            dvf_s.reshape((1, dbsz, dseq) + heads),
            s_samp[None])
```

```python
import functools
import math

import numpy as np
import jax
import jax.numpy as jnp
from jax import lax
from jax.experimental import pallas as pl
from jax.experimental.pallas import tpu as pltpu

F32 = jnp.float32
BF16 = jnp.bfloat16

D_MODEL = 1024
CHUNK = 64
N_META = 16
RET_HEADS = 4
RET_DK = D_MODEL // 8
RET_DV = 2 * RET_DK
RET_QK_W = RET_HEADS * RET_DK
RET_V_W = RET_HEADS * RET_DV
ROPE_BASE = 10000.0
RET_EPS = 1e-6
DIFF_HEADS = 8
DIFF_DH = D_MODEL // 16
DIFF_DV = 2 * DIFF_DH
DIFF_W = DIFF_HEADS * DIFF_DV
DIFF_EPS = 1e-5
N_BUCKETS = 32
MAX_DISTANCE = 128
D_FF = ((8 * D_MODEL + 3 * 256 - 1) // (3 * 256)) * 256
NORM_EPS = 1e-6
NEG_INF = -1e30
LAM_INIT = 0.8 - 0.6 * math.exp(-0.3 * 0)

OFF_RQ = 0
OFF_RK = OFF_RQ + RET_QK_W
OFF_RV = OFF_RK + RET_QK_W
OFF_RG = OFF_RV + RET_V_W
OFF_DQ = OFF_RG + RET_V_W
OFF_DK = OFF_DQ + DIFF_W
OFF_DV = OFF_DK + DIFF_W
OFF_GR = OFF_DV + DIFF_W
OFF_GD = OFF_GR + D_MODEL
W_IN = OFF_GD + D_MODEL

LANES = 128
VMEM_LIMIT_BYTES = 56 * 1024 * 1024
TM_PROJ = 256
TM_TAIL = 256
T_RET = 128
TQ = 256
TK = 256
HEADS_PER_STEP = 4
FF_CHUNK = 512

NT_DIMS = (((1,), (1,)), ((), ()))
TN_DIMS = (((0,), (0,)), ((), ()))


def _cparams(n_axes):
    return pltpu.CompilerParams(dimension_semantics=("arbitrary",) * n_axes,
                                vmem_limit_bytes=VMEM_LIMIT_BYTES)


def _resident(shape):
    nd = len(shape)
    return pl.BlockSpec(shape, lambda *_: (0,) * nd, pipeline_mode=pl.Buffered(1))


def _t5_bucket_np(rel):
    nb = N_BUCKETS // 2
    max_exact = nb // 2
    ret = np.where(rel > 0, nb, 0)
    n = np.abs(rel)
    nf = np.maximum(n, max_exact).astype(np.float64)
    large = max_exact + (np.log(nf / max_exact) / math.log(MAX_DISTANCE / max_exact)
                         * (nb - max_exact)).astype(np.int32)
    large = np.minimum(large, nb - 1)
    return (ret + np.where(n < max_exact, n, large)).astype(np.int32)


def _bias_idx(qpos, kpos, n_real):
    qpos = np.asarray(qpos)[:, None]
    kpos = np.asarray(kpos)[None, :]
    vis = (np.floor_divide(kpos, CHUNK) <= np.floor_divide(qpos, CHUNK))
    vis = vis & (np.arange(kpos.shape[1])[None, :] < n_real)
    return np.where(vis, _t5_bucket_np(kpos - qpos), -1).astype(np.int32)


def _rotary_tables(pos):
    half = RET_DK // 2
    freq = 1.0 / (ROPE_BASE ** jnp.linspace(0.0, 1.0, half, dtype=F32))
    ang = jnp.asarray(pos, F32)[:, None] * freq[None, :]
    cos, sin = jnp.cos(ang), jnp.sin(ang)
    return jnp.concatenate([cos, cos], axis=1), jnp.concatenate([-sin, sin], axis=1)


def _retention_tables(t):
    gam = 1.0 - 2.0 ** (-5.0 - np.arange(RET_HEADS, dtype=np.float64))
    lg = np.log(gam)[:, None, None]
    n = np.arange(t)[:, None]
    m = np.arange(t)[None, :]
    cn, cm = n // CHUNK, m // CHUNK
    expo = np.where(cm == cn, np.abs(n - m), n - m).astype(np.float64)[None]
    dmat = np.where((cm <= cn)[None], np.exp(lg * expo), 0.0)
    qdec = np.exp(lg[:, :, 0] * (np.arange(t)[None, :] + 1.0))[:, :, None]
    kdec = np.exp(lg[:, :, 0] * (t - 1.0 - np.arange(t)[None, :]))[:, :, None]
    sdec = np.exp(lg[:, 0, 0] * t)
    return (jnp.asarray(dmat, F32), jnp.asarray(qdec, F32), jnp.asarray(kdec, F32),
            tuple(float(s) for s in sdec))


def _bias_kernel(rb_ref, *refs, n_tiles, shifted):
    idx_refs, out_refs = refs[:n_tiles], refs[n_tiles:]
    h = pl.program_id(0)
    far = rb_ref[N_BUCKETS // 2 - 1, h]
    for idx_ref, out_ref, sh in zip(idx_refs, out_refs, shifted):
        idx = idx_ref[...]
        acc = jnp.full(idx.shape, NEG_INF, F32)
        for b in range(N_BUCKETS):
            val = rb_ref[b, h] - far if sh else rb_ref[b, h]
            acc = jnp.where(idx == b, val, acc)
        out_ref[0] = acc


def _bias_tiles(rel_bias, idx_list, shifted):
    n = len(idx_list)
    in_specs = [pl.BlockSpec(memory_space=pltpu.SMEM)]
    in_specs += [pl.BlockSpec(ix.shape, lambda h: (0, 0)) for ix in idx_list]
    out_specs = [pl.BlockSpec((1,) + ix.shape, lambda h: (h, 0, 0)) for ix in idx_list]
    out_shape = [jax.ShapeDtypeStruct((DIFF_HEADS,) + ix.shape, F32) for ix in idx_list]
    return pl.pallas_call(
        functools.partial(_bias_kernel, n_tiles=n, shifted=tuple(shifted)),
        grid=(DIFF_HEADS,), in_specs=in_specs, out_specs=out_specs, out_shape=out_shape,
        compiler_params=_cparams(1), name="bias_tiles",
    )(rel_bias, *[jnp.asarray(ix) for ix in idx_list])


def _inproj_kernel(x_ref, g_ref, w_ref, cs_ref, sn_ref,
                   rq_ref, rk_ref, rv_ref, rg_ref, dq_ref, dkb_ref, dkf_ref,
                   dvb_ref, dvf_ref, gr_ref, gd_ref, *, transposed):
    x = x_ref[...]
    xn = x * lax.rsqrt(jnp.mean(x * x, axis=-1, keepdims=True) + NORM_EPS) * g_ref[...]
    xn = xn.astype(BF16)

    def proj(lo, n):
        return jnp.dot(xn, w_ref[:, lo:lo + n], preferred_element_type=F32)

    cs = cs_ref[...]
    sn = sn_ref[...]

    def rotary(u):
        return u * cs + pltpu.roll(u, RET_DK // 2, 1) * sn

    uq = proj(OFF_RQ, RET_QK_W)
    uk = proj(OFF_RK, RET_QK_W)
    for hh in range(RET_HEADS):
        sl = slice(hh * RET_DK, (hh + 1) * RET_DK)
        rq_ref[:, sl] = (rotary(uq[:, sl]) * (RET_DK ** -0.5)).astype(BF16)
        rk_ref[:, sl] = rotary(uk[:, sl]).astype(BF16)
    rv_ref[...] = proj(OFF_RV, RET_V_W).astype(BF16)
    rg_ref[...] = proj(OFF_RG, RET_V_W)
    dq = proj(OFF_DQ, DIFF_W) * (DIFF_DH ** -0.5)
    dk = proj(OFF_DK, DIFF_W)
    dkf_ref[...] = dk
    dkb_ref[...] = dk.astype(BF16)
    dv = proj(OFF_DV, DIFF_W)
    dvf_ref[...] = dv
    if transposed:
        dq_ref[0] = dq.T.astype(BF16)
        dvb_ref[0] = dv.T.astype(BF16)
    else:
        dq_ref[...] = dq.astype(BF16)
        dvb_ref[...] = dv.astype(BF16)
    gr_ref[...] = proj(OFF_GR, D_MODEL)
    gd_ref[...] = proj(OFF_GD, D_MODEL)


def _in_proj(x2d, g, w_bf, cs, sn, tm, transposed=False):
    rows = x2d.shape[0]
    n_pos = cs.shape[0] // tm
    row = lambda w: pl.BlockSpec((tm, w), lambda i: (i, 0))
    pos = pl.BlockSpec((tm, LANES), lambda i: (i % n_pos, 0))
    outs = [(RET_QK_W, BF16), (RET_QK_W, BF16), (RET_V_W, BF16), (RET_V_W, F32),
            (DIFF_W, BF16), (DIFF_W, BF16), (DIFF_W, F32), (DIFF_W, BF16), (DIFF_W, F32),
            (D_MODEL, F32), (D_MODEL, F32)]
    out_specs = [row(w) for w, _ in outs]
    out_shape = [jax.ShapeDtypeStruct((rows, w), dt) for w, dt in outs]
    if transposed:
        for o in (4, 7):
            out_specs[o] = pl.BlockSpec((1, DIFF_W, tm), lambda i: (i, 0, 0))
            out_shape[o] = jax.ShapeDtypeStruct((rows // tm, DIFF_W, tm), BF16)
    return pl.pallas_call(
        functools.partial(_inproj_kernel, transposed=transposed), grid=(rows // tm,),
        in_specs=[row(D_MODEL), _resident((1, D_MODEL)), _resident((D_MODEL, W_IN)), pos, pos],
        out_specs=out_specs, out_shape=out_shape,
        compiler_params=_cparams(1), name="in_proj",
    )(x2d, g, w_bf, cs, sn)


def _retention_kernel(rq_ref, rk_ref, rv_ref, rg_ref, s0_ref, dm_ref, qd_ref, kd_ref,
                      y_ref, sfin_ref, s_scr, *, sdec, with_output):
    t = pl.program_id(1)

    @pl.when(t == 0)
    def _():
        s_scr[...] = s0_ref[0]

    for hh in range(RET_HEADS):
        qs = slice(hh * RET_DK, (hh + 1) * RET_DK)
        vs = slice(hh * RET_DV, (hh + 1) * RET_DV)
        q = rq_ref[:, qs]
        k = rk_ref[:, qs]
        v = rv_ref[:, vs]
        state = s_scr[hh]
        if with_output:
            s = lax.dot_general(q, k, NT_DIMS, preferred_element_type=F32) * dm_ref[hh]
            o = jnp.dot(s.astype(BF16), v, preferred_element_type=F32)
            qd = (q.astype(F32) * qd_ref[hh]).astype(BF16)
            o = o + jnp.dot(qd, state.astype(BF16), preferred_element_type=F32)
            y = o * lax.rsqrt(jnp.mean(o * o, axis=-1, keepdims=True) + RET_EPS)
            g = rg_ref[:, vs]
            y_ref[:, vs] = (y * (g * jax.nn.sigmoid(g))).astype(BF16)
        else:
            y_ref[:, vs] = jnp.zeros((y_ref.shape[0], RET_DV), BF16)
        kd = (k.astype(F32) * kd_ref[hh]).astype(BF16)
        kv = lax.dot_general(kd, v, TN_DIMS, preferred_element_type=F32)
        s_scr[hh] = sdec[hh] * state + kv

    @pl.when(t == pl.num_programs(1) - 1)
    def _():
        sfin_ref[0] = s_scr[...]


def _retention(rq, rk, rv, rg, s0, n_seq, t, with_output=True):
    rows = rq.shape[0]
    nt = rows // n_seq // t
    dmat, qdec, kdec, sdec = _retention_tables(t)
    row = lambda w: pl.BlockSpec((t, w), lambda b, i: (b * nt + i, 0))
    s_shape = (1, RET_HEADS, RET_DK, RET_DV)
    if s0.shape[0] == 1:
        s0_spec = pl.BlockSpec(s_shape, lambda b, i: (0, 0, 0, 0))
    else:
        s0_spec = pl.BlockSpec(s_shape, lambda b, i: (b, 0, 0, 0))
    y, sfin = pl.pallas_call(
        functools.partial(_retention_kernel, sdec=sdec, with_output=with_output),
        grid=(n_seq, nt),
        in_specs=[row(RET_QK_W), row(RET_QK_W), row(RET_V_W), row(RET_V_W), s0_spec,
                  _resident(dmat.shape), _resident(qdec.shape), _resident(kdec.shape)],
        out_specs=[row(RET_V_W), pl.BlockSpec(s_shape, lambda b, i: (b, 0, 0, 0))],
        out_shape=[jax.ShapeDtypeStruct((rows, RET_V_W), BF16),
                   jax.ShapeDtypeStruct((n_seq,) + s_shape[1:], F32)],
        scratch_shapes=[pltpu.VMEM(s_shape[1:], F32)],
        compiler_params=_cparams(2), name="retention",
    )(rq, rk, rv, rg, s0, dmat, qdec, kdec)
    return y, sfin


def _lambda(lam_ref):
    lp = lam_ref[...]
    a = jnp.exp(jnp.sum(lp[0:1] * lp[1:2], axis=-1, keepdims=True))
    b = jnp.exp(jnp.sum(lp[2:3] * lp[3:4], axis=-1, keepdims=True))
    return a - b + LAM_INIT


def _stack_maps(q):
    qf = q.astype(F32)
    lane = lax.broadcasted_iota(jnp.int32, q.shape, 1)
    return jnp.concatenate([jnp.where(lane < DIFF_DH, qf, 0.0),
                            jnp.where(lane < DIFF_DH, 0.0, qf)], axis=0).astype(BF16)


def _mix_and_norm(o_all, lam, subg, n):
    o = o_all[:n] - lam * o_all[n:]
    y = o * lax.rsqrt(jnp.mean(o * o, axis=-1, keepdims=True) + DIFF_EPS) * subg
    return (y * (1.0 - LAM_INIT)).astype(BF16)


def _diff_attn_kernel(qt_ref, k_ref, vt_ref, mk_ref, mvt_ref, tdiag_ref, tsub_ref, tmeta_ref,
                      lam_ref, subg_ref, y_ref, qq_scr, m_scr, l_scr, acc_scr):
    i = pl.program_id(2)
    heads = range(HEADS_PER_STEP)
    hs = lambda h: slice(h * DIFF_DV, (h + 1) * DIFF_DV)

    for h in heads:
        qt = qt_ref[0, hs(h), :].astype(F32)
        row = lax.broadcasted_iota(jnp.int32, qt.shape, 0)
        qq_scr[h] = jnp.concatenate([jnp.where(row < DIFF_DH, qt, 0.0),
                                     jnp.where(row < DIFF_DH, 0.0, qt)], axis=1).astype(BF16)

    def scores(h, k_blk, tile):
        s = jnp.dot(k_blk, qq_scr[h], preferred_element_type=F32)
        return s if tile is None else s + tile

    def update(h, k_blk, vt_blk, tile):
        s = scores(h, k_blk, tile)
        m_prev = m_scr[h]
        m_new = jnp.maximum(m_prev, jnp.max(s, axis=0, keepdims=True))
        alpha = jnp.exp(m_prev - m_new)
        p = jnp.exp(s - m_new)
        l_scr[h] = alpha * l_scr[h] + jnp.sum(p, axis=0, keepdims=True)
        acc_scr[h] = alpha * acc_scr[h] + jnp.dot(vt_blk, p.astype(BF16),
                                                  preferred_element_type=F32)
        m_scr[h] = m_new

    for h in heads:
        s = scores(h, mk_ref[:, hs(h)], tmeta_ref[h, jnp.minimum(i, 1)])
        m0 = jnp.max(s, axis=0, keepdims=True)
        p = jnp.exp(s - m0)
        m_scr[h] = m0
        l_scr[h] = jnp.sum(p, axis=0, keepdims=True)
        acc_scr[h] = jnp.dot(mvt_ref[hs(h), :], p.astype(BF16), preferred_element_type=F32)

    def key_block(j, tiles):
        off = pl.multiple_of(j * TK, TK)
        s = [scores(h, k_ref[pl.ds(off, TK), hs(h)], None if tiles is None else tiles[h])
             for h in heads]
        m_prev = [m_scr[h] for h in heads]
        m_new = [jnp.maximum(m_prev[h], jnp.max(s[h], axis=0, keepdims=True)) for h in heads]
        p = [jnp.exp(s[h] - m_new[h]) for h in heads]
        alpha = [jnp.exp(m_prev[h] - m_new[h]) for h in heads]
        for h in heads:
            m_scr[h] = m_new[h]
            l_scr[h] = alpha[h] * l_scr[h] + jnp.sum(p[h], axis=0, keepdims=True)
        pv = [jnp.dot(vt_ref[j, hs(h), :], p[h].astype(BF16), preferred_element_type=F32)
              for h in heads]
        for h in heads:
            acc_scr[h] = alpha[h] * acc_scr[h] + pv[h]

    def far_block(j, carry):
        key_block(j, None)
        return carry

    lax.fori_loop(0, jnp.maximum(i - 1, 0), far_block, 0)

    @pl.when(i >= 1)
    def _():
        key_block(i - 1, tsub_ref)

    key_block(i, tdiag_ref)

    lam = _lambda(lam_ref)
    for h in heads:
        o_all = acc_scr[h] / l_scr[h]
        o = o_all[:, :TQ] - lam * o_all[:, TQ:]
        y = o * lax.rsqrt(jnp.mean(o * o, axis=0, keepdims=True) + DIFF_EPS) * subg_ref[...]
        y_ref[:, hs(h)] = (y * (1.0 - LAM_INIT)).T.astype(BF16)


def _diff_attn(dqt, dk, dvt, mk, mvt, tdiag, tsub, tmeta, lam_p, subg_b, n_seq):
    rows = dk.shape[0]
    seq = rows // n_seq
    nq = seq // TQ
    hb = HEADS_PER_STEP
    wid = hb * DIFF_DV
    tile3 = lambda b, g, i: (g, 0, 0)
    return pl.pallas_call(
        _diff_attn_kernel, grid=(n_seq, DIFF_HEADS // hb, nq),
        in_specs=[pl.BlockSpec((1, wid, TQ), lambda b, g, i: (b * nq + i, g, 0)),
                  pl.BlockSpec((seq, wid), lambda b, g, i: (b, g)),
                  pl.BlockSpec((nq, wid, TK), lambda b, g, i: (b, g, 0)),
                  pl.BlockSpec((LANES, wid), lambda b, g, i: (0, g)),
                  pl.BlockSpec((wid, LANES), lambda b, g, i: (g, 0)),
                  pl.BlockSpec((hb, TK, 2 * TQ), tile3),
                  pl.BlockSpec((hb, TK, 2 * TQ), tile3),
                  pl.BlockSpec((hb, 2, LANES, 2 * TQ), lambda b, g, i: (g, 0, 0, 0)),
                  pl.BlockSpec((4, DIFF_DH), lambda b, g, i: (0, 0)),
                  pl.BlockSpec((DIFF_DV, TQ), lambda b, g, i: (0, 0))],
        out_specs=pl.BlockSpec((TQ, wid), lambda b, g, i: (b * nq + i, g)),
        out_shape=jax.ShapeDtypeStruct((rows, DIFF_W), BF16),
        scratch_shapes=[pltpu.VMEM((hb, DIFF_DV, 2 * TQ), BF16), pltpu.VMEM((hb, 1, 2 * TQ), F32),
                        pltpu.VMEM((hb, 1, 2 * TQ), F32), pltpu.VMEM((hb, DIFF_DV, 2 * TQ), F32)],
        compiler_params=_cparams(3), name="diff_attn",
    )(dqt, dk, dvt, mk, mvt, tdiag, tsub, tmeta, lam_p, subg_b)


def _sample_attn_kernel(q_ref, ck_ref, cv_ref, nk_ref, nv_ref, tile_ref, lam_ref, subg_ref,
                        y_ref, k_scr, v_scr, *, past, n_new):
    pad = k_scr.shape[0] - past - n_new
    k_scr[0:past] = ck_ref[...].astype(BF16)
    v_scr[0:past] = cv_ref[...].astype(BF16)
    k_scr[past:past + n_new] = nk_ref[...]
    v_scr[past:past + n_new] = nv_ref[...]
    k_scr[past + n_new:] = jnp.zeros((pad, LANES), BF16)
    v_scr[past + n_new:] = jnp.zeros((pad, LANES), BF16)
    qq = _stack_maps(q_ref[...])
    s = lax.dot_general(qq, k_scr[...], NT_DIMS, preferred_element_type=F32)
    s = (s.reshape(2, n_new, s.shape[-1]) + tile_ref[0][None]).reshape(2 * n_new, s.shape[-1])
    m = jnp.max(s, axis=1, keepdims=True)
    p = jnp.exp(s - m)
    l = jnp.sum(p, axis=1, keepdims=True)
    o_all = jnp.dot(p.astype(BF16), v_scr[...], preferred_element_type=F32) / l
    y_ref[...] = _mix_and_norm(o_all, _lambda(lam_ref), subg_ref[...], n_new)


def _sample_attn(dq, ck, cv, nk, nv, tile, lam_p, subg, n_seq, past, n_new):
    kpad = tile.shape[-1]
    return pl.pallas_call(
        functools.partial(_sample_attn_kernel, past=past, n_new=n_new),
        grid=(n_seq, DIFF_HEADS),
        in_specs=[pl.BlockSpec((n_new, LANES), lambda b, h: (b, h)),
                  pl.BlockSpec((past, LANES), lambda b, h: (b, h)),
                  pl.BlockSpec((past, LANES), lambda b, h: (b, h)),
                  pl.BlockSpec((n_new, LANES), lambda b, h: (b, h)),
                  pl.BlockSpec((n_new, LANES), lambda b, h: (b, h)),
                  pl.BlockSpec((1, n_new, kpad), lambda b, h: (h, 0, 0)),
                  pl.BlockSpec((4, DIFF_DH), lambda b, h: (0, 0)),
                  pl.BlockSpec((1, DIFF_DV), lambda b, h: (0, 0))],
        out_specs=pl.BlockSpec((n_new, LANES), lambda b, h: (b, h)),
        out_shape=jax.ShapeDtypeStruct((n_seq * n_new, DIFF_W), BF16),
        scratch_shapes=[pltpu.VMEM((kpad, LANES), BF16), pltpu.VMEM((kpad, LANES), BF16)],
        compiler_params=_cparams(2), name="sample_attn",
    )(dq, ck, cv, nk, nv, tile, lam_p, subg)


def _tail_kernel(h_ref, yr_ref, yd_ref, gr_ref, gd_ref, wrb_ref, wdb_ref, wo_ref, n2_ref,
                 wup_ref, wdn_ref, nf_ref, out_ref):
    a = jnp.dot(yr_ref[...], wrb_ref[...], preferred_element_type=F32)
    b = jnp.dot(yd_ref[...], wdb_ref[...], preferred_element_type=F32)
    merged = jax.nn.sigmoid(gr_ref[...]) * a + jax.nn.sigmoid(gd_ref[...]) * b
    h = h_ref[...] + jnp.dot(merged.astype(BF16), wo_ref[...], preferred_element_type=F32)
    xn = h * lax.rsqrt(jnp.mean(h * h, axis=-1, keepdims=True) + NORM_EPS) * n2_ref[...]
    xn = xn.astype(BF16)
    acc = jnp.zeros_like(h)
    for lo in range(0, D_FF, FF_CHUNK):
        n = min(FF_CHUNK, D_FF - lo)
        gate = jnp.dot(xn, wup_ref[:, lo:lo + n], preferred_element_type=F32)
        up = jnp.dot(xn, wup_ref[:, D_FF + lo:D_FF + lo + n], preferred_element_type=F32)
        act = (gate * jax.nn.sigmoid(gate) * up).astype(BF16)
        acc = acc + jnp.dot(act, wdn_ref[lo:lo + n, :], preferred_element_type=F32)
    h = h + acc
    out_ref[...] = h * lax.rsqrt(jnp.mean(h * h, axis=-1, keepdims=True) + NORM_EPS) * nf_ref[...]


def _tail(h2d, y_ret, y_diff, gr, gd, wrb, wdb, wo, n2, wup, wdn, nf, tm):
    rows = h2d.shape[0]
    row = pl.BlockSpec((tm, D_MODEL), lambda i: (i, 0))
    return pl.pallas_call(
        _tail_kernel, grid=(rows // tm,),
        in_specs=[row, row, row, row, row,
                  _resident(wrb.shape), _resident(wdb.shape), _resident(wo.shape),
                  _resident(n2.shape), _resident(wup.shape), _resident(wdn.shape),
                  _resident(nf.shape)],
        out_specs=row,
        out_shape=jax.ShapeDtypeStruct((rows, D_MODEL), F32),
        compiler_params=_cparams(1), name="tail",
    )(h2d, y_ret, y_diff, gr, gd, wrb, wdb, wo, n2, wup, wdn, nf)


def kernel(x_prompt, x_sample, cache_k, cache_v, state_ret, meta_tokens, rel_bias, norm1_g, w_in,
           lambda_q1, lambda_k1, lambda_q2, lambda_k2, diff_subln_g, w_ret_branch, w_diff_branch,
           w_o, norm2_g, w_ffn_up, w_ffn_down, normf_g):
    assert w_in.shape[0] == 1, "single-layer step only"
    bsz, seq, _ = x_prompt.shape
    dbsz, dseq, _ = x_sample.shape
    past = cache_k.shape[2]
    assert TQ == TK == TM_PROJ and TQ % CHUNK == 0 and DIFF_HEADS % HEADS_PER_STEP == 0
    assert seq % TQ == 0 and seq % T_RET == 0
    assert dseq <= CHUNK and past % CHUNK == 0 and meta_tokens.shape[0] == N_META

    w_in_bf = w_in[0].astype(BF16)
    g1 = norm1_g[0][None, :]
    lam_p = jnp.stack([lambda_q1[0], lambda_k1[0], lambda_q2[0], lambda_k2[0]])
    subg = diff_subln_g[0][None, :]
    wts = (w_ret_branch[0].astype(BF16), w_diff_branch[0].astype(BF16), w_o[0].astype(BF16),
           norm2_g[0][None, :], w_ffn_up[0].astype(BF16), w_ffn_down[0].astype(BF16),
           normf_g[None, :])

    kpad = ((past + dseq + LANES - 1) // LANES) * LANES
    both_maps = lambda ix: np.concatenate([ix.T, ix.T], axis=1)
    idx_diag = both_maps(_bias_idx(np.arange(TQ), np.arange(TK), TK))
    idx_sub = both_maps(_bias_idx(np.arange(TQ) + TK, np.arange(TK), TK))
    idx_meta = np.concatenate(
        [both_maps(_bias_idx(np.arange(TQ) + b * TQ, np.arange(LANES) - N_META, N_META))
         for b in range(2)], axis=0)
    idx_samp = _bias_idx(past + np.arange(dseq), np.arange(kpad), past + dseq)
    t_diag, t_sub, t_meta, t_samp = _bias_tiles(rel_bias, [idx_diag, idx_sub, idx_meta, idx_samp],
                                                [True, True, True, False])
    t_meta = t_meta.reshape(DIFF_HEADS, 2, LANES, 2 * TQ)

    cs_m, sn_m = _rotary_tables(np.arange(-N_META, 0))
    m_out = _in_proj(meta_tokens, g1, w_in_bf, cs_m, sn_m, N_META)
    zero_state = jnp.zeros((1, RET_HEADS, RET_DK, RET_DV), F32)
    _, s_meta = _retention(m_out[0], m_out[1], m_out[2], m_out[3], zero_state, 1, N_META,
                           with_output=False)
    pad_rows = ((0, LANES - N_META), (0, 0))
    mk = jnp.pad(m_out[5], pad_rows)
    mvt = jnp.pad(m_out[7], pad_rows).T
    subg_b = jnp.broadcast_to(diff_subln_g[0][:, None], (DIFF_DV, TQ))

    cs_p, sn_p = _rotary_tables(np.arange(seq))
    x2d = x_prompt.reshape(bsz * seq, D_MODEL)
    (rq, rk, rv, rg, dqt, dkb, dkf, dvt, dvf, gr, gd) = _in_proj(x2d, g1, w_in_bf, cs_p, sn_p,
                                                                 TM_PROJ, transposed=True)
    y_ret, s_fin = _retention(rq, rk, rv, rg, s_meta, bsz, T_RET)
    y_diff = _diff_attn(dqt, dkb, dvt, mk, mvt, t_diag, t_sub, t_meta, lam_p, subg_b, bsz)
    y_prompt = _tail(x2d, y_ret, y_diff, gr, gd, *wts, TM_TAIL).reshape(bsz, seq, D_MODEL)
    k_rows = jnp.concatenate([jnp.broadcast_to(m_out[6][None], (bsz, N_META, DIFF_W)),
                              dkf.reshape(bsz, seq, DIFF_W)], axis=1)
    v_rows = jnp.concatenate([jnp.broadcast_to(m_out[8][None], (bsz, N_META, DIFF_W)),
                              dvf.reshape(bsz, seq, DIFF_W)], axis=1)

    cs_s, sn_s = _rotary_tables(np.tile(past + np.arange(dseq), dbsz))
    xs2d = x_sample.reshape(dbsz * dseq, D_MODEL)
    (rq, rk, rv, rg, dq, dkb, dkf_s, dvb, dvf_s, gr, gd) = _in_proj(xs2d, g1, w_in_bf, cs_s, sn_s,
                                                                     dbsz * dseq)
    y_ret, s_samp = _retention(rq, rk, rv, rg, state_ret[0], dbsz, dseq)
    ck = cache_k[0].reshape(dbsz * past, DIFF_W)
    cv = cache_v[0].reshape(dbsz * past, DIFF_W)
    y_diff = _sample_attn(dq, ck, cv, dkb, dvb, t_samp, lam_p, subg, dbsz, past, dseq)
    y_sample = _tail(xs2d, y_ret, y_diff, gr, gd, *wts, dbsz * dseq).reshape(dbsz, dseq, D_MODEL)

    heads = (DIFF_HEADS, DIFF_DV)
    return (y_prompt, y_sample,
            k_rows.reshape((1, bsz, N_META + seq) + heads),
            v_rows.reshape((1, bsz, N_META + seq) + heads),
            s_fin[None],
            dkf_s.reshape((1, dbsz, dseq) + heads),
            dvf_s.reshape((1, dbsz, dseq) + heads),
            s_samp[None])
```

```python
import functools
import math

import numpy as np
import jax
import jax.numpy as jnp
from jax import lax
from jax.experimental import pallas as pl
from jax.experimental.pallas import tpu as pltpu

F32 = jnp.float32
BF16 = jnp.bfloat16

D_MODEL = 1024
CHUNK = 64
N_META = 16
RET_HEADS = 4
RET_DK = D_MODEL // 8
RET_DV = 2 * RET_DK
RET_QK_W = RET_HEADS * RET_DK
RET_V_W = RET_HEADS * RET_DV
ROPE_BASE = 10000.0
RET_EPS = 1e-6
DIFF_HEADS = 8
DIFF_DH = D_MODEL // 16
DIFF_DV = 2 * DIFF_DH
DIFF_W = DIFF_HEADS * DIFF_DV
DIFF_EPS = 1e-5
N_BUCKETS = 32
MAX_DISTANCE = 128
D_FF = ((8 * D_MODEL + 3 * 256 - 1) // (3 * 256)) * 256
NORM_EPS = 1e-6
NEG_INF = -1e30
LAM_INIT = 0.8 - 0.6 * math.exp(-0.3 * 0)
LOG2E = math.log2(math.e)
Q_SCALE = DIFF_DH ** -0.5 * LOG2E
ONES_ROWS = 16
DV_AUG = DIFF_DV + ONES_ROWS

OFF_RQ = 0
OFF_RK = OFF_RQ + RET_QK_W
OFF_RV = OFF_RK + RET_QK_W
OFF_RG = OFF_RV + RET_V_W
OFF_DQ = OFF_RG + RET_V_W
OFF_DK = OFF_DQ + DIFF_W
OFF_DV = OFF_DK + DIFF_W
OFF_GR = OFF_DV + DIFF_W
OFF_GD = OFF_GR + D_MODEL
W_IN = OFF_GD + D_MODEL

LANES = 128
VMEM_LIMIT_BYTES = 56 * 1024 * 1024
TM_PROJ = 256
TM_TAIL = 256
T_RET = 128
TQ = 256
TK = 256
HEADS_PER_STEP = 4
FF_CHUNK = 512

NT_DIMS = (((1,), (1,)), ((), ()))
TN_DIMS = (((0,), (0,)), ((), ()))


def _cparams(n_axes):
    return pltpu.CompilerParams(dimension_semantics=("arbitrary",) * n_axes,
                                vmem_limit_bytes=VMEM_LIMIT_BYTES)


def _resident(shape):
    nd = len(shape)
    return pl.BlockSpec(shape, lambda *_: (0,) * nd, pipeline_mode=pl.Buffered(1))


def _t5_bucket_np(rel):
    nb = N_BUCKETS // 2
    max_exact = nb // 2
    ret = np.where(rel > 0, nb, 0)
    n = np.abs(rel)
    nf = np.maximum(n, max_exact).astype(np.float64)
    large = max_exact + (np.log(nf / max_exact) / math.log(MAX_DISTANCE / max_exact)
                         * (nb - max_exact)).astype(np.int32)
    large = np.minimum(large, nb - 1)
    return (ret + np.where(n < max_exact, n, large)).astype(np.int32)


def _bias_idx(qpos, kpos, n_real):
    qpos = np.asarray(qpos)[:, None]
    kpos = np.asarray(kpos)[None, :]
    vis = (np.floor_divide(kpos, CHUNK) <= np.floor_divide(qpos, CHUNK))
    vis = vis & (np.arange(kpos.shape[1])[None, :] < n_real)
    return np.where(vis, _t5_bucket_np(kpos - qpos), -1).astype(np.int32)


def _rotary_tables(pos):
    half = RET_DK // 2
    freq = 1.0 / (ROPE_BASE ** jnp.linspace(0.0, 1.0, half, dtype=F32))
    ang = jnp.asarray(pos, F32)[:, None] * freq[None, :]
    cos, sin = jnp.cos(ang), jnp.sin(ang)
    return jnp.concatenate([cos, cos], axis=1), jnp.concatenate([-sin, sin], axis=1)


def _retention_tables(t):
    gam = 1.0 - 2.0 ** (-5.0 - np.arange(RET_HEADS, dtype=np.float64))
    lg = np.log(gam)[:, None, None]
    n = np.arange(t)[:, None]
    m = np.arange(t)[None, :]
    cn, cm = n // CHUNK, m // CHUNK
    expo = np.where(cm == cn, np.abs(n - m), n - m).astype(np.float64)[None]
    dmat = np.where((cm <= cn)[None], np.exp(lg * expo), 0.0)
    qdec = np.exp(lg[:, :, 0] * (np.arange(t)[None, :] + 1.0))[:, :, None]
    kdec = np.exp(lg[:, :, 0] * (t - 1.0 - np.arange(t)[None, :]))[:, :, None]
    sdec = np.exp(lg[:, 0, 0] * t)
    return (jnp.asarray(dmat, F32), jnp.asarray(qdec, F32), jnp.asarray(kdec, F32),
            tuple(float(s) for s in sdec))


def _bias_kernel(rb_ref, *refs, n_tiles, shifted):
    idx_refs, out_refs = refs[:n_tiles], refs[n_tiles:]
    h = pl.program_id(0)
    far = rb_ref[N_BUCKETS // 2 - 1, h]
    for idx_ref, out_ref, sh in zip(idx_refs, out_refs, shifted):
        idx = idx_ref[...]
        acc = jnp.full(idx.shape, NEG_INF, F32)
        for b in range(N_BUCKETS):
            val = (rb_ref[b, h] - far if sh else rb_ref[b, h]) * LOG2E
            acc = jnp.where(idx == b, val, acc)
        out_ref[0] = acc


def _bias_tiles(rel_bias, idx_list, shifted):
    n = len(idx_list)
    in_specs = [pl.BlockSpec(memory_space=pltpu.SMEM)]
    in_specs += [pl.BlockSpec(ix.shape, lambda h: (0, 0)) for ix in idx_list]
    out_specs = [pl.BlockSpec((1,) + ix.shape, lambda h: (h, 0, 0)) for ix in idx_list]
    out_shape = [jax.ShapeDtypeStruct((DIFF_HEADS,) + ix.shape, F32) for ix in idx_list]
    return pl.pallas_call(
        functools.partial(_bias_kernel, n_tiles=n, shifted=tuple(shifted)),
        grid=(DIFF_HEADS,), in_specs=in_specs, out_specs=out_specs, out_shape=out_shape,
        compiler_params=_cparams(1), name="bias_tiles",
    )(rel_bias, *[jnp.asarray(ix) for ix in idx_list])


def _inproj_kernel(x_ref, g_ref, w_ref, cs_ref, sn_ref,
                   rq_ref, rk_ref, rv_ref, rg_ref, dq_ref, dkb_ref, dkf_ref,
                   dvb_ref, dvf_ref, gr_ref, gd_ref, *, transposed):
    x = x_ref[...]
    xn = x * lax.rsqrt(jnp.mean(x * x, axis=-1, keepdims=True) + NORM_EPS) * g_ref[...]
    xn = xn.astype(BF16)

    def proj(lo, n):
        return jnp.dot(xn, w_ref[:, lo:lo + n], preferred_element_type=F32)

    cs = cs_ref[...]
    sn = sn_ref[...]

    def rotary(u):
        return u * cs + pltpu.roll(u, RET_DK // 2, 1) * sn

    uq = proj(OFF_RQ, RET_QK_W)
    uk = proj(OFF_RK, RET_QK_W)
    for hh in range(RET_HEADS):
        sl = slice(hh * RET_DK, (hh + 1) * RET_DK)
        rq_ref[:, sl] = (rotary(uq[:, sl]) * (RET_DK ** -0.5)).astype(BF16)
        rk_ref[:, sl] = rotary(uk[:, sl]).astype(BF16)
    rv_ref[...] = proj(OFF_RV, RET_V_W).astype(BF16)
    rg_ref[...] = proj(OFF_RG, RET_V_W)
    dq = proj(OFF_DQ, DIFF_W) * Q_SCALE
    dk = proj(OFF_DK, DIFF_W)
    dkf_ref[...] = dk
    dkb_ref[...] = dk.astype(BF16)
    dv = proj(OFF_DV, DIFF_W)
    dvf_ref[...] = dv
    if transposed:
        dq_ref[0] = dq.T.astype(BF16)
        dvt = dv.T.astype(BF16)
        ones = jnp.ones((ONES_ROWS, dvt.shape[1]), BF16)
        for hh in range(DIFF_HEADS):
            dvb_ref[0, hh * DV_AUG:hh * DV_AUG + DIFF_DV, :] = dvt[hh * DIFF_DV:(hh + 1) * DIFF_DV]
            dvb_ref[0, hh * DV_AUG + DIFF_DV:(hh + 1) * DV_AUG, :] = ones
    else:
        dq_ref[...] = dq.astype(BF16)
        dvb_ref[...] = dv.astype(BF16)
    gr_ref[...] = proj(OFF_GR, D_MODEL)
    gd_ref[...] = proj(OFF_GD, D_MODEL)


def _in_proj(x2d, g, w_bf, cs, sn, tm, transposed=False):
    rows = x2d.shape[0]
    n_pos = cs.shape[0] // tm
    row = lambda w: pl.BlockSpec((tm, w), lambda i: (i, 0))
    pos = pl.BlockSpec((tm, LANES), lambda i: (i % n_pos, 0))
    outs = [(RET_QK_W, BF16), (RET_QK_W, BF16), (RET_V_W, BF16), (RET_V_W, F32),
            (DIFF_W, BF16), (DIFF_W, BF16), (DIFF_W, F32), (DIFF_W, BF16), (DIFF_W, F32),
            (D_MODEL, F32), (D_MODEL, F32)]
    out_specs = [row(w) for w, _ in outs]
    out_shape = [jax.ShapeDtypeStruct((rows, w), dt) for w, dt in outs]
    if transposed:
        for o, w in ((4, DIFF_W), (7, DIFF_HEADS * DV_AUG)):
            out_specs[o] = pl.BlockSpec((1, w, tm), lambda i: (i, 0, 0))
            out_shape[o] = jax.ShapeDtypeStruct((rows // tm, w, tm), BF16)
    return pl.pallas_call(
        functools.partial(_inproj_kernel, transposed=transposed), grid=(rows // tm,),
        in_specs=[row(D_MODEL), _resident((1, D_MODEL)), _resident((D_MODEL, W_IN)), pos, pos],
        out_specs=out_specs, out_shape=out_shape,
        compiler_params=_cparams(1), name="in_proj",
    )(x2d, g, w_bf, cs, sn)


def _retention_kernel(rq_ref, rk_ref, rv_ref, rg_ref, s0_ref, dm_ref, qd_ref, kd_ref,
                      y_ref, sfin_ref, s_scr, *, sdec, with_output):
    t = pl.program_id(1)

    @pl.when(t == 0)
    def _():
        s_scr[...] = s0_ref[0]

    for hh in range(RET_HEADS):
        qs = slice(hh * RET_DK, (hh + 1) * RET_DK)
        vs = slice(hh * RET_DV, (hh + 1) * RET_DV)
        q = rq_ref[:, qs]
        k = rk_ref[:, qs]
        v = rv_ref[:, vs]
        state = s_scr[hh]
        if with_output:
            s = lax.dot_general(q, k, NT_DIMS, preferred_element_type=F32) * dm_ref[hh]
            o = jnp.dot(s.astype(BF16), v, preferred_element_type=F32)
            qd = (q.astype(F32) * qd_ref[hh]).astype(BF16)
            o = o + jnp.dot(qd, state.astype(BF16), preferred_element_type=F32)
            y = o * lax.rsqrt(jnp.mean(o * o, axis=-1, keepdims=True) + RET_EPS)
            g = rg_ref[:, vs]
            y_ref[:, vs] = (y * (g * jax.nn.sigmoid(g))).astype(BF16)
        else:
            y_ref[:, vs] = jnp.zeros((y_ref.shape[0], RET_DV), BF16)
        kd = (k.astype(F32) * kd_ref[hh]).astype(BF16)
        kv = lax.dot_general(kd, v, TN_DIMS, preferred_element_type=F32)
        s_scr[hh] = sdec[hh] * state + kv

    @pl.when(t == pl.num_programs(1) - 1)
    def _():
        sfin_ref[0] = s_scr[...]


def _retention(rq, rk, rv, rg, s0, n_seq, t, with_output=True):
    rows = rq.shape[0]
    nt = rows // n_seq // t
    dmat, qdec, kdec, sdec = _retention_tables(t)
    row = lambda w: pl.BlockSpec((t, w), lambda b, i: (b * nt + i, 0))
    s_shape = (1, RET_HEADS, RET_DK, RET_DV)
    if s0.shape[0] == 1:
        s0_spec = pl.BlockSpec(s_shape, lambda b, i: (0, 0, 0, 0))
    else:
        s0_spec = pl.BlockSpec(s_shape, lambda b, i: (b, 0, 0, 0))
    y, sfin = pl.pallas_call(
        functools.partial(_retention_kernel, sdec=sdec, with_output=with_output),
        grid=(n_seq, nt),
        in_specs=[row(RET_QK_W), row(RET_QK_W), row(RET_V_W), row(RET_V_W), s0_spec,
                  _resident(dmat.shape), _resident(qdec.shape), _resident(kdec.shape)],
        out_specs=[row(RET_V_W), pl.BlockSpec(s_shape, lambda b, i: (b, 0, 0, 0))],
        out_shape=[jax.ShapeDtypeStruct((rows, RET_V_W), BF16),
                   jax.ShapeDtypeStruct((n_seq,) + s_shape[1:], F32)],
        scratch_shapes=[pltpu.VMEM(s_shape[1:], F32)],
        compiler_params=_cparams(2), name="retention",
    )(rq, rk, rv, rg, s0, dmat, qdec, kdec)
    return y, sfin


def _lambda(lam_ref):
    lp = lam_ref[...]
    a = jnp.exp(jnp.sum(lp[0:1] * lp[1:2], axis=-1, keepdims=True))
    b = jnp.exp(jnp.sum(lp[2:3] * lp[3:4], axis=-1, keepdims=True))
    return a - b + LAM_INIT


def _stack_maps(q):
    qf = q.astype(F32)
    lane = lax.broadcasted_iota(jnp.int32, q.shape, 1)
    return jnp.concatenate([jnp.where(lane < DIFF_DH, qf, 0.0),
                            jnp.where(lane < DIFF_DH, 0.0, qf)], axis=0).astype(BF16)


def _mix_and_norm(o_all, lam, subg, n):
    o = o_all[:n] - lam * o_all[n:]
    y = o * lax.rsqrt(jnp.mean(o * o, axis=-1, keepdims=True) + DIFF_EPS) * subg
    return (y * (1.0 - LAM_INIT)).astype(BF16)


def _diff_attn_kernel(qt_ref, k_ref, vt_ref, mk_ref, mvt_ref, tdiag_ref, tsub_ref, tmeta_ref,
                      lam_ref, subg_ref, y_ref, qq_scr, m_scr, acc_scr, s_scr):
    i = pl.program_id(2)
    heads = range(HEADS_PER_STEP)
    hs = lambda h: slice(h * DIFF_DV, (h + 1) * DIFF_DV)
    vs = lambda h: slice(h * DV_AUG, (h + 1) * DV_AUG)

    for h in heads:
        qt = qt_ref[0, hs(h), :].astype(F32)
        row = lax.broadcasted_iota(jnp.int32, qt.shape, 0)
        qq_scr[h] = jnp.concatenate([jnp.where(row < DIFF_DH, qt, 0.0),
                                     jnp.where(row < DIFF_DH, 0.0, qt)], axis=1).astype(BF16)

    def scores(h, k_blk, tile):
        s = jnp.dot(k_blk, qq_scr[h], preferred_element_type=F32)
        return s if tile is None else s + tile

    for h in heads:
        s = scores(h, mk_ref[:, hs(h)], tmeta_ref[h, jnp.minimum(i, 1)])
        m0 = jnp.max(s, axis=0, keepdims=True)
        m_scr[h] = m0
        acc_scr[h] = jnp.dot(mvt_ref[vs(h), :], jnp.exp2(s - m0).astype(BF16),
                             preferred_element_type=F32)

    def qk(j):
        off = pl.multiple_of(j * TK, TK)
        return tuple(scores(h, k_ref[pl.ds(off, TK), hs(h)], None) for h in heads)

    def consume(j, s):
        m_prev = [m_scr[h] for h in heads]
        m_new = [jnp.maximum(m_prev[h], jnp.max(s[h], axis=0, keepdims=True)) for h in heads]
        p = [jnp.exp2(s[h] - m_new[h]).astype(BF16) for h in heads]
        alpha = [jnp.exp2(m_prev[h] - m_new[h]) for h in heads]
        for h in heads:
            m_scr[h] = m_new[h]
        pv = [jnp.dot(vt_ref[j, vs(h), :], p[h], preferred_element_type=F32) for h in heads]
        for h in heads:
            acc_scr[h] = alpha[h] * acc_scr[h] + pv[h]

    def qk_to_scratch(j):
        for h, s in enumerate(qk(j)):
            s_scr[h] = s

    def far_block(j, carry):
        consume(j, [s_scr[h] for h in heads])
        qk_to_scratch(j + 1)
        return carry

    j_sub = jnp.maximum(i - 1, 0)
    qk_to_scratch(0)
    lax.fori_loop(0, j_sub, far_block, 0)
    pen = jnp.where(i == 0, NEG_INF, 0.0)
    consume(j_sub, [s_scr[h] + (tsub_ref[h] + pen) for h in heads])
    s_diag = qk(i)
    consume(i, [s_diag[h] + tdiag_ref[h] for h in heads])

    lam = _lambda(lam_ref)
    for h in heads:
        o_all = acc_scr[h, :DIFF_DV, :] / acc_scr[h, DIFF_DV:DIFF_DV + 1, :]
        o = o_all[:, :TQ] - lam * o_all[:, TQ:]
        y = o * lax.rsqrt(jnp.mean(o * o, axis=0, keepdims=True) + DIFF_EPS) * subg_ref[...]
        y_ref[:, hs(h)] = (y * (1.0 - LAM_INIT)).T.astype(BF16)


def _diff_attn(dqt, dk, dvt, mk, mvt, tdiag, tsub, tmeta, lam_p, subg_b, n_seq):
    rows = dk.shape[0]
    seq = rows // n_seq
    nq = seq // TQ
    hb = HEADS_PER_STEP
    wid = hb * DIFF_DV
    wid_v = hb * DV_AUG
    tile3 = lambda b, g, i: (g, 0, 0)
    slow = lambda shape, imap: pl.BlockSpec(shape, imap, pipeline_mode=pl.Buffered(1))
    return pl.pallas_call(
        _diff_attn_kernel, grid=(n_seq, DIFF_HEADS // hb, nq),
        in_specs=[pl.BlockSpec((1, wid, TQ), lambda b, g, i: (b * nq + i, g, 0)),
                  slow((seq, wid), lambda b, g, i: (b, g)),
                  slow((nq, wid_v, TK), lambda b, g, i: (b, g, 0)),
                  slow((N_META, wid), lambda b, g, i: (0, g)),
                  slow((wid_v, N_META), lambda b, g, i: (g, 0)),
                  slow((hb, TK, 2 * TQ), tile3),
                  slow((hb, TK, 2 * TQ), tile3),
                  slow((hb, 2, N_META, 2 * TQ), lambda b, g, i: (g, 0, 0, 0)),
                  slow((4, DIFF_DH), lambda b, g, i: (0, 0)),
                  slow((DIFF_DV, TQ), lambda b, g, i: (0, 0))],
        out_specs=pl.BlockSpec((TQ, wid), lambda b, g, i: (b * nq + i, g)),
        out_shape=jax.ShapeDtypeStruct((rows, DIFF_W), BF16),
        scratch_shapes=[pltpu.VMEM((hb, DIFF_DV, 2 * TQ), BF16), pltpu.VMEM((hb, 1, 2 * TQ), F32),
                        pltpu.VMEM((hb, DV_AUG, 2 * TQ), F32), pltpu.VMEM((hb, TK, 2 * TQ), F32)],
        compiler_params=_cparams(3), name="diff_attn",
    )(dqt, dk, dvt, mk, mvt, tdiag, tsub, tmeta, lam_p, subg_b)


def _sample_attn_kernel(q_ref, ck_ref, cv_ref, nk_ref, nv_ref, tile_ref, lam_ref, subg_ref,
                        y_ref, k_scr, v_scr, *, past, n_new):
    pad = k_scr.shape[0] - past - n_new
    k_scr[0:past] = ck_ref[...].astype(BF16)
    v_scr[0:past] = cv_ref[...].astype(BF16)
    k_scr[past:past + n_new] = nk_ref[...]
    v_scr[past:past + n_new] = nv_ref[...]
    k_scr[past + n_new:] = jnp.zeros((pad, LANES), BF16)
    v_scr[past + n_new:] = jnp.zeros((pad, LANES), BF16)
    qq = _stack_maps(q_ref[...])
    s = lax.dot_general(qq, k_scr[...], NT_DIMS, preferred_element_type=F32)
    s = (s.reshape(2, n_new, s.shape[-1]) + tile_ref[0][None]).reshape(2 * n_new, s.shape[-1])
    m = jnp.max(s, axis=1, keepdims=True)
    p = jnp.exp2(s - m)
    l = jnp.sum(p, axis=1, keepdims=True)
    o_all = jnp.dot(p.astype(BF16), v_scr[...], preferred_element_type=F32) / l
    y_ref[...] = _mix_and_norm(o_all, _lambda(lam_ref), subg_ref[...], n_new)


def _sample_attn(dq, ck, cv, nk, nv, tile, lam_p, subg, n_seq, past, n_new):
    kpad = tile.shape[-1]
    return pl.pallas_call(
        functools.partial(_sample_attn_kernel, past=past, n_new=n_new),
        grid=(n_seq, DIFF_HEADS),
        in_specs=[pl.BlockSpec((n_new, LANES), lambda b, h: (b, h)),
                  pl.BlockSpec((past, LANES), lambda b, h: (b, h)),
                  pl.BlockSpec((past, LANES), lambda b, h: (b, h)),
                  pl.BlockSpec((n_new, LANES), lambda b, h: (b, h)),
                  pl.BlockSpec((n_new, LANES), lambda b, h: (b, h)),
                  pl.BlockSpec((1, n_new, kpad), lambda b, h: (h, 0, 0)),
                  pl.BlockSpec((4, DIFF_DH), lambda b, h: (0, 0)),
                  pl.BlockSpec((1, DIFF_DV), lambda b, h: (0, 0))],
        out_specs=pl.BlockSpec((n_new, LANES), lambda b, h: (b, h)),
        out_shape=jax.ShapeDtypeStruct((n_seq * n_new, DIFF_W), BF16),
        scratch_shapes=[pltpu.VMEM((kpad, LANES), BF16), pltpu.VMEM((kpad, LANES), BF16)],
        compiler_params=_cparams(2), name="sample_attn",
    )(dq, ck, cv, nk, nv, tile, lam_p, subg)


def _tail_kernel(h_ref, yr_ref, yd_ref, gr_ref, gd_ref, wrb_ref, wdb_ref, wo_ref, n2_ref,
                 wup_ref, wdn_ref, nf_ref, out_ref):
    a = jnp.dot(yr_ref[...], wrb_ref[...], preferred_element_type=F32)
    b = jnp.dot(yd_ref[...], wdb_ref[...], preferred_element_type=F32)
    merged = jax.nn.sigmoid(gr_ref[...]) * a + jax.nn.sigmoid(gd_ref[...]) * b
    h = h_ref[...] + jnp.dot(merged.astype(BF16), wo_ref[...], preferred_element_type=F32)
    xn = h * lax.rsqrt(jnp.mean(h * h, axis=-1, keepdims=True) + NORM_EPS) * n2_ref[...]
    xn = xn.astype(BF16)
    acc = jnp.zeros_like(h)
    for lo in range(0, D_FF, FF_CHUNK):
        n = min(FF_CHUNK, D_FF - lo)
        gate = jnp.dot(xn, wup_ref[:, lo:lo + n], preferred_element_type=F32)
        up = jnp.dot(xn, wup_ref[:, D_FF + lo:D_FF + lo + n], preferred_element_type=F32)
        act = (gate * jax.nn.sigmoid(gate) * up).astype(BF16)
        acc = acc + jnp.dot(act, wdn_ref[lo:lo + n, :], preferred_element_type=F32)
    h = h + acc
    out_ref[...] = h * lax.rsqrt(jnp.mean(h * h, axis=-1, keepdims=True) + NORM_EPS) * nf_ref[...]


def _tail(h2d, y_ret, y_diff, gr, gd, wrb, wdb, wo, n2, wup, wdn, nf, tm):
    rows = h2d.shape[0]
    row = pl.BlockSpec((tm, D_MODEL), lambda i: (i, 0))
    return pl.pallas_call(
        _tail_kernel, grid=(rows // tm,),
        in_specs=[row, row, row, row, row,
                  _resident(wrb.shape), _resident(wdb.shape), _resident(wo.shape),
                  _resident(n2.shape), _resident(wup.shape), _resident(wdn.shape),
                  _resident(nf.shape)],
        out_specs=row,
        out_shape=jax.ShapeDtypeStruct((rows, D_MODEL), F32),
        compiler_params=_cparams(1), name="tail",
    )(h2d, y_ret, y_diff, gr, gd, wrb, wdb, wo, n2, wup, wdn, nf)


def kernel(x_prompt, x_sample, cache_k, cache_v, state_ret, meta_tokens, rel_bias, norm1_g, w_in,
           lambda_q1, lambda_k1, lambda_q2, lambda_k2, diff_subln_g, w_ret_branch, w_diff_branch,
           w_o, norm2_g, w_ffn_up, w_ffn_down, normf_g):
    assert w_in.shape[0] == 1, "single-layer step only"
    bsz, seq, _ = x_prompt.shape
    dbsz, dseq, _ = x_sample.shape
    past = cache_k.shape[2]
    assert TQ == TK == TM_PROJ and TQ % CHUNK == 0 and DIFF_HEADS % HEADS_PER_STEP == 0
    assert seq % TQ == 0 and seq % T_RET == 0
    assert dseq <= CHUNK and past % CHUNK == 0 and meta_tokens.shape[0] == N_META

    w_in_bf = w_in[0].astype(BF16)
    g1 = norm1_g[0][None, :]
    lam_p = jnp.stack([lambda_q1[0], lambda_k1[0], lambda_q2[0], lambda_k2[0]])
    subg = diff_subln_g[0][None, :]
    wts = (w_ret_branch[0].astype(BF16), w_diff_branch[0].astype(BF16), w_o[0].astype(BF16),
           norm2_g[0][None, :], w_ffn_up[0].astype(BF16), w_ffn_down[0].astype(BF16),
           normf_g[None, :])

    kpad = ((past + dseq + LANES - 1) // LANES) * LANES
    both_maps = lambda ix: np.concatenate([ix.T, ix.T], axis=1)
    idx_diag = both_maps(_bias_idx(np.arange(TQ), np.arange(TK), TK))
    idx_sub = both_maps(_bias_idx(np.arange(TQ) + TK, np.arange(TK), TK))
    idx_meta = np.concatenate(
        [both_maps(_bias_idx(np.arange(TQ) + b * TQ, np.arange(N_META) - N_META, N_META))
         for b in range(2)], axis=0)
    idx_samp = _bias_idx(past + np.arange(dseq), np.arange(kpad), past + dseq)
    t_diag, t_sub, t_meta, t_samp = _bias_tiles(rel_bias, [idx_diag, idx_sub, idx_meta, idx_samp],
                                                [True, True, True, False])
    t_meta = t_meta.reshape(DIFF_HEADS, 2, N_META, 2 * TQ)

    cs_m, sn_m = _rotary_tables(np.arange(-N_META, 0))
    m_out = _in_proj(meta_tokens, g1, w_in_bf, cs_m, sn_m, N_META)
    zero_state = jnp.zeros((1, RET_HEADS, RET_DK, RET_DV), F32)
    _, s_meta = _retention(m_out[0], m_out[1], m_out[2], m_out[3], zero_state, 1, N_META,
                           with_output=False)
    mk = m_out[5]
    mv_aug = jnp.concatenate([m_out[7].reshape(N_META, DIFF_HEADS, DIFF_DV),
                              jnp.ones((N_META, DIFF_HEADS, ONES_ROWS), BF16)], axis=2)
    mvt = mv_aug.reshape(N_META, DIFF_HEADS * DV_AUG).T
    subg_b = jnp.broadcast_to(diff_subln_g[0][:, None], (DIFF_DV, TQ))

    cs_p, sn_p = _rotary_tables(np.arange(seq))
    x2d = x_prompt.reshape(bsz * seq, D_MODEL)
    (rq, rk, rv, rg, dqt, dkb, dkf, dvt, dvf, gr, gd) = _in_proj(x2d, g1, w_in_bf, cs_p, sn_p,
                                                                 TM_PROJ, transposed=True)
    y_ret, s_fin = _retention(rq, rk, rv, rg, s_meta, bsz, T_RET)
    y_diff = _diff_attn(dqt, dkb, dvt, mk, mvt, t_diag, t_sub, t_meta, lam_p, subg_b, bsz)
    y_prompt = _tail(x2d, y_ret, y_diff, gr, gd, *wts, TM_TAIL).reshape(bsz, seq, D_MODEL)
    k_rows = jnp.concatenate([jnp.broadcast_to(m_out[6][None], (bsz, N_META, DIFF_W)),
                              dkf.reshape(bsz, seq, DIFF_W)], axis=1)
    v_rows = jnp.concatenate([jnp.broadcast_to(m_out[8][None], (bsz, N_META, DIFF_W)),
                              dvf.reshape(bsz, seq, DIFF_W)], axis=1)

    cs_s, sn_s = _rotary_tables(np.tile(past + np.arange(dseq), dbsz))
    xs2d = x_sample.reshape(dbsz * dseq, D_MODEL)
    (rq, rk, rv, rg, dq, dkb, dkf_s, dvb, dvf_s, gr, gd) = _in_proj(xs2d, g1, w_in_bf, cs_s, sn_s,
                                                                     dbsz * dseq)
    y_ret, s_samp = _retention(rq, rk, rv, rg, state_ret[0], dbsz, dseq)
    ck = cache_k[0].reshape(dbsz * past, DIFF_W)
    cv = cache_v[0].reshape(dbsz * past, DIFF_W)
    y_diff = _sample_attn(dq, ck, cv, dkb, dvb, t_samp, lam_p, subg, dbsz, past, dseq)
    y_sample = _tail(xs2d, y_ret, y_diff, gr, gd, *wts, dbsz * dseq).reshape(dbsz, dseq, D_MODEL)

    heads = (DIFF_HEADS, DIFF_DV)
    return (y_prompt, y_sample,
            k_rows.reshape((1, bsz, N_META + seq) + heads),
            v_rows.reshape((1, bsz, N_META + seq) + heads),
            s_fin[None],
            dkf_s.reshape((1, dbsz, dseq) + heads),
            dvf_s.reshape((1, dbsz, dseq) + heads),
            s_samp[None])
```

```python
import functools
import math

import numpy as np
import jax
import jax.numpy as jnp
from jax import lax
from jax.experimental import pallas as pl
from jax.experimental.pallas import tpu as pltpu

F32 = jnp.float32
BF16 = jnp.bfloat16

D_MODEL = 1024
CHUNK = 64
N_META = 16
RET_HEADS = 4
RET_DK = D_MODEL // 8
RET_DV = 2 * RET_DK
RET_QK_W = RET_HEADS * RET_DK
RET_V_W = RET_HEADS * RET_DV
ROPE_BASE = 10000.0
RET_EPS = 1e-6
DIFF_HEADS = 8
DIFF_DH = D_MODEL // 16
DIFF_DV = 2 * DIFF_DH
DIFF_W = DIFF_HEADS * DIFF_DV
DIFF_EPS = 1e-5
N_BUCKETS = 32
MAX_DISTANCE = 128
D_FF = ((8 * D_MODEL + 3 * 256 - 1) // (3 * 256)) * 256
NORM_EPS = 1e-6
NEG_INF = -1e30
LAM_INIT = 0.8 - 0.6 * math.exp(-0.3 * 0)
LOG2E = math.log2(math.e)
Q_SCALE = DIFF_DH ** -0.5 * LOG2E
ONES_ROWS = 16
DV_AUG = DIFF_DV + ONES_ROWS

OFF_RQ = 0
OFF_RK = OFF_RQ + RET_QK_W
OFF_RV = OFF_RK + RET_QK_W
OFF_RG = OFF_RV + RET_V_W
OFF_DQ = OFF_RG + RET_V_W
OFF_DK = OFF_DQ + DIFF_W
OFF_DV = OFF_DK + DIFF_W
OFF_GR = OFF_DV + DIFF_W
OFF_GD = OFF_GR + D_MODEL
W_IN = OFF_GD + D_MODEL

LANES = 128
VMEM_LIMIT_BYTES = 56 * 1024 * 1024
TM_PROJ = 256
TM_TAIL = 256
T_RET = 128
TQ = 256
TK = 256
HEADS_PER_STEP = 4
FF_CHUNK = 512

NT_DIMS = (((1,), (1,)), ((), ()))
TN_DIMS = (((0,), (0,)), ((), ()))


def _cparams(n_axes):
    return pltpu.CompilerParams(dimension_semantics=("arbitrary",) * n_axes,
                                vmem_limit_bytes=VMEM_LIMIT_BYTES)


def _resident(shape):
    nd = len(shape)
    return pl.BlockSpec(shape, lambda *_: (0,) * nd, pipeline_mode=pl.Buffered(1))


def _t5_bucket_np(rel):
    nb = N_BUCKETS // 2
    max_exact = nb // 2
    ret = np.where(rel > 0, nb, 0)
    n = np.abs(rel)
    nf = np.maximum(n, max_exact).astype(np.float64)
    large = max_exact + (np.log(nf / max_exact) / math.log(MAX_DISTANCE / max_exact)
                         * (nb - max_exact)).astype(np.int32)
    large = np.minimum(large, nb - 1)
    return (ret + np.where(n < max_exact, n, large)).astype(np.int32)


def _bias_idx(qpos, kpos, n_real):
    qpos = np.asarray(qpos)[:, None]
    kpos = np.asarray(kpos)[None, :]
    vis = (np.floor_divide(kpos, CHUNK) <= np.floor_divide(qpos, CHUNK))
    vis = vis & (np.arange(kpos.shape[1])[None, :] < n_real)
    return np.where(vis, _t5_bucket_np(kpos - qpos), -1).astype(np.int32)


def _rotary_tables(pos):
    half = RET_DK // 2
    freq = 1.0 / (ROPE_BASE ** jnp.linspace(0.0, 1.0, half, dtype=F32))
    ang = jnp.asarray(pos, F32)[:, None] * freq[None, :]
    cos, sin = jnp.cos(ang), jnp.sin(ang)
    return jnp.concatenate([cos, cos], axis=1), jnp.concatenate([-sin, sin], axis=1)


def _retention_tables(t):
    gam = 1.0 - 2.0 ** (-5.0 - np.arange(RET_HEADS, dtype=np.float64))
    lg = np.log(gam)[:, None, None]
    n = np.arange(t)[:, None]
    m = np.arange(t)[None, :]
    cn, cm = n // CHUNK, m // CHUNK
    expo = np.where(cm == cn, np.abs(n - m), n - m).astype(np.float64)[None]
    dmat = np.where((cm <= cn)[None], np.exp(lg * expo), 0.0)
    qdec = np.exp(lg[:, :, 0] * (np.arange(t)[None, :] + 1.0))[:, :, None]
    kdec = np.exp(lg[:, :, 0] * (t - 1.0 - np.arange(t)[None, :]))[:, :, None]
    sdec = np.exp(lg[:, 0, 0] * t)
    return (jnp.asarray(dmat, F32), jnp.asarray(qdec, F32), jnp.asarray(kdec, F32),
            tuple(float(s) for s in sdec))


def _bias_kernel(rb_ref, *refs, n_tiles, shifted):
    idx_refs, out_refs = refs[:n_tiles], refs[n_tiles:]
    h = pl.program_id(0)
    far = rb_ref[N_BUCKETS // 2 - 1, h]
    for idx_ref, out_ref, sh in zip(idx_refs, out_refs, shifted):
        idx = idx_ref[...]
        acc = jnp.full(idx.shape, NEG_INF, F32)
        for b in range(N_BUCKETS):
            val = (rb_ref[b, h] - far if sh else rb_ref[b, h]) * LOG2E
            acc = jnp.where(idx == b, val, acc)
        out_ref[0] = acc


def _bias_tiles(rel_bias, idx_list, shifted):
    n = len(idx_list)
    in_specs = [pl.BlockSpec(memory_space=pltpu.SMEM)]
    in_specs += [pl.BlockSpec(ix.shape, lambda h: (0, 0)) for ix in idx_list]
    out_specs = [pl.BlockSpec((1,) + ix.shape, lambda h: (h, 0, 0)) for ix in idx_list]
    out_shape = [jax.ShapeDtypeStruct((DIFF_HEADS,) + ix.shape, F32) for ix in idx_list]
    return pl.pallas_call(
        functools.partial(_bias_kernel, n_tiles=n, shifted=tuple(shifted)),
        grid=(DIFF_HEADS,), in_specs=in_specs, out_specs=out_specs, out_shape=out_shape,
        compiler_params=_cparams(1), name="bias_tiles",
    )(rel_bias, *[jnp.asarray(ix) for ix in idx_list])


def _inproj_kernel(x_ref, g_ref, w_ref, cs_ref, sn_ref,
                   rq_ref, rk_ref, rv_ref, rg_ref, dq_ref, dkb_ref, dkf_ref,
                   dvb_ref, dvf_ref, gr_ref, gd_ref, *, transposed):
    x = x_ref[...]
    xn = x * lax.rsqrt(jnp.mean(x * x, axis=-1, keepdims=True) + NORM_EPS) * g_ref[...]
    xn = xn.astype(BF16)

    def proj(lo, n):
        return jnp.dot(xn, w_ref[:, lo:lo + n], preferred_element_type=F32)

    cs = cs_ref[...]
    sn = sn_ref[...]

    def rotary(u):
        return u * cs + pltpu.roll(u, RET_DK // 2, 1) * sn

    uq = proj(OFF_RQ, RET_QK_W)
    uk = proj(OFF_RK, RET_QK_W)
    for hh in range(RET_HEADS):
        sl = slice(hh * RET_DK, (hh + 1) * RET_DK)
        rq_ref[:, sl] = (rotary(uq[:, sl]) * (RET_DK ** -0.5)).astype(BF16)
        rk_ref[:, sl] = rotary(uk[:, sl]).astype(BF16)
    rv_ref[...] = proj(OFF_RV, RET_V_W).astype(BF16)
    rg_ref[...] = proj(OFF_RG, RET_V_W)
    dq = proj(OFF_DQ, DIFF_W) * Q_SCALE
    dk = proj(OFF_DK, DIFF_W)
    dkf_ref[...] = dk
    dkb_ref[...] = dk.astype(BF16)
    dv = proj(OFF_DV, DIFF_W)
    dvf_ref[...] = dv
    if transposed:
        dq_ref[0] = dq.T.astype(BF16)
        dvt = dv.T.astype(BF16)
        ones = jnp.ones((ONES_ROWS, dvt.shape[1]), BF16)
        for hh in range(DIFF_HEADS):
            dvb_ref[0, hh * DV_AUG:hh * DV_AUG + DIFF_DV, :] = dvt[hh * DIFF_DV:(hh + 1) * DIFF_DV]
            dvb_ref[0, hh * DV_AUG + DIFF_DV:(hh + 1) * DV_AUG, :] = ones
    else:
        dq_ref[...] = dq.astype(BF16)
        dvb_ref[...] = dv.astype(BF16)
    gr_ref[...] = proj(OFF_GR, D_MODEL)
    gd_ref[...] = proj(OFF_GD, D_MODEL)


def _in_proj(x2d, g, w_bf, cs, sn, tm, transposed=False):
    rows = x2d.shape[0]
    n_pos = cs.shape[0] // tm
    row = lambda w: pl.BlockSpec((tm, w), lambda i: (i, 0))
    pos = pl.BlockSpec((tm, LANES), lambda i: (i % n_pos, 0))
    outs = [(RET_QK_W, BF16), (RET_QK_W, BF16), (RET_V_W, BF16), (RET_V_W, F32),
            (DIFF_W, BF16), (DIFF_W, BF16), (DIFF_W, F32), (DIFF_W, BF16), (DIFF_W, F32),
            (D_MODEL, F32), (D_MODEL, F32)]
    out_specs = [row(w) for w, _ in outs]
    out_shape = [jax.ShapeDtypeStruct((rows, w), dt) for w, dt in outs]
    if transposed:
        for o, w in ((4, DIFF_W), (7, DIFF_HEADS * DV_AUG)):
            out_specs[o] = pl.BlockSpec((1, w, tm), lambda i: (i, 0, 0))
            out_shape[o] = jax.ShapeDtypeStruct((rows // tm, w, tm), BF16)
    return pl.pallas_call(
        functools.partial(_inproj_kernel, transposed=transposed), grid=(rows // tm,),
        in_specs=[row(D_MODEL), _resident((1, D_MODEL)), _resident((D_MODEL, W_IN)), pos, pos],
        out_specs=out_specs, out_shape=out_shape,
        compiler_params=_cparams(1), name="in_proj",
    )(x2d, g, w_bf, cs, sn)


def _retention_kernel(rq_ref, rk_ref, rv_ref, rg_ref, s0_ref, dm_ref, qd_ref, kd_ref,
                      y_ref, sfin_ref, s_scr, *, sdec, with_output):
    t = pl.program_id(1)

    @pl.when(t == 0)
    def _():
        s_scr[...] = s0_ref[0]

    for hh in range(RET_HEADS):
        qs = slice(hh * RET_DK, (hh + 1) * RET_DK)
        vs = slice(hh * RET_DV, (hh + 1) * RET_DV)
        q = rq_ref[:, qs]
        k = rk_ref[:, qs]
        v = rv_ref[:, vs]
        state = s_scr[hh]
        if with_output:
            s = lax.dot_general(q, k, NT_DIMS, preferred_element_type=F32) * dm_ref[hh]
            o = jnp.dot(s.astype(BF16), v, preferred_element_type=F32)
            qd = (q.astype(F32) * qd_ref[hh]).astype(BF16)
            o = o + jnp.dot(qd, state.astype(BF16), preferred_element_type=F32)
            y = o * lax.rsqrt(jnp.mean(o * o, axis=-1, keepdims=True) + RET_EPS)
            g = rg_ref[:, vs]
            y_ref[:, vs] = (y * (g * jax.nn.sigmoid(g))).astype(BF16)
        else:
            y_ref[:, vs] = jnp.zeros((y_ref.shape[0], RET_DV), BF16)
        kd = (k.astype(F32) * kd_ref[hh]).astype(BF16)
        kv = lax.dot_general(kd, v, TN_DIMS, preferred_element_type=F32)
        s_scr[hh] = sdec[hh] * state + kv

    @pl.when(t == pl.num_programs(1) - 1)
    def _():
        sfin_ref[0] = s_scr[...]


def _retention(rq, rk, rv, rg, s0, n_seq, t, with_output=True):
    rows = rq.shape[0]
    nt = rows // n_seq // t
    dmat, qdec, kdec, sdec = _retention_tables(t)
    row = lambda w: pl.BlockSpec((t, w), lambda b, i: (b * nt + i, 0))
    s_shape = (1, RET_HEADS, RET_DK, RET_DV)
    if s0.shape[0] == 1:
        s0_spec = pl.BlockSpec(s_shape, lambda b, i: (0, 0, 0, 0))
    else:
        s0_spec = pl.BlockSpec(s_shape, lambda b, i: (b, 0, 0, 0))
    y, sfin = pl.pallas_call(
        functools.partial(_retention_kernel, sdec=sdec, with_output=with_output),
        grid=(n_seq, nt),
        in_specs=[row(RET_QK_W), row(RET_QK_W), row(RET_V_W), row(RET_V_W), s0_spec,
                  _resident(dmat.shape), _resident(qdec.shape), _resident(kdec.shape)],
        out_specs=[row(RET_V_W), pl.BlockSpec(s_shape, lambda b, i: (b, 0, 0, 0))],
        out_shape=[jax.ShapeDtypeStruct((rows, RET_V_W), BF16),
                   jax.ShapeDtypeStruct((n_seq,) + s_shape[1:], F32)],
        scratch_shapes=[pltpu.VMEM(s_shape[1:], F32)],
        compiler_params=_cparams(2), name="retention",
    )(rq, rk, rv, rg, s0, dmat, qdec, kdec)
    return y, sfin


def _lambda(lam_ref):
    lp = lam_ref[...]
    a = jnp.exp(jnp.sum(lp[0:1] * lp[1:2], axis=-1, keepdims=True))
    b = jnp.exp(jnp.sum(lp[2:3] * lp[3:4], axis=-1, keepdims=True))
    return a - b + LAM_INIT


def _stack_maps(q):
    qf = q.astype(F32)
    lane = lax.broadcasted_iota(jnp.int32, q.shape, 1)
    return jnp.concatenate([jnp.where(lane < DIFF_DH, qf, 0.0),
                            jnp.where(lane < DIFF_DH, 0.0, qf)], axis=0).astype(BF16)


def _mix_and_norm(o_all, lam, subg, n):
    o = o_all[:n] - lam * o_all[n:]
    y = o * lax.rsqrt(jnp.mean(o * o, axis=-1, keepdims=True) + DIFF_EPS) * subg
    return (y * (1.0 - LAM_INIT)).astype(BF16)


def _diff_attn_kernel(qt_ref, k_ref, vt_ref, mk_ref, mvt_ref, tdiag_ref, tsub_ref, tmeta_ref,
                      lam_ref, subg_ref, y_ref, qq_scr, m_scr, acc_scr, s_scr):
    i = pl.program_id(2)
    heads = range(HEADS_PER_STEP)
    hs = lambda h: slice(h * DIFF_DV, (h + 1) * DIFF_DV)
    vs = lambda h: slice(h * DV_AUG, (h + 1) * DV_AUG)

    for h in heads:
        qt = qt_ref[0, hs(h), :].astype(F32)
        row = lax.broadcasted_iota(jnp.int32, qt.shape, 0)
        qq_scr[h] = jnp.concatenate([jnp.where(row < DIFF_DH, qt, 0.0),
                                     jnp.where(row < DIFF_DH, 0.0, qt)], axis=1).astype(BF16)

    def scores(h, k_blk, tile):
        s = jnp.dot(k_blk, qq_scr[h], preferred_element_type=F32)
        return s if tile is None else s + tile

    for h in heads:
        s = scores(h, mk_ref[:, hs(h)], tmeta_ref[h, jnp.minimum(i, 1)])
        m0 = jnp.max(s, axis=0, keepdims=True)
        m_scr[h] = m0
        acc_scr[h] = jnp.dot(mvt_ref[vs(h), :], jnp.exp2(s - m0).astype(BF16),
                             preferred_element_type=F32)

    def qk(j):
        off = pl.multiple_of(j * TK, TK)
        return tuple(scores(h, k_ref[pl.ds(off, TK), hs(h)], None) for h in heads)

    def consume(j, s):
        m_prev = [m_scr[h] for h in heads]
        m_new = [jnp.maximum(m_prev[h], jnp.max(s[h], axis=0, keepdims=True)) for h in heads]
        p = [jnp.exp2(s[h] - m_new[h]).astype(BF16) for h in heads]
        alpha = [jnp.exp2(m_prev[h] - m_new[h]) for h in heads]
        for h in heads:
            m_scr[h] = m_new[h]
        pv = [jnp.dot(vt_ref[j, vs(h), :], p[h], preferred_element_type=F32) for h in heads]
        for h in heads:
            acc_scr[h] = alpha[h] * acc_scr[h] + pv[h]

    def qk_to_scratch(j, buf):
        for h, s in enumerate(qk(j)):
            s_scr[buf, h] = s

    def far_pair(t, carry):
        j = 2 * t
        qk_to_scratch(j + 1, 1)
        consume(j, [s_scr[0, h] for h in heads])
        qk_to_scratch(j + 2, 0)
        consume(j + 1, [s_scr[1, h] for h in heads])
        return carry

    j_sub = jnp.maximum(i - 1, 0)
    n_pairs = j_sub // 2
    qk_to_scratch(0, 0)
    lax.fori_loop(0, n_pairs, far_pair, 0)

    @pl.when(j_sub % 2 == 1)
    def _():
        consume(j_sub - 1, [s_scr[0, h] for h in heads])
        qk_to_scratch(j_sub, 0)

    pen = jnp.where(i == 0, NEG_INF, 0.0)
    s_diag = qk(i)
    consume(j_sub, [s_scr[0, h] + (tsub_ref[h] + pen) for h in heads])
    consume(i, [s_diag[h] + tdiag_ref[h] for h in heads])

    lam = _lambda(lam_ref)
    for h in heads:
        o_all = acc_scr[h, :DIFF_DV, :] / acc_scr[h, DIFF_DV:DIFF_DV + 1, :]
        o = o_all[:, :TQ] - lam * o_all[:, TQ:]
        y = o * lax.rsqrt(jnp.mean(o * o, axis=0, keepdims=True) + DIFF_EPS) * subg_ref[...]
        y_ref[:, hs(h)] = (y * (1.0 - LAM_INIT)).T.astype(BF16)


def _diff_attn(dqt, dk, dvt, mk, mvt, tdiag, tsub, tmeta, lam_p, subg_b, n_seq):
    rows = dk.shape[0]
    seq = rows // n_seq
    nq = seq // TQ
    hb = HEADS_PER_STEP
    wid = hb * DIFF_DV
    wid_v = hb * DV_AUG
    tile3 = lambda b, g, i: (g, 0, 0)
    slow = lambda shape, imap: pl.BlockSpec(shape, imap, pipeline_mode=pl.Buffered(1))
    return pl.pallas_call(
        _diff_attn_kernel, grid=(n_seq, DIFF_HEADS // hb, nq),
        in_specs=[pl.BlockSpec((1, wid, TQ), lambda b, g, i: (b * nq + i, g, 0)),
                  slow((seq, wid), lambda b, g, i: (b, g)),
                  slow((nq, wid_v, TK), lambda b, g, i: (b, g, 0)),
                  slow((N_META, wid), lambda b, g, i: (0, g)),
                  slow((wid_v, N_META), lambda b, g, i: (g, 0)),
                  slow((hb, TK, 2 * TQ), tile3),
                  slow((hb, TK, 2 * TQ), tile3),
                  slow((hb, 2, N_META, 2 * TQ), lambda b, g, i: (g, 0, 0, 0)),
                  slow((4, DIFF_DH), lambda b, g, i: (0, 0)),
                  slow((DIFF_DV, TQ), lambda b, g, i: (0, 0))],
        out_specs=pl.BlockSpec((TQ, wid), lambda b, g, i: (b * nq + i, g)),
        out_shape=jax.ShapeDtypeStruct((rows, DIFF_W), BF16),
        scratch_shapes=[pltpu.VMEM((hb, DIFF_DV, 2 * TQ), BF16), pltpu.VMEM((hb, 1, 2 * TQ), F32),
                        pltpu.VMEM((hb, DV_AUG, 2 * TQ), F32),
                        pltpu.VMEM((2, hb, TK, 2 * TQ), F32)],
        compiler_params=_cparams(3), name="diff_attn",
    )(dqt, dk, dvt, mk, mvt, tdiag, tsub, tmeta, lam_p, subg_b)


def _sample_attn_kernel(q_ref, ck_ref, cv_ref, nk_ref, nv_ref, tile_ref, lam_ref, subg_ref,
                        y_ref, k_scr, v_scr, *, past, n_new):
    pad = k_scr.shape[0] - past - n_new
    k_scr[0:past] = ck_ref[...].astype(BF16)
    v_scr[0:past] = cv_ref[...].astype(BF16)
    k_scr[past:past + n_new] = nk_ref[...]
    v_scr[past:past + n_new] = nv_ref[...]
    k_scr[past + n_new:] = jnp.zeros((pad, LANES), BF16)
    v_scr[past + n_new:] = jnp.zeros((pad, LANES), BF16)
    qq = _stack_maps(q_ref[...])
    s = lax.dot_general(qq, k_scr[...], NT_DIMS, preferred_element_type=F32)
    s = (s.reshape(2, n_new, s.shape[-1]) + tile_ref[0][None]).reshape(2 * n_new, s.shape[-1])
    m = jnp.max(s, axis=1, keepdims=True)
    p = jnp.exp2(s - m)
    l = jnp.sum(p, axis=1, keepdims=True)
    o_all = jnp.dot(p.astype(BF16), v_scr[...], preferred_element_type=F32) / l
    y_ref[...] = _mix_and_norm(o_all, _lambda(lam_ref), subg_ref[...], n_new)


def _sample_attn(dq, ck, cv, nk, nv, tile, lam_p, subg, n_seq, past, n_new):
    kpad = tile.shape[-1]
    return pl.pallas_call(
        functools.partial(_sample_attn_kernel, past=past, n_new=n_new),
        grid=(n_seq, DIFF_HEADS),
        in_specs=[pl.BlockSpec((n_new, LANES), lambda b, h: (b, h)),
                  pl.BlockSpec((past, LANES), lambda b, h: (b, h)),
                  pl.BlockSpec((past, LANES), lambda b, h: (b, h)),
                  pl.BlockSpec((n_new, LANES), lambda b, h: (b, h)),
                  pl.BlockSpec((n_new, LANES), lambda b, h: (b, h)),
                  pl.BlockSpec((1, n_new, kpad), lambda b, h: (h, 0, 0)),
                  pl.BlockSpec((4, DIFF_DH), lambda b, h: (0, 0)),
                  pl.BlockSpec((1, DIFF_DV), lambda b, h: (0, 0))],
        out_specs=pl.BlockSpec((n_new, LANES), lambda b, h: (b, h)),
        out_shape=jax.ShapeDtypeStruct((n_seq * n_new, DIFF_W), BF16),
        scratch_shapes=[pltpu.VMEM((kpad, LANES), BF16), pltpu.VMEM((kpad, LANES), BF16)],
        compiler_params=_cparams(2), name="sample_attn",
    )(dq, ck, cv, nk, nv, tile, lam_p, subg)


def _tail_kernel(h_ref, yr_ref, yd_ref, gr_ref, gd_ref, wrb_ref, wdb_ref, wo_ref, n2_ref,
                 wup_ref, wdn_ref, nf_ref, out_ref):
    a = jnp.dot(yr_ref[...], wrb_ref[...], preferred_element_type=F32)
    b = jnp.dot(yd_ref[...], wdb_ref[...], preferred_element_type=F32)
    merged = jax.nn.sigmoid(gr_ref[...]) * a + jax.nn.sigmoid(gd_ref[...]) * b
    h = h_ref[...] + jnp.dot(merged.astype(BF16), wo_ref[...], preferred_element_type=F32)
    xn = h * lax.rsqrt(jnp.mean(h * h, axis=-1, keepdims=True) + NORM_EPS) * n2_ref[...]
    xn = xn.astype(BF16)
    acc = jnp.zeros_like(h)
    for lo in range(0, D_FF, FF_CHUNK):
        n = min(FF_CHUNK, D_FF - lo)
        gate = jnp.dot(xn, wup_ref[:, lo:lo + n], preferred_element_type=F32)
        up = jnp.dot(xn, wup_ref[:, D_FF + lo:D_FF + lo + n], preferred_element_type=F32)
        act = (gate * jax.nn.sigmoid(gate) * up).astype(BF16)
        acc = acc + jnp.dot(act, wdn_ref[lo:lo + n, :], preferred_element_type=F32)
    h = h + acc
    out_ref[...] = h * lax.rsqrt(jnp.mean(h * h, axis=-1, keepdims=True) + NORM_EPS) * nf_ref[...]


def _tail(h2d, y_ret, y_diff, gr, gd, wrb, wdb, wo, n2, wup, wdn, nf, tm):
    rows = h2d.shape[0]
    row = pl.BlockSpec((tm, D_MODEL), lambda i: (i, 0))
    return pl.pallas_call(
        _tail_kernel, grid=(rows // tm,),
        in_specs=[row, row, row, row, row,
                  _resident(wrb.shape), _resident(wdb.shape), _resident(wo.shape),
                  _resident(n2.shape), _resident(wup.shape), _resident(wdn.shape),
                  _resident(nf.shape)],
        out_specs=row,
        out_shape=jax.ShapeDtypeStruct((rows, D_MODEL), F32),
        compiler_params=_cparams(1), name="tail",
    )(h2d, y_ret, y_diff, gr, gd, wrb, wdb, wo, n2, wup, wdn, nf)


def kernel(x_prompt, x_sample, cache_k, cache_v, state_ret, meta_tokens, rel_bias, norm1_g, w_in,
           lambda_q1, lambda_k1, lambda_q2, lambda_k2, diff_subln_g, w_ret_branch, w_diff_branch,
           w_o, norm2_g, w_ffn_up, w_ffn_down, normf_g):
    assert w_in.shape[0] == 1, "single-layer step only"
    bsz, seq, _ = x_prompt.shape
    dbsz, dseq, _ = x_sample.shape
    past = cache_k.shape[2]
    assert TQ == TK == TM_PROJ and TQ % CHUNK == 0 and DIFF_HEADS % HEADS_PER_STEP == 0
    assert seq % TQ == 0 and seq % T_RET == 0
    assert dseq <= CHUNK and past % CHUNK == 0 and meta_tokens.shape[0] == N_META

    w_in_bf = w_in[0].astype(BF16)
    g1 = norm1_g[0][None, :]
    lam_p = jnp.stack([lambda_q1[0], lambda_k1[0], lambda_q2[0], lambda_k2[0]])
    subg = diff_subln_g[0][None, :]
    wts = (w_ret_branch[0].astype(BF16), w_diff_branch[0].astype(BF16), w_o[0].astype(BF16),
           norm2_g[0][None, :], w_ffn_up[0].astype(BF16), w_ffn_down[0].astype(BF16),
           normf_g[None, :])

    kpad = ((past + dseq + LANES - 1) // LANES) * LANES
    both_maps = lambda ix: np.concatenate([ix.T, ix.T], axis=1)
    idx_diag = both_maps(_bias_idx(np.arange(TQ), np.arange(TK), TK))
    idx_sub = both_maps(_bias_idx(np.arange(TQ) + TK, np.arange(TK), TK))
    idx_meta = np.concatenate(
        [both_maps(_bias_idx(np.arange(TQ) + b * TQ, np.arange(N_META) - N_META, N_META))
         for b in range(2)], axis=0)
    idx_samp = _bias_idx(past + np.arange(dseq), np.arange(kpad), past + dseq)
    t_diag, t_sub, t_meta, t_samp = _bias_tiles(rel_bias, [idx_diag, idx_sub, idx_meta, idx_samp],
                                                [True, True, True, False])
    t_meta = t_meta.reshape(DIFF_HEADS, 2, N_META, 2 * TQ)

    cs_m, sn_m = _rotary_tables(np.arange(-N_META, 0))
    m_out = _in_proj(meta_tokens, g1, w_in_bf, cs_m, sn_m, N_META)
    zero_state = jnp.zeros((1, RET_HEADS, RET_DK, RET_DV), F32)
    _, s_meta = _retention(m_out[0], m_out[1], m_out[2], m_out[3], zero_state, 1, N_META,
                           with_output=False)
    mk = m_out[5]
    mv_aug = jnp.concatenate([m_out[7].reshape(N_META, DIFF_HEADS, DIFF_DV),
                              jnp.ones((N_META, DIFF_HEADS, ONES_ROWS), BF16)], axis=2)
    mvt = mv_aug.reshape(N_META, DIFF_HEADS * DV_AUG).T
    subg_b = jnp.broadcast_to(diff_subln_g[0][:, None], (DIFF_DV, TQ))

    cs_p, sn_p = _rotary_tables(np.arange(seq))
    x2d = x_prompt.reshape(bsz * seq, D_MODEL)
    (rq, rk, rv, rg, dqt, dkb, dkf, dvt, dvf, gr, gd) = _in_proj(x2d, g1, w_in_bf, cs_p, sn_p,
                                                                 TM_PROJ, transposed=True)
    y_ret, s_fin = _retention(rq, rk, rv, rg, s_meta, bsz, T_RET)
    y_diff = _diff_attn(dqt, dkb, dvt, mk, mvt, t_diag, t_sub, t_meta, lam_p, subg_b, bsz)
    y_prompt = _tail(x2d, y_ret, y_diff, gr, gd, *wts, TM_TAIL).reshape(bsz, seq, D_MODEL)
    k_rows = jnp.concatenate([jnp.broadcast_to(m_out[6][None], (bsz, N_META, DIFF_W)),
                              dkf.reshape(bsz, seq, DIFF_W)], axis=1)
    v_rows = jnp.concatenate([jnp.broadcast_to(m_out[8][None], (bsz, N_META, DIFF_W)),
                              dvf.reshape(bsz, seq, DIFF_W)], axis=1)

    cs_s, sn_s = _rotary_tables(np.tile(past + np.arange(dseq), dbsz))
    xs2d = x_sample.reshape(dbsz * dseq, D_MODEL)
    (rq, rk, rv, rg, dq, dkb, dkf_s, dvb, dvf_s, gr, gd) = _in_proj(xs2d, g1, w_in_bf, cs_s, sn_s,
                                                                     dbsz * dseq)
    y_ret, s_samp = _retention(rq, rk, rv, rg, state_ret[0], dbsz, dseq)
    ck = cache_k[0].reshape(dbsz * past, DIFF_W)
    cv = cache_v[0].reshape(dbsz * past, DIFF_W)
    y_diff = _sample_attn(dq, ck, cv, dkb, dvb, t_samp, lam_p, subg, dbsz, past, dseq)
    y_sample = _tail(xs2d, y_ret, y_diff, gr, gd, *wts, dbsz * dseq).reshape(dbsz, dseq, D_MODEL)

    heads = (DIFF_HEADS, DIFF_DV)
    return (y_prompt, y_sample,
            k_rows.reshape((1, bsz, N_META + seq) + heads),
            v_rows.reshape((1, bsz, N_META + seq) + heads),
            s_fin[None],
            dkf_s.reshape((1, dbsz, dseq) + heads),
            dvf_s.reshape((1, dbsz, dseq) + heads),
            s_samp[None])
```

```python
import functools
import math

import numpy as np
import jax
import jax.numpy as jnp
from jax import lax
from jax.experimental import pallas as pl
from jax.experimental.pallas import tpu as pltpu

F32 = jnp.float32
BF16 = jnp.bfloat16

D_MODEL = 1024
CHUNK = 64
N_META = 16
RET_HEADS = 4
RET_DK = D_MODEL // 8
RET_DV = 2 * RET_DK
RET_QK_W = RET_HEADS * RET_DK
RET_V_W = RET_HEADS * RET_DV
ROPE_BASE = 10000.0
RET_EPS = 1e-6
DIFF_HEADS = 8
DIFF_DH = D_MODEL // 16
DIFF_DV = 2 * DIFF_DH
DIFF_W = DIFF_HEADS * DIFF_DV
DIFF_EPS = 1e-5
N_BUCKETS = 32
MAX_DISTANCE = 128
D_FF = ((8 * D_MODEL + 3 * 256 - 1) // (3 * 256)) * 256
NORM_EPS = 1e-6
NEG_INF = -1e30
LAM_INIT = 0.8 - 0.6 * math.exp(-0.3 * 0)
LOG2E = math.log2(math.e)
Q_SCALE = DIFF_DH ** -0.5 * LOG2E
ONES_ROWS = 16
DV_AUG = DIFF_DV + ONES_ROWS

OFF_RQ = 0
OFF_RK = OFF_RQ + RET_QK_W
OFF_RV = OFF_RK + RET_QK_W
OFF_RG = OFF_RV + RET_V_W
OFF_DQ = OFF_RG + RET_V_W
OFF_DK = OFF_DQ + DIFF_W
OFF_DV = OFF_DK + DIFF_W
OFF_GR = OFF_DV + DIFF_W
OFF_GD = OFF_GR + D_MODEL
W_IN = OFF_GD + D_MODEL

LANES = 128
VMEM_LIMIT_BYTES = 56 * 1024 * 1024
TM_PROJ = 256
TM_TAIL = 256
T_RET = 128
TQ = 256
TK = 256
HEADS_PER_STEP = 4
FF_CHUNK = 512

NT_DIMS = (((1,), (1,)), ((), ()))
TN_DIMS = (((0,), (0,)), ((), ()))


def _cparams(n_axes):
    return pltpu.CompilerParams(dimension_semantics=("arbitrary",) * n_axes,
                                vmem_limit_bytes=VMEM_LIMIT_BYTES)


def _resident(shape):
    nd = len(shape)
    return pl.BlockSpec(shape, lambda *_: (0,) * nd, pipeline_mode=pl.Buffered(1))


def _t5_bucket_np(rel):
    nb = N_BUCKETS // 2
    max_exact = nb // 2
    ret = np.where(rel > 0, nb, 0)
    n = np.abs(rel)
    nf = np.maximum(n, max_exact).astype(np.float64)
    large = max_exact + (np.log(nf / max_exact) / math.log(MAX_DISTANCE / max_exact)
                         * (nb - max_exact)).astype(np.int32)
    large = np.minimum(large, nb - 1)
    return (ret + np.where(n < max_exact, n, large)).astype(np.int32)


def _bias_idx(qpos, kpos, n_real):
    qpos = np.asarray(qpos)[:, None]
    kpos = np.asarray(kpos)[None, :]
    vis = (np.floor_divide(kpos, CHUNK) <= np.floor_divide(qpos, CHUNK))
    vis = vis & (np.arange(kpos.shape[1])[None, :] < n_real)
    return np.where(vis, _t5_bucket_np(kpos - qpos), -1).astype(np.int32)


def _rotary_tables(pos):
    half = RET_DK // 2
    freq = 1.0 / (ROPE_BASE ** jnp.linspace(0.0, 1.0, half, dtype=F32))
    ang = jnp.asarray(pos, F32)[:, None] * freq[None, :]
    cos, sin = jnp.cos(ang), jnp.sin(ang)
    return jnp.concatenate([cos, cos], axis=1), jnp.concatenate([-sin, sin], axis=1)


def _retention_tables(t):
    gam = 1.0 - 2.0 ** (-5.0 - np.arange(RET_HEADS, dtype=np.float64))
    lg = np.log(gam)[:, None, None]
    n = np.arange(t)[:, None]
    m = np.arange(t)[None, :]
    cn, cm = n // CHUNK, m // CHUNK
    expo = np.where(cm == cn, np.abs(n - m), n - m).astype(np.float64)[None]
    dmat = np.where((cm <= cn)[None], np.exp(lg * expo), 0.0)
    qdec = np.exp(lg[:, :, 0] * (np.arange(t)[None, :] + 1.0))[:, :, None]
    kdec = np.exp(lg[:, :, 0] * (t - 1.0 - np.arange(t)[None, :]))[:, :, None]
    sdec = np.exp(lg[:, 0, 0] * t)
    return (jnp.asarray(dmat, F32), jnp.asarray(qdec, F32), jnp.asarray(kdec, F32),
            tuple(float(s) for s in sdec))


def _bias_kernel(rb_ref, *refs, n_tiles, shifted):
    idx_refs, out_refs = refs[:n_tiles], refs[n_tiles:]
    h = pl.program_id(0)
    far = rb_ref[N_BUCKETS // 2 - 1, h]
    for idx_ref, out_ref, sh in zip(idx_refs, out_refs, shifted):
        idx = idx_ref[...]
        acc = jnp.full(idx.shape, NEG_INF, F32)
        for b in range(N_BUCKETS):
            val = (rb_ref[b, h] - far if sh else rb_ref[b, h]) * LOG2E
            acc = jnp.where(idx == b, val, acc)
        out_ref[0] = acc


def _bias_tiles(rel_bias, idx_list, shifted):
    n = len(idx_list)
    in_specs = [pl.BlockSpec(memory_space=pltpu.SMEM)]
    in_specs += [pl.BlockSpec(ix.shape, lambda h: (0, 0)) for ix in idx_list]
    out_specs = [pl.BlockSpec((1,) + ix.shape, lambda h: (h, 0, 0)) for ix in idx_list]
    out_shape = [jax.ShapeDtypeStruct((DIFF_HEADS,) + ix.shape, F32) for ix in idx_list]
    return pl.pallas_call(
        functools.partial(_bias_kernel, n_tiles=n, shifted=tuple(shifted)),
        grid=(DIFF_HEADS,), in_specs=in_specs, out_specs=out_specs, out_shape=out_shape,
        compiler_params=_cparams(1), name="bias_tiles",
    )(rel_bias, *[jnp.asarray(ix) for ix in idx_list])


def _inproj_kernel(x_ref, g_ref, w_ref, cs_ref, sn_ref,
                   rq_ref, rk_ref, rv_ref, rg_ref, dq_ref, dkb_ref, dkf_ref,
                   dvb_ref, dvf_ref, gr_ref, gd_ref, *, transposed):
    x = x_ref[...]
    xn = x * lax.rsqrt(jnp.mean(x * x, axis=-1, keepdims=True) + NORM_EPS) * g_ref[...]
    xn = xn.astype(BF16)

    def proj(lo, n):
        return jnp.dot(xn, w_ref[:, lo:lo + n], preferred_element_type=F32)

    cs = cs_ref[...]
    sn = sn_ref[...]

    def rotary(u):
        return u * cs + pltpu.roll(u, RET_DK // 2, 1) * sn

    uq = proj(OFF_RQ, RET_QK_W)
    uk = proj(OFF_RK, RET_QK_W)
    for hh in range(RET_HEADS):
        sl = slice(hh * RET_DK, (hh + 1) * RET_DK)
        rq_ref[:, sl] = (rotary(uq[:, sl]) * (RET_DK ** -0.5)).astype(BF16)
        rk_ref[:, sl] = rotary(uk[:, sl]).astype(BF16)
    rv_ref[...] = proj(OFF_RV, RET_V_W).astype(BF16)
    rg_ref[...] = proj(OFF_RG, RET_V_W)
    dq = proj(OFF_DQ, DIFF_W) * Q_SCALE
    dk = proj(OFF_DK, DIFF_W)
    dkf_ref[...] = dk
    dkb_ref[...] = dk.astype(BF16)
    dv = proj(OFF_DV, DIFF_W)
    dvf_ref[...] = dv
    if transposed:
        dq_ref[0] = dq.T.astype(BF16)
        dvt = dv.T.astype(BF16)
        ones = jnp.ones((ONES_ROWS, dvt.shape[1]), BF16)
        for hh in range(DIFF_HEADS):
            dvb_ref[0, hh * DV_AUG:hh * DV_AUG + DIFF_DV, :] = dvt[hh * DIFF_DV:(hh + 1) * DIFF_DV]
            dvb_ref[0, hh * DV_AUG + DIFF_DV:(hh + 1) * DV_AUG, :] = ones
    else:
        dq_ref[...] = dq.astype(BF16)
        dvb_ref[...] = dv.astype(BF16)
    gr_ref[...] = proj(OFF_GR, D_MODEL)
    gd_ref[...] = proj(OFF_GD, D_MODEL)


def _in_proj(x2d, g, w_bf, cs, sn, tm, transposed=False):
    rows = x2d.shape[0]
    n_pos = cs.shape[0] // tm
    row = lambda w: pl.BlockSpec((tm, w), lambda i: (i, 0))
    pos = pl.BlockSpec((tm, LANES), lambda i: (i % n_pos, 0))
    outs = [(RET_QK_W, BF16), (RET_QK_W, BF16), (RET_V_W, BF16), (RET_V_W, F32),
            (DIFF_W, BF16), (DIFF_W, BF16), (DIFF_W, F32), (DIFF_W, BF16), (DIFF_W, F32),
            (D_MODEL, F32), (D_MODEL, F32)]
    out_specs = [row(w) for w, _ in outs]
    out_shape = [jax.ShapeDtypeStruct((rows, w), dt) for w, dt in outs]
    if transposed:
        for o, w in ((4, DIFF_W), (7, DIFF_HEADS * DV_AUG)):
            out_specs[o] = pl.BlockSpec((1, w, tm), lambda i: (i, 0, 0))
            out_shape[o] = jax.ShapeDtypeStruct((rows // tm, w, tm), BF16)
    return pl.pallas_call(
        functools.partial(_inproj_kernel, transposed=transposed), grid=(rows // tm,),
        in_specs=[row(D_MODEL), _resident((1, D_MODEL)), _resident((D_MODEL, W_IN)), pos, pos],
        out_specs=out_specs, out_shape=out_shape,
        compiler_params=_cparams(1), name="in_proj",
    )(x2d, g, w_bf, cs, sn)


def _retention_kernel(rq_ref, rk_ref, rv_ref, rg_ref, s0_ref, dm_ref, qd_ref, kd_ref,
                      y_ref, sfin_ref, s_scr, *, sdec, with_output):
    t = pl.program_id(1)

    @pl.when(t == 0)
    def _():
        s_scr[...] = s0_ref[0]

    for hh in range(RET_HEADS):
        qs = slice(hh * RET_DK, (hh + 1) * RET_DK)
        vs = slice(hh * RET_DV, (hh + 1) * RET_DV)
        q = rq_ref[:, qs]
        k = rk_ref[:, qs]
        v = rv_ref[:, vs]
        state = s_scr[hh]
        if with_output:
            s = lax.dot_general(q, k, NT_DIMS, preferred_element_type=F32) * dm_ref[hh]
            o = jnp.dot(s.astype(BF16), v, preferred_element_type=F32)
            qd = (q.astype(F32) * qd_ref[hh]).astype(BF16)
            o = o + jnp.dot(qd, state.astype(BF16), preferred_element_type=F32)
            y = o * lax.rsqrt(jnp.mean(o * o, axis=-1, keepdims=True) + RET_EPS)
            g = rg_ref[:, vs]
            y_ref[:, vs] = (y * (g * jax.nn.sigmoid(g))).astype(BF16)
        else:
            y_ref[:, vs] = jnp.zeros((y_ref.shape[0], RET_DV), BF16)
        kd = (k.astype(F32) * kd_ref[hh]).astype(BF16)
        kv = lax.dot_general(kd, v, TN_DIMS, preferred_element_type=F32)
        s_scr[hh] = sdec[hh] * state + kv

    @pl.when(t == pl.num_programs(1) - 1)
    def _():
        sfin_ref[0] = s_scr[...]


def _retention(rq, rk, rv, rg, s0, n_seq, t, with_output=True):
    rows = rq.shape[0]
    nt = rows // n_seq // t
    dmat, qdec, kdec, sdec = _retention_tables(t)
    row = lambda w: pl.BlockSpec((t, w), lambda b, i: (b * nt + i, 0))
    s_shape = (1, RET_HEADS, RET_DK, RET_DV)
    if s0.shape[0] == 1:
        s0_spec = pl.BlockSpec(s_shape, lambda b, i: (0, 0, 0, 0))
    else:
        s0_spec = pl.BlockSpec(s_shape, lambda b, i: (b, 0, 0, 0))
    y, sfin = pl.pallas_call(
        functools.partial(_retention_kernel, sdec=sdec, with_output=with_output),
        grid=(n_seq, nt),
        in_specs=[row(RET_QK_W), row(RET_QK_W), row(RET_V_W), row(RET_V_W), s0_spec,
                  _resident(dmat.shape), _resident(qdec.shape), _resident(kdec.shape)],
        out_specs=[row(RET_V_W), pl.BlockSpec(s_shape, lambda b, i: (b, 0, 0, 0))],
        out_shape=[jax.ShapeDtypeStruct((rows, RET_V_W), BF16),
                   jax.ShapeDtypeStruct((n_seq,) + s_shape[1:], F32)],
        scratch_shapes=[pltpu.VMEM(s_shape[1:], F32)],
        compiler_params=_cparams(2), name="retention",
    )(rq, rk, rv, rg, s0, dmat, qdec, kdec)
    return y, sfin


def _lambda(lam_ref):
    lp = lam_ref[...]
    a = jnp.exp(jnp.sum(lp[0:1] * lp[1:2], axis=-1, keepdims=True))
    b = jnp.exp(jnp.sum(lp[2:3] * lp[3:4], axis=-1, keepdims=True))
    return a - b + LAM_INIT


def _stack_maps(q):
    qf = q.astype(F32)
    lane = lax.broadcasted_iota(jnp.int32, q.shape, 1)
    return jnp.concatenate([jnp.where(lane < DIFF_DH, qf, 0.0),
                            jnp.where(lane < DIFF_DH, 0.0, qf)], axis=0).astype(BF16)


def _mix_and_norm(o_all, lam, subg, n):
    o = o_all[:n] - lam * o_all[n:]
    y = o * lax.rsqrt(jnp.mean(o * o, axis=-1, keepdims=True) + DIFF_EPS) * subg
    return (y * (1.0 - LAM_INIT)).astype(BF16)


def _diff_attn_kernel(qt_ref, k_ref, vt_ref, mk_ref, mvt_ref, tdiag_ref, tsub_ref, tmeta_ref,
                      lam_ref, subg_ref, y_ref, qq_scr, m_scr, acc_scr, s_scr):
    i = pl.program_id(2)
    heads = range(HEADS_PER_STEP)
    hs = lambda h: slice(h * DIFF_DV, (h + 1) * DIFF_DV)
    vs = lambda h: slice(h * DV_AUG, (h + 1) * DV_AUG)

    for h in heads:
        qt = qt_ref[0, hs(h), :].astype(F32)
        row = lax.broadcasted_iota(jnp.int32, qt.shape, 0)
        qq_scr[h] = jnp.concatenate([jnp.where(row < DIFF_DH, qt, 0.0),
                                     jnp.where(row < DIFF_DH, 0.0, qt)], axis=1).astype(BF16)

    def scores(h, k_blk, tile):
        s = jnp.dot(k_blk, qq_scr[h], preferred_element_type=F32)
        return s if tile is None else s + tile

    for h in heads:
        m_scr[h] = jnp.full((1, 2 * TQ), NEG_INF, F32)
        acc_scr[h] = jnp.zeros((DV_AUG, 2 * TQ), F32)

    def meta_state():
        out = []
        for h in heads:
            s = scores(h, mk_ref[:, hs(h)], tmeta_ref[h, jnp.minimum(i, 1)])
            m0 = jnp.max(s, axis=0, keepdims=True)
            out.append((m0, jnp.dot(mvt_ref[vs(h), :], jnp.exp2(s - m0).astype(BF16),
                                    preferred_element_type=F32)))
        return out

    def qk(j):
        off = pl.multiple_of(j * TK, TK)
        return tuple(scores(h, k_ref[pl.ds(off, TK), hs(h)], None) for h in heads)

    def consume(j, s):
        m_prev = [m_scr[h] for h in heads]
        m_new = [jnp.maximum(m_prev[h], jnp.max(s[h], axis=0, keepdims=True)) for h in heads]
        p = [jnp.exp2(s[h] - m_new[h]).astype(BF16) for h in heads]
        alpha = [jnp.exp2(m_prev[h] - m_new[h]) for h in heads]
        for h in heads:
            m_scr[h] = m_new[h]
        pv = [jnp.dot(vt_ref[j, vs(h), :], p[h], preferred_element_type=F32) for h in heads]
        for h in heads:
            acc_scr[h] = alpha[h] * acc_scr[h] + pv[h]

    def qk_to_scratch(j, buf):
        for h, s in enumerate(qk(j)):
            s_scr[buf, h] = s

    def far_pair(t, carry):
        j = 2 * t
        qk_to_scratch(j + 1, 1)
        consume(j, [s_scr[0, h] for h in heads])
        qk_to_scratch(j + 2, 0)
        consume(j + 1, [s_scr[1, h] for h in heads])
        return carry

    j_sub = jnp.maximum(i - 1, 0)
    n_pairs = j_sub // 2
    qk_to_scratch(0, 0)
    lax.fori_loop(0, n_pairs, far_pair, 0)

    def finish(meta):
        lam = _lambda(lam_ref)
        for h in heads:
            m_meta, acc_meta = meta[h]
            m_main = m_scr[h]
            m = jnp.maximum(m_main, m_meta)
            acc = jnp.exp2(m_main - m) * acc_scr[h] + jnp.exp2(m_meta - m) * acc_meta
            o_all = acc[:DIFF_DV] / acc[DIFF_DV:DIFF_DV + 1]
            o = o_all[:, :TQ] - lam * o_all[:, TQ:]
            y = o * lax.rsqrt(jnp.mean(o * o, axis=0, keepdims=True) + DIFF_EPS) * subg_ref[...]
            y_ref[:, hs(h)] = (y * (1.0 - LAM_INIT)).T.astype(BF16)

    pen = jnp.where(i == 0, 3.0 * NEG_INF, 0.0)

    def last_blocks(sub_buf):
        meta = meta_state()
        s_diag = qk(i)
        consume(j_sub, [s_scr[sub_buf, h] + (tsub_ref[h] + pen) for h in heads])
        consume(i, [s_diag[h] + tdiag_ref[h] for h in heads])
        finish(meta)

    @pl.when(j_sub % 2 == 1)
    def _():
        qk_to_scratch(j_sub, 1)
        consume(j_sub - 1, [s_scr[0, h] for h in heads])
        last_blocks(1)

    @pl.when(j_sub % 2 == 0)
    def _():
        last_blocks(0)


def _diff_attn(dqt, dk, dvt, mk, mvt, tdiag, tsub, tmeta, lam_p, subg_b, n_seq):
    rows = dk.shape[0]
    seq = rows // n_seq
    nq = seq // TQ
    hb = HEADS_PER_STEP
    wid = hb * DIFF_DV
    wid_v = hb * DV_AUG
    tile3 = lambda b, g, i: (g, 0, 0)
    slow = lambda shape, imap: pl.BlockSpec(shape, imap, pipeline_mode=pl.Buffered(1))
    return pl.pallas_call(
        _diff_attn_kernel, grid=(n_seq, DIFF_HEADS // hb, nq),
        in_specs=[pl.BlockSpec((1, wid, TQ), lambda b, g, i: (b * nq + i, g, 0)),
                  slow((seq, wid), lambda b, g, i: (b, g)),
                  slow((nq, wid_v, TK), lambda b, g, i: (b, g, 0)),
                  slow((N_META, wid), lambda b, g, i: (0, g)),
                  slow((wid_v, N_META), lambda b, g, i: (g, 0)),
                  slow((hb, TK, 2 * TQ), tile3),
                  slow((hb, TK, 2 * TQ), tile3),
                  slow((hb, 2, N_META, 2 * TQ), lambda b, g, i: (g, 0, 0, 0)),
                  slow((4, DIFF_DH), lambda b, g, i: (0, 0)),
                  slow((DIFF_DV, TQ), lambda b, g, i: (0, 0))],
        out_specs=pl.BlockSpec((TQ, wid), lambda b, g, i: (b * nq + i, g)),
        out_shape=jax.ShapeDtypeStruct((rows, DIFF_W), BF16),
        scratch_shapes=[pltpu.VMEM((hb, DIFF_DV, 2 * TQ), BF16), pltpu.VMEM((hb, 1, 2 * TQ), F32),
                        pltpu.VMEM((hb, DV_AUG, 2 * TQ), F32),
                        pltpu.VMEM((2, hb, TK, 2 * TQ), F32)],
        compiler_params=_cparams(3), name="diff_attn",
    )(dqt, dk, dvt, mk, mvt, tdiag, tsub, tmeta, lam_p, subg_b)


def _sample_attn_kernel(q_ref, ck_ref, cv_ref, nk_ref, nv_ref, tile_ref, lam_ref, subg_ref,
                        y_ref, k_scr, v_scr, *, past, n_new):
    pad = k_scr.shape[1] - past - n_new
    lam = _lambda(lam_ref)
    for h in range(DIFF_HEADS):
        hs = slice(h * DIFF_DV, (h + 1) * DIFF_DV)
        k_scr[h, 0:past] = ck_ref[0, pl.ds(h, past, stride=DIFF_HEADS), :].astype(BF16)
        v_scr[h, 0:past] = cv_ref[0, pl.ds(h, past, stride=DIFF_HEADS), :].astype(BF16)
        k_scr[h, past:past + n_new] = nk_ref[:, hs]
        v_scr[h, past:past + n_new] = nv_ref[:, hs]
        k_scr[h, past + n_new:] = jnp.zeros((pad, LANES), BF16)
        v_scr[h, past + n_new:] = jnp.zeros((pad, LANES), BF16)
        qq = _stack_maps(q_ref[:, hs])
        s = lax.dot_general(qq, k_scr[h], NT_DIMS, preferred_element_type=F32)
        s = (s.reshape(2, n_new, s.shape[-1]) + tile_ref[h][None]).reshape(2 * n_new, s.shape[-1])
        m = jnp.max(s, axis=1, keepdims=True)
        p = jnp.exp2(s - m)
        l = jnp.sum(p, axis=1, keepdims=True)
        o_all = jnp.dot(p.astype(BF16), v_scr[h], preferred_element_type=F32) / l
        y_ref[:, hs] = _mix_and_norm(o_all, lam, subg_ref[...], n_new)


def _sample_attn(dq, ck, cv, nk, nv, tile, lam_p, subg, n_seq, past, n_new):
    kpad = tile.shape[-1]
    row = pl.BlockSpec((n_new, DIFF_W), lambda b: (b, 0))
    cache = pl.BlockSpec((1, past * DIFF_HEADS, DIFF_DV), lambda b: (b, 0, 0))
    return pl.pallas_call(
        functools.partial(_sample_attn_kernel, past=past, n_new=n_new),
        grid=(n_seq,),
        in_specs=[row, cache, cache, row, row, _resident(tile.shape),
                  _resident((4, DIFF_DH)), _resident((1, DIFF_DV))],
        out_specs=row,
        out_shape=jax.ShapeDtypeStruct((n_seq * n_new, DIFF_W), BF16),
        scratch_shapes=[pltpu.VMEM((DIFF_HEADS, kpad, LANES), BF16),
                        pltpu.VMEM((DIFF_HEADS, kpad, LANES), BF16)],
        compiler_params=_cparams(1), name="sample_attn",
    )(dq, ck, cv, nk, nv, tile, lam_p, subg)


def _tail_kernel(h_ref, yr_ref, yd_ref, gr_ref, gd_ref, wrb_ref, wdb_ref, wo_ref, n2_ref,
                 wup_ref, wdn_ref, nf_ref, out_ref):
    a = jnp.dot(yr_ref[...], wrb_ref[...], preferred_element_type=F32)
    b = jnp.dot(yd_ref[...], wdb_ref[...], preferred_element_type=F32)
    merged = jax.nn.sigmoid(gr_ref[...]) * a + jax.nn.sigmoid(gd_ref[...]) * b
    h = h_ref[...] + jnp.dot(merged.astype(BF16), wo_ref[...], preferred_element_type=F32)
    xn = h * lax.rsqrt(jnp.mean(h * h, axis=-1, keepdims=True) + NORM_EPS) * n2_ref[...]
    xn = xn.astype(BF16)
    acc = jnp.zeros_like(h)
    for lo in range(0, D_FF, FF_CHUNK):
        n = min(FF_CHUNK, D_FF - lo)
        gate = jnp.dot(xn, wup_ref[:, lo:lo + n], preferred_element_type=F32)
        up = jnp.dot(xn, wup_ref[:, D_FF + lo:D_FF + lo + n], preferred_element_type=F32)
        act = (gate * jax.nn.sigmoid(gate) * up).astype(BF16)
        acc = acc + jnp.dot(act, wdn_ref[lo:lo + n, :], preferred_element_type=F32)
    h = h + acc
    out_ref[...] = h * lax.rsqrt(jnp.mean(h * h, axis=-1, keepdims=True) + NORM_EPS) * nf_ref[...]


def _tail(h2d, y_ret, y_diff, gr, gd, wrb, wdb, wo, n2, wup, wdn, nf, tm):
    rows = h2d.shape[0]
    row = pl.BlockSpec((tm, D_MODEL), lambda i: (i, 0))
    return pl.pallas_call(
        _tail_kernel, grid=(rows // tm,),
        in_specs=[row, row, row, row, row,
                  _resident(wrb.shape), _resident(wdb.shape), _resident(wo.shape),
                  _resident(n2.shape), _resident(wup.shape), _resident(wdn.shape),
                  _resident(nf.shape)],
        out_specs=row,
        out_shape=jax.ShapeDtypeStruct((rows, D_MODEL), F32),
        compiler_params=_cparams(1), name="tail",
    )(h2d, y_ret, y_diff, gr, gd, wrb, wdb, wo, n2, wup, wdn, nf)


def kernel(x_prompt, x_sample, cache_k, cache_v, state_ret, meta_tokens, rel_bias, norm1_g, w_in,
           lambda_q1, lambda_k1, lambda_q2, lambda_k2, diff_subln_g, w_ret_branch, w_diff_branch,
           w_o, norm2_g, w_ffn_up, w_ffn_down, normf_g):
    assert w_in.shape[0] == 1, "single-layer step only"
    bsz, seq, _ = x_prompt.shape
    dbsz, dseq, _ = x_sample.shape
    past = cache_k.shape[2]
    assert TQ == TK == TM_PROJ and TQ % CHUNK == 0 and DIFF_HEADS % HEADS_PER_STEP == 0
    assert seq % TQ == 0 and seq % T_RET == 0
    assert dseq <= CHUNK and past % CHUNK == 0 and meta_tokens.shape[0] == N_META

    w_in_bf = w_in[0].astype(BF16)
    g1 = norm1_g[0][None, :]
    lam_p = jnp.stack([lambda_q1[0], lambda_k1[0], lambda_q2[0], lambda_k2[0]])
    subg = diff_subln_g[0][None, :]
    wts = (w_ret_branch[0].astype(BF16), w_diff_branch[0].astype(BF16), w_o[0].astype(BF16),
           norm2_g[0][None, :], w_ffn_up[0].astype(BF16), w_ffn_down[0].astype(BF16),
           normf_g[None, :])

    kpad = ((past + dseq + LANES - 1) // LANES) * LANES
    both_maps = lambda ix: np.concatenate([ix.T, ix.T], axis=1)
    idx_diag = both_maps(_bias_idx(np.arange(TQ), np.arange(TK), TK))
    idx_sub = both_maps(_bias_idx(np.arange(TQ) + TK, np.arange(TK), TK))
    idx_meta = np.concatenate(
        [both_maps(_bias_idx(np.arange(TQ) + b * TQ, np.arange(N_META) - N_META, N_META))
         for b in range(2)], axis=0)
    idx_samp = _bias_idx(past + np.arange(dseq), np.arange(kpad), past + dseq)
    t_diag, t_sub, t_meta, t_samp = _bias_tiles(rel_bias, [idx_diag, idx_sub, idx_meta, idx_samp],
                                                [True, True, True, False])
    t_meta = t_meta.reshape(DIFF_HEADS, 2, N_META, 2 * TQ)

    cs_m, sn_m = _rotary_tables(np.arange(-N_META, 0))
    m_out = _in_proj(meta_tokens, g1, w_in_bf, cs_m, sn_m, N_META)
    zero_state = jnp.zeros((1, RET_HEADS, RET_DK, RET_DV), F32)
    _, s_meta = _retention(m_out[0], m_out[1], m_out[2], m_out[3], zero_state, 1, N_META,
                           with_output=False)
    mk = m_out[5]
    mv_aug = jnp.concatenate([m_out[7].reshape(N_META, DIFF_HEADS, DIFF_DV),
                              jnp.ones((N_META, DIFF_HEADS, ONES_ROWS), BF16)], axis=2)
    mvt = mv_aug.reshape(N_META, DIFF_HEADS * DV_AUG).T
    subg_b = jnp.broadcast_to(diff_subln_g[0][:, None], (DIFF_DV, TQ))

    cs_p, sn_p = _rotary_tables(np.arange(seq))
    x2d = x_prompt.reshape(bsz * seq, D_MODEL)
    (rq, rk, rv, rg, dqt, dkb, dkf, dvt, dvf, gr, gd) = _in_proj(x2d, g1, w_in_bf, cs_p, sn_p,
                                                                 TM_PROJ, transposed=True)
    y_ret, s_fin = _retention(rq, rk, rv, rg, s_meta, bsz, T_RET)
    y_diff = _diff_attn(dqt, dkb, dvt, mk, mvt, t_diag, t_sub, t_meta, lam_p, subg_b, bsz)
    y_prompt = _tail(x2d, y_ret, y_diff, gr, gd, *wts, TM_TAIL).reshape(bsz, seq, D_MODEL)
    k_rows = jnp.concatenate([jnp.broadcast_to(m_out[6][None], (bsz, N_META, DIFF_W)),
                              dkf.reshape(bsz, seq, DIFF_W)], axis=1)
    v_rows = jnp.concatenate([jnp.broadcast_to(m_out[8][None], (bsz, N_META, DIFF_W)),
                              dvf.reshape(bsz, seq, DIFF_W)], axis=1)

    cs_s, sn_s = _rotary_tables(np.tile(past + np.arange(dseq), dbsz))
    xs2d = x_sample.reshape(dbsz * dseq, D_MODEL)
    (rq, rk, rv, rg, dq, dkb, dkf_s, dvb, dvf_s, gr, gd) = _in_proj(xs2d, g1, w_in_bf, cs_s, sn_s,
                                                                     dbsz * dseq)
    y_ret, s_samp = _retention(rq, rk, rv, rg, state_ret[0], dbsz, dseq)
    ck = cache_k[0].reshape(dbsz, past * DIFF_HEADS, DIFF_DV)
    cv = cache_v[0].reshape(dbsz, past * DIFF_HEADS, DIFF_DV)
    y_diff = _sample_attn(dq, ck, cv, dkb, dvb, t_samp, lam_p, subg, dbsz, past, dseq)
    y_sample = _tail(xs2d, y_ret, y_diff, gr, gd, *wts, dbsz * dseq).reshape(dbsz, dseq, D_MODEL)

    heads = (DIFF_HEADS, DIFF_DV)
    return (y_prompt, y_sample,
            k_rows.reshape((1, bsz, N_META + seq) + heads),
            v_rows.reshape((1, bsz, N_META + seq) + heads),
            s_fin[None],
            dkf_s.reshape((1, dbsz, dseq) + heads),
            dvf_s.reshape((1, dbsz, dseq) + heads),
            s_samp[None])
```

```python
import functools
import math

import numpy as np
import jax
import jax.numpy as jnp
from jax import lax
from jax.experimental import pallas as pl
from jax.experimental.pallas import tpu as pltpu

F32 = jnp.float32
BF16 = jnp.bfloat16

D_MODEL = 1024
CHUNK = 64
N_META = 16
RET_HEADS = 4
RET_DK = D_MODEL // 8
RET_DV = 2 * RET_DK
RET_QK_W = RET_HEADS * RET_DK
RET_V_W = RET_HEADS * RET_DV
ROPE_BASE = 10000.0
RET_EPS = 1e-6
DIFF_HEADS = 8
DIFF_DH = D_MODEL // 16
DIFF_DV = 2 * DIFF_DH
DIFF_W = DIFF_HEADS * DIFF_DV
DIFF_EPS = 1e-5
N_BUCKETS = 32
MAX_DISTANCE = 128
D_FF = ((8 * D_MODEL + 3 * 256 - 1) // (3 * 256)) * 256
NORM_EPS = 1e-6
NEG_INF = -1e30
LAM_INIT = 0.8 - 0.6 * math.exp(-0.3 * 0)
LOG2E = math.log2(math.e)
Q_SCALE = DIFF_DH ** -0.5 * LOG2E
ONES_ROWS = 16
DV_AUG = DIFF_DV + ONES_ROWS

OFF_RQ = 0
OFF_RK = OFF_RQ + RET_QK_W
OFF_RV = OFF_RK + RET_QK_W
OFF_RG = OFF_RV + RET_V_W
OFF_DQ = OFF_RG + RET_V_W
OFF_DK = OFF_DQ + DIFF_W
OFF_DV = OFF_DK + DIFF_W
OFF_GR = OFF_DV + DIFF_W
OFF_GD = OFF_GR + D_MODEL
W_IN = OFF_GD + D_MODEL

LANES = 128
VMEM_LIMIT_BYTES = 56 * 1024 * 1024
TM_PROJ = 256
TM_TAIL = 256
TQ = 256
TK = 256
HEADS_PER_STEP = 4
FF_CHUNK = 512

NT_DIMS = (((1,), (1,)), ((), ()))
TN_DIMS = (((0,), (0,)), ((), ()))


def _cparams(n_axes):
    return pltpu.CompilerParams(dimension_semantics=("arbitrary",) * n_axes,
                                vmem_limit_bytes=VMEM_LIMIT_BYTES)


def _resident(shape):
    nd = len(shape)
    return pl.BlockSpec(shape, lambda *_: (0,) * nd, pipeline_mode=pl.Buffered(1))


def _t5_bucket_np(rel):
    nb = N_BUCKETS // 2
    max_exact = nb // 2
    ret = np.where(rel > 0, nb, 0)
    n = np.abs(rel)
    nf = np.maximum(n, max_exact).astype(np.float64)
    large = max_exact + (np.log(nf / max_exact) / math.log(MAX_DISTANCE / max_exact)
                         * (nb - max_exact)).astype(np.int32)
    large = np.minimum(large, nb - 1)
    return (ret + np.where(n < max_exact, n, large)).astype(np.int32)


def _bias_idx(qpos, kpos, n_real):
    qpos = np.asarray(qpos)[:, None]
    kpos = np.asarray(kpos)[None, :]
    vis = (np.floor_divide(kpos, CHUNK) <= np.floor_divide(qpos, CHUNK))
    vis = vis & (np.arange(kpos.shape[1])[None, :] < n_real)
    return np.where(vis, _t5_bucket_np(kpos - qpos), -1).astype(np.int32)


def _rotary_tables(pos):
    half = RET_DK // 2
    freq = 1.0 / (ROPE_BASE ** jnp.linspace(0.0, 1.0, half, dtype=F32))
    ang = jnp.asarray(pos, F32)[:, None] * freq[None, :]
    cos, sin = jnp.cos(ang), jnp.sin(ang)
    return jnp.concatenate([cos, cos], axis=1), jnp.concatenate([-sin, sin], axis=1)


def _retention_tables(t):
    gam = 1.0 - 2.0 ** (-5.0 - np.arange(RET_HEADS, dtype=np.float64))
    lg = np.log(gam)[:, None, None]
    n = np.arange(t)[:, None]
    m = np.arange(t)[None, :]
    cn, cm = n // CHUNK, m // CHUNK
    expo = np.where(cm == cn, np.abs(n - m), n - m).astype(np.float64)[None]
    dmat = np.where((cm <= cn)[None], np.exp(lg * expo), 0.0)
    qdec = np.exp(lg[:, :, 0] * (np.arange(t)[None, :] + 1.0))[:, :, None]
    kdec = np.exp(lg[:, :, 0] * (t - 1.0 - np.arange(t)[None, :]))[:, :, None]
    sdec = np.exp(lg[:, 0, 0] * t)
    return (jnp.asarray(dmat, F32), jnp.asarray(qdec, F32), jnp.asarray(kdec, F32),
            tuple(float(s) for s in sdec))


def _bias_kernel(rb_ref, *refs, n_tiles, shifted):
    idx_refs, out_refs = refs[:n_tiles], refs[n_tiles:]
    h = pl.program_id(0)
    far = rb_ref[N_BUCKETS // 2 - 1, h]
    for idx_ref, out_ref, sh in zip(idx_refs, out_refs, shifted):
        idx = idx_ref[...]
        acc = jnp.full(idx.shape, NEG_INF, F32)
        for b in range(N_BUCKETS):
            val = (rb_ref[b, h] - far if sh else rb_ref[b, h]) * LOG2E
            acc = jnp.where(idx == b, val, acc)
        out_ref[0] = acc


def _bias_tiles(rel_bias, idx_list, shifted):
    n = len(idx_list)
    in_specs = [pl.BlockSpec(memory_space=pltpu.SMEM)]
    in_specs += [pl.BlockSpec(ix.shape, lambda h: (0, 0)) for ix in idx_list]
    out_specs = [pl.BlockSpec((1,) + ix.shape, lambda h: (h, 0, 0)) for ix in idx_list]
    out_shape = [jax.ShapeDtypeStruct((DIFF_HEADS,) + ix.shape, F32) for ix in idx_list]
    return pl.pallas_call(
        functools.partial(_bias_kernel, n_tiles=n, shifted=tuple(shifted)),
        grid=(DIFF_HEADS,), in_specs=in_specs, out_specs=out_specs, out_shape=out_shape,
        compiler_params=_cparams(1), name="bias_tiles",
    )(rel_bias, *[jnp.asarray(ix) for ix in idx_list])


def _norm_proj(x_ref, g_ref, w_ref):
    x = x_ref[...]
    xn = x * lax.rsqrt(jnp.mean(x * x, axis=-1, keepdims=True) + NORM_EPS) * g_ref[...]
    xn = xn.astype(BF16)
    return lambda lo, n: jnp.dot(xn, w_ref[:, lo:lo + n], preferred_element_type=F32)


def _rotary(u, cs, sn):
    return u * cs + pltpu.roll(u, RET_DK // 2, 1) * sn


def _retention_head(qf, kf, v, gate, state, dm, qd, kd, sdec, with_output=True):
    y = None
    if with_output:
        s = lax.dot_general(qf.astype(BF16), kf.astype(BF16), NT_DIMS,
                            preferred_element_type=F32) * dm
        o = jnp.dot(s.astype(BF16), v, preferred_element_type=F32)
        o = o + jnp.dot((qf * qd).astype(BF16), state.astype(BF16), preferred_element_type=F32)
        y = o * lax.rsqrt(jnp.mean(o * o, axis=-1, keepdims=True) + RET_EPS)
        y = y * (gate * jax.nn.sigmoid(gate))
    kv = lax.dot_general((kf * kd).astype(BF16), v, TN_DIMS, preferred_element_type=F32)
    return y, sdec * state + kv


def _inproj_kernel(x_ref, g_ref, w_ref, cs_ref, sn_ref,
                   rq_ref, rk_ref, rv_ref, rg_ref, dq_ref, dkb_ref, dkf_ref,
                   dvb_ref, dvf_ref, gr_ref, gd_ref):
    proj = _norm_proj(x_ref, g_ref, w_ref)
    cs = cs_ref[...]
    sn = sn_ref[...]
    uq = proj(OFF_RQ, RET_QK_W)
    uk = proj(OFF_RK, RET_QK_W)
    for hh in range(RET_HEADS):
        sl = slice(hh * RET_DK, (hh + 1) * RET_DK)
        rq_ref[:, sl] = (_rotary(uq[:, sl], cs, sn) * (RET_DK ** -0.5)).astype(BF16)
        rk_ref[:, sl] = _rotary(uk[:, sl], cs, sn).astype(BF16)
    rv_ref[...] = proj(OFF_RV, RET_V_W).astype(BF16)
    rg_ref[...] = proj(OFF_RG, RET_V_W)
    dq_ref[...] = (proj(OFF_DQ, DIFF_W) * Q_SCALE).astype(BF16)
    dk = proj(OFF_DK, DIFF_W)
    dkf_ref[...] = dk
    dkb_ref[...] = dk.astype(BF16)
    dv = proj(OFF_DV, DIFF_W)
    dvf_ref[...] = dv
    dvb_ref[...] = dv.astype(BF16)
    gr_ref[...] = proj(OFF_GR, D_MODEL)
    gd_ref[...] = proj(OFF_GD, D_MODEL)


def _in_proj(x2d, g, w_bf, cs, sn, tm):
    rows = x2d.shape[0]
    n_pos = cs.shape[0] // tm
    row = lambda w: pl.BlockSpec((tm, w), lambda i: (i, 0))
    pos = pl.BlockSpec((tm, LANES), lambda i: (i % n_pos, 0))
    outs = [(RET_QK_W, BF16), (RET_QK_W, BF16), (RET_V_W, BF16), (RET_V_W, F32),
            (DIFF_W, BF16), (DIFF_W, BF16), (DIFF_W, F32), (DIFF_W, BF16), (DIFF_W, F32),
            (D_MODEL, F32), (D_MODEL, F32)]
    return pl.pallas_call(
        _inproj_kernel, grid=(rows // tm,),
        in_specs=[row(D_MODEL), _resident((1, D_MODEL)), _resident((D_MODEL, W_IN)), pos, pos],
        out_specs=[row(w) for w, _ in outs],
        out_shape=[jax.ShapeDtypeStruct((rows, w), dt) for w, dt in outs],
        compiler_params=_cparams(1), name="in_proj",
    )(x2d, g, w_bf, cs, sn)


def _prompt_proj_kernel(x_ref, g_ref, w_ref, cs_ref, sn_ref, s0_ref, dm_ref, qd_ref, kd_ref,
                        yret_ref, sfin_ref, dqt_ref, dkb_ref, dkf_ref, dvt_ref, dvf_ref,
                        gr_ref, gd_ref, s_scr, *, tiles_per_seq, sdec):
    i = pl.program_id(0)
    proj = _norm_proj(x_ref, g_ref, w_ref)
    cs = cs_ref[...]
    sn = sn_ref[...]

    @pl.when(i % tiles_per_seq == 0)
    def _():
        s_scr[...] = s0_ref[0]

    uq = proj(OFF_RQ, RET_QK_W)
    uk = proj(OFF_RK, RET_QK_W)
    rv = proj(OFF_RV, RET_V_W).astype(BF16)
    rg = proj(OFF_RG, RET_V_W)
    for hh in range(RET_HEADS):
        qs = slice(hh * RET_DK, (hh + 1) * RET_DK)
        vs = slice(hh * RET_DV, (hh + 1) * RET_DV)
        y, state = _retention_head(_rotary(uq[:, qs], cs, sn) * (RET_DK ** -0.5),
                                   _rotary(uk[:, qs], cs, sn), rv[:, vs], rg[:, vs],
                                   s_scr[hh], dm_ref[hh], qd_ref[hh], kd_ref[hh], sdec[hh])
        yret_ref[:, vs] = y.astype(BF16)
        s_scr[hh] = state

    @pl.when(i % tiles_per_seq == tiles_per_seq - 1)
    def _():
        sfin_ref[0] = s_scr[...]

    dqt_ref[0] = (proj(OFF_DQ, DIFF_W) * Q_SCALE).T.astype(BF16)
    dk = proj(OFF_DK, DIFF_W)
    dkf_ref[...] = dk
    dkb_ref[...] = dk.astype(BF16)
    dv = proj(OFF_DV, DIFF_W)
    dvf_ref[...] = dv
    dvt = dv.T.astype(BF16)
    ones = jnp.ones((ONES_ROWS, dvt.shape[1]), BF16)
    for hh in range(DIFF_HEADS):
        dvt_ref[0, hh * DV_AUG:hh * DV_AUG + DIFF_DV, :] = dvt[hh * DIFF_DV:(hh + 1) * DIFF_DV]
        dvt_ref[0, hh * DV_AUG + DIFF_DV:(hh + 1) * DV_AUG, :] = ones
    gr_ref[...] = proj(OFF_GR, D_MODEL)
    gd_ref[...] = proj(OFF_GD, D_MODEL)


def _prompt_proj(x2d, g, w_bf, cs, sn, s0, n_seq, tm):
    rows = x2d.shape[0]
    tps = rows // n_seq // tm
    dmat, qdec, kdec, sdec = _retention_tables(tm)
    row = lambda w: pl.BlockSpec((tm, w), lambda i: (i, 0))
    pos = pl.BlockSpec((tm, LANES), lambda i: (i % tps, 0))
    tposed = lambda w: pl.BlockSpec((1, w, tm), lambda i: (i, 0, 0))
    s_shape = (1, RET_HEADS, RET_DK, RET_DV)
    wv = DIFF_HEADS * DV_AUG
    return pl.pallas_call(
        functools.partial(_prompt_proj_kernel, tiles_per_seq=tps, sdec=sdec), grid=(rows // tm,),
        in_specs=[row(D_MODEL), _resident((1, D_MODEL)), _resident((D_MODEL, W_IN)), pos, pos,
                  _resident(s_shape), _resident(dmat.shape), _resident(qdec.shape),
                  _resident(kdec.shape)],
        out_specs=[row(RET_V_W), pl.BlockSpec(s_shape, lambda i: (i // tps, 0, 0, 0)),
                   tposed(DIFF_W), row(DIFF_W), row(DIFF_W), tposed(wv), row(DIFF_W),
                   row(D_MODEL), row(D_MODEL)],
        out_shape=[jax.ShapeDtypeStruct((rows, RET_V_W), BF16),
                   jax.ShapeDtypeStruct((n_seq,) + s_shape[1:], F32),
                   jax.ShapeDtypeStruct((rows // tm, DIFF_W, tm), BF16),
                   jax.ShapeDtypeStruct((rows, DIFF_W), BF16),
                   jax.ShapeDtypeStruct((rows, DIFF_W), F32),
                   jax.ShapeDtypeStruct((rows // tm, wv, tm), BF16),
                   jax.ShapeDtypeStruct((rows, DIFF_W), F32),
                   jax.ShapeDtypeStruct((rows, D_MODEL), F32),
                   jax.ShapeDtypeStruct((rows, D_MODEL), F32)],
        scratch_shapes=[pltpu.VMEM(s_shape[1:], F32)],
        compiler_params=_cparams(1), name="prompt_proj",
    )(x2d, g, w_bf, cs, sn, s0, dmat, qdec, kdec)


def _retention_kernel(rq_ref, rk_ref, rv_ref, rg_ref, s0_ref, dm_ref, qd_ref, kd_ref,
                      y_ref, sfin_ref, s_scr, *, sdec, with_output):
    t = pl.program_id(1)

    @pl.when(t == 0)
    def _():
        s_scr[...] = s0_ref[0]

    for hh in range(RET_HEADS):
        qs = slice(hh * RET_DK, (hh + 1) * RET_DK)
        vs = slice(hh * RET_DV, (hh + 1) * RET_DV)
        y, state = _retention_head(rq_ref[:, qs].astype(F32), rk_ref[:, qs].astype(F32),
                                   rv_ref[:, vs], rg_ref[:, vs], s_scr[hh], dm_ref[hh],
                                   qd_ref[hh], kd_ref[hh], sdec[hh], with_output)
        if with_output:
            y_ref[:, vs] = y.astype(BF16)
        else:
            y_ref[:, vs] = jnp.zeros((y_ref.shape[0], RET_DV), BF16)
        s_scr[hh] = state

    @pl.when(t == pl.num_programs(1) - 1)
    def _():
        sfin_ref[0] = s_scr[...]


def _retention(rq, rk, rv, rg, s0, n_seq, t, with_output=True):
    rows = rq.shape[0]
    nt = rows // n_seq // t
    dmat, qdec, kdec, sdec = _retention_tables(t)
    row = lambda w: pl.BlockSpec((t, w), lambda b, i: (b * nt + i, 0))
    s_shape = (1, RET_HEADS, RET_DK, RET_DV)
    if s0.shape[0] == 1:
        s0_spec = pl.BlockSpec(s_shape, lambda b, i: (0, 0, 0, 0))
    else:
        s0_spec = pl.BlockSpec(s_shape, lambda b, i: (b, 0, 0, 0))
    y, sfin = pl.pallas_call(
        functools.partial(_retention_kernel, sdec=sdec, with_output=with_output),
        grid=(n_seq, nt),
        in_specs=[row(RET_QK_W), row(RET_QK_W), row(RET_V_W), row(RET_V_W), s0_spec,
                  _resident(dmat.shape), _resident(qdec.shape), _resident(kdec.shape)],
        out_specs=[row(RET_V_W), pl.BlockSpec(s_shape, lambda b, i: (b, 0, 0, 0))],
        out_shape=[jax.ShapeDtypeStruct((rows, RET_V_W), BF16),
                   jax.ShapeDtypeStruct((n_seq,) + s_shape[1:], F32)],
        scratch_shapes=[pltpu.VMEM(s_shape[1:], F32)],
        compiler_params=_cparams(2), name="retention",
    )(rq, rk, rv, rg, s0, dmat, qdec, kdec)
    return y, sfin


def _lambda(lam_ref):
    lp = lam_ref[...]
    a = jnp.exp(jnp.sum(lp[0:1] * lp[1:2], axis=-1, keepdims=True))
    b = jnp.exp(jnp.sum(lp[2:3] * lp[3:4], axis=-1, keepdims=True))
    return a - b + LAM_INIT


def _stack_maps(q):
    qf = q.astype(F32)
    lane = lax.broadcasted_iota(jnp.int32, q.shape, 1)
    return jnp.concatenate([jnp.where(lane < DIFF_DH, qf, 0.0),
                            jnp.where(lane < DIFF_DH, 0.0, qf)], axis=0).astype(BF16)


def _mix_and_norm(o_all, lam, subg, n):
    o = o_all[:n] - lam * o_all[n:]
    y = o * lax.rsqrt(jnp.mean(o * o, axis=-1, keepdims=True) + DIFF_EPS) * subg
    return (y * (1.0 - LAM_INIT)).astype(BF16)


def _diff_attn_kernel(qt_ref, k_ref, vt_ref, mk_ref, mvt_ref, tdiag_ref, tsub_ref, tmeta_ref,
                      lam_ref, subg_ref, y_ref, qq_scr, m_scr, acc_scr, s_scr):
    i = pl.program_id(2)
    heads = range(HEADS_PER_STEP)
    hs = lambda h: slice(h * DIFF_DV, (h + 1) * DIFF_DV)
    vs = lambda h: slice(h * DV_AUG, (h + 1) * DV_AUG)

    for h in heads:
        qt = qt_ref[0, hs(h), :].astype(F32)
        row = lax.broadcasted_iota(jnp.int32, qt.shape, 0)
        qq_scr[h] = jnp.concatenate([jnp.where(row < DIFF_DH, qt, 0.0),
                                     jnp.where(row < DIFF_DH, 0.0, qt)], axis=1).astype(BF16)

    def scores(h, k_blk, tile):
        s = jnp.dot(k_blk, qq_scr[h], preferred_element_type=F32)
        return s if tile is None else s + tile

    for h in heads:
        m_scr[h] = jnp.full((1, 2 * TQ), NEG_INF, F32)
        acc_scr[h] = jnp.zeros((DV_AUG, 2 * TQ), F32)

    def meta_state():
        out = []
        for h in heads:
            s = scores(h, mk_ref[:, hs(h)], tmeta_ref[h, jnp.minimum(i, 1)])
            m0 = jnp.max(s, axis=0, keepdims=True)
            out.append((m0, jnp.dot(mvt_ref[vs(h), :], jnp.exp2(s - m0).astype(BF16),
                                    preferred_element_type=F32)))
        return out

    def qk(j):
        off = pl.multiple_of(j * TK, TK)
        return tuple(scores(h, k_ref[pl.ds(off, TK), hs(h)], None) for h in heads)

    def consume(j, s):
        m_prev = [m_scr[h] for h in heads]
        m_new = [jnp.maximum(m_prev[h], jnp.max(s[h], axis=0, keepdims=True)) for h in heads]
        p = [jnp.exp2(s[h] - m_new[h]).astype(BF16) for h in heads]
        alpha = [jnp.exp2(m_prev[h] - m_new[h]) for h in heads]
        for h in heads:
            m_scr[h] = m_new[h]
        pv = [jnp.dot(vt_ref[j, vs(h), :], p[h], preferred_element_type=F32) for h in heads]
        for h in heads:
            acc_scr[h] = alpha[h] * acc_scr[h] + pv[h]

    def qk_to_scratch(j, buf):
        for h, s in enumerate(qk(j)):
            s_scr[buf, h] = s

    def far_pair(t, carry):
        j = 2 * t
        qk_to_scratch(j + 1, 1)
        consume(j, [s_scr[0, h] for h in heads])
        qk_to_scratch(j + 2, 0)
        consume(j + 1, [s_scr[1, h] for h in heads])
        return carry

    j_sub = jnp.maximum(i - 1, 0)
    n_pairs = j_sub // 2
    qk_to_scratch(0, 0)
    lax.fori_loop(0, n_pairs, far_pair, 0)

    def finish(meta):
        lam = _lambda(lam_ref)
        for h in heads:
            m_meta, acc_meta = meta[h]
            m_main = m_scr[h]
            m = jnp.maximum(m_main, m_meta)
            acc = jnp.exp2(m_main - m) * acc_scr[h] + jnp.exp2(m_meta - m) * acc_meta
            o_all = acc[:DIFF_DV] / acc[DIFF_DV:DIFF_DV + 1]
            o = o_all[:, :TQ] - lam * o_all[:, TQ:]
            y = o * lax.rsqrt(jnp.mean(o * o, axis=0, keepdims=True) + DIFF_EPS) * subg_ref[...]
            y_ref[:, hs(h)] = (y * (1.0 - LAM_INIT)).T.astype(BF16)

    pen = jnp.where(i == 0, 3.0 * NEG_INF, 0.0)

    def last_blocks(sub_buf):
        meta = meta_state()
        s_diag = qk(i)
        consume(j_sub, [s_scr[sub_buf, h] + (tsub_ref[h] + pen) for h in heads])
        consume(i, [s_diag[h] + tdiag_ref[h] for h in heads])
        finish(meta)

    @pl.when(j_sub % 2 == 1)
    def _():
        qk_to_scratch(j_sub, 1)
        consume(j_sub - 1, [s_scr[0, h] for h in heads])
        last_blocks(1)

    @pl.when(j_sub % 2 == 0)
    def _():
        last_blocks(0)


def _diff_attn(dqt, dk, dvt, mk, mvt, tdiag, tsub, tmeta, lam_p, subg_b, n_seq):
    rows = dk.shape[0]
    seq = rows // n_seq
    nq = seq // TQ
    hb = HEADS_PER_STEP
    wid = hb * DIFF_DV
    wid_v = hb * DV_AUG
    tile3 = lambda b, g, i: (g, 0, 0)
    slow = lambda shape, imap: pl.BlockSpec(shape, imap, pipeline_mode=pl.Buffered(1))
    return pl.pallas_call(
        _diff_attn_kernel, grid=(n_seq, DIFF_HEADS // hb, nq),
        in_specs=[pl.BlockSpec((1, wid, TQ), lambda b, g, i: (b * nq + i, g, 0)),
                  slow((seq, wid), lambda b, g, i: (b, g)),
                  slow((nq, wid_v, TK), lambda b, g, i: (b, g, 0)),
                  slow((N_META, wid), lambda b, g, i: (0, g)),
                  slow((wid_v, N_META), lambda b, g, i: (g, 0)),
                  slow((hb, TK, 2 * TQ), tile3),
                  slow((hb, TK, 2 * TQ), tile3),
                  slow((hb, 2, N_META, 2 * TQ), lambda b, g, i: (g, 0, 0, 0)),
                  slow((4, DIFF_DH), lambda b, g, i: (0, 0)),
                  slow((DIFF_DV, TQ), lambda b, g, i: (0, 0))],
        out_specs=pl.BlockSpec((TQ, wid), lambda b, g, i: (b * nq + i, g)),
        out_shape=jax.ShapeDtypeStruct((rows, DIFF_W), BF16),
        scratch_shapes=[pltpu.VMEM((hb, DIFF_DV, 2 * TQ), BF16), pltpu.VMEM((hb, 1, 2 * TQ), F32),
                        pltpu.VMEM((hb, DV_AUG, 2 * TQ), F32),
                        pltpu.VMEM((2, hb, TK, 2 * TQ), F32)],
        compiler_params=_cparams(3), name="diff_attn",
    )(dqt, dk, dvt, mk, mvt, tdiag, tsub, tmeta, lam_p, subg_b)


def _sample_attn_kernel(q_ref, ck_ref, cv_ref, nk_ref, nv_ref, tile_ref, lam_ref, subg_ref,
                        y_ref, k_scr, v_scr, *, past, n_new):
    pad = k_scr.shape[1] - past - n_new
    lam = _lambda(lam_ref)
    for h in range(DIFF_HEADS):
        hs = slice(h * DIFF_DV, (h + 1) * DIFF_DV)
        k_scr[h, 0:past] = ck_ref[0, pl.ds(h, past, stride=DIFF_HEADS), :].astype(BF16)
        v_scr[h, 0:past] = cv_ref[0, pl.ds(h, past, stride=DIFF_HEADS), :].astype(BF16)
        k_scr[h, past:past + n_new] = nk_ref[:, hs]
        v_scr[h, past:past + n_new] = nv_ref[:, hs]
        k_scr[h, past + n_new:] = jnp.zeros((pad, LANES), BF16)
        v_scr[h, past + n_new:] = jnp.zeros((pad, LANES), BF16)
        qq = _stack_maps(q_ref[:, hs])
        s = lax.dot_general(qq, k_scr[h], NT_DIMS, preferred_element_type=F32)
        s = (s.reshape(2, n_new, s.shape[-1]) + tile_ref[h][None]).reshape(2 * n_new, s.shape[-1])
        m = jnp.max(s, axis=1, keepdims=True)
        p = jnp.exp2(s - m)
        l = jnp.sum(p, axis=1, keepdims=True)
        o_all = jnp.dot(p.astype(BF16), v_scr[h], preferred_element_type=F32) / l
        y_ref[:, hs] = _mix_and_norm(o_all, lam, subg_ref[...], n_new)


def _sample_attn(dq, ck, cv, nk, nv, tile, lam_p, subg, n_seq, past, n_new):
    kpad = tile.shape[-1]
    row = pl.BlockSpec((n_new, DIFF_W), lambda b: (b, 0))
    cache = pl.BlockSpec((1, past * DIFF_HEADS, DIFF_DV), lambda b: (b, 0, 0))
    return pl.pallas_call(
        functools.partial(_sample_attn_kernel, past=past, n_new=n_new),
        grid=(n_seq,),
        in_specs=[row, cache, cache, row, row, _resident(tile.shape),
                  _resident((4, DIFF_DH)), _resident((1, DIFF_DV))],
        out_specs=row,
        out_shape=jax.ShapeDtypeStruct((n_seq * n_new, DIFF_W), BF16),
        scratch_shapes=[pltpu.VMEM((DIFF_HEADS, kpad, LANES), BF16),
                        pltpu.VMEM((DIFF_HEADS, kpad, LANES), BF16)],
        compiler_params=_cparams(1), name="sample_attn",
    )(dq, ck, cv, nk, nv, tile, lam_p, subg)


def _tail_kernel(h_ref, yr_ref, yd_ref, gr_ref, gd_ref, wrb_ref, wdb_ref, wo_ref, n2_ref,
                 wup_ref, wdn_ref, nf_ref, out_ref):
    a = jnp.dot(yr_ref[...], wrb_ref[...], preferred_element_type=F32)
    b = jnp.dot(yd_ref[...], wdb_ref[...], preferred_element_type=F32)
    merged = jax.nn.sigmoid(gr_ref[...]) * a + jax.nn.sigmoid(gd_ref[...]) * b
    h = h_ref[...] + jnp.dot(merged.astype(BF16), wo_ref[...], preferred_element_type=F32)
    xn = h * lax.rsqrt(jnp.mean(h * h, axis=-1, keepdims=True) + NORM_EPS) * n2_ref[...]
    xn = xn.astype(BF16)
    acc = jnp.zeros_like(h)
    for lo in range(0, D_FF, FF_CHUNK):
        n = min(FF_CHUNK, D_FF - lo)
        gate = jnp.dot(xn, wup_ref[:, lo:lo + n], preferred_element_type=F32)
        up = jnp.dot(xn, wup_ref[:, D_FF + lo:D_FF + lo + n], preferred_element_type=F32)
        act = (gate * jax.nn.sigmoid(gate) * up).astype(BF16)
        acc = acc + jnp.dot(act, wdn_ref[lo:lo + n, :], preferred_element_type=F32)
    h = h + acc
    out_ref[...] = h * lax.rsqrt(jnp.mean(h * h, axis=-1, keepdims=True) + NORM_EPS) * nf_ref[...]


def _tail(h2d, y_ret, y_diff, gr, gd, wrb, wdb, wo, n2, wup, wdn, nf, tm):
    rows = h2d.shape[0]
    row = pl.BlockSpec((tm, D_MODEL), lambda i: (i, 0))
    return pl.pallas_call(
        _tail_kernel, grid=(rows // tm,),
        in_specs=[row, row, row, row, row,
                  _resident(wrb.shape), _resident(wdb.shape), _resident(wo.shape),
                  _resident(n2.shape), _resident(wup.shape), _resident(wdn.shape),
                  _resident(nf.shape)],
        out_specs=row,
        out_shape=jax.ShapeDtypeStruct((rows, D_MODEL), F32),
        compiler_params=_cparams(1), name="tail",
    )(h2d, y_ret, y_diff, gr, gd, wrb, wdb, wo, n2, wup, wdn, nf)


def kernel(x_prompt, x_sample, cache_k, cache_v, state_ret, meta_tokens, rel_bias, norm1_g, w_in,
           lambda_q1, lambda_k1, lambda_q2, lambda_k2, diff_subln_g, w_ret_branch, w_diff_branch,
           w_o, norm2_g, w_ffn_up, w_ffn_down, normf_g):
    assert w_in.shape[0] == 1, "single-layer step only"
    bsz, seq, _ = x_prompt.shape
    dbsz, dseq, _ = x_sample.shape
    past = cache_k.shape[2]
    assert TQ == TK == TM_PROJ and TQ % CHUNK == 0 and DIFF_HEADS % HEADS_PER_STEP == 0
    assert seq % TQ == 0
    assert dseq <= CHUNK and past % CHUNK == 0 and meta_tokens.shape[0] == N_META

    w_in_bf = w_in[0].astype(BF16)
    g1 = norm1_g[0][None, :]
    lam_p = jnp.stack([lambda_q1[0], lambda_k1[0], lambda_q2[0], lambda_k2[0]])
    subg = diff_subln_g[0][None, :]
    wts = (w_ret_branch[0].astype(BF16), w_diff_branch[0].astype(BF16), w_o[0].astype(BF16),
           norm2_g[0][None, :], w_ffn_up[0].astype(BF16), w_ffn_down[0].astype(BF16),
           normf_g[None, :])

    kpad = ((past + dseq + LANES - 1) // LANES) * LANES
    both_maps = lambda ix: np.concatenate([ix.T, ix.T], axis=1)
    idx_diag = both_maps(_bias_idx(np.arange(TQ), np.arange(TK), TK))
    idx_sub = both_maps(_bias_idx(np.arange(TQ) + TK, np.arange(TK), TK))
    idx_meta = np.concatenate(
        [both_maps(_bias_idx(np.arange(TQ) + b * TQ, np.arange(N_META) - N_META, N_META))
         for b in range(2)], axis=0)
    idx_samp = _bias_idx(past + np.arange(dseq), np.arange(kpad), past + dseq)
    t_diag, t_sub, t_meta, t_samp = _bias_tiles(rel_bias, [idx_diag, idx_sub, idx_meta, idx_samp],
                                                [True, True, True, False])
    t_meta = t_meta.reshape(DIFF_HEADS, 2, N_META, 2 * TQ)

    cs_m, sn_m = _rotary_tables(np.arange(-N_META, 0))
    m_out = _in_proj(meta_tokens, g1, w_in_bf, cs_m, sn_m, N_META)
    zero_state = jnp.zeros((1, RET_HEADS, RET_DK, RET_DV), F32)
    _, s_meta = _retention(m_out[0], m_out[1], m_out[2], m_out[3], zero_state, 1, N_META,
                           with_output=False)
    mk = m_out[5]
    mv_aug = jnp.concatenate([m_out[7].reshape(N_META, DIFF_HEADS, DIFF_DV),
                              jnp.ones((N_META, DIFF_HEADS, ONES_ROWS), BF16)], axis=2)
    mvt = mv_aug.reshape(N_META, DIFF_HEADS * DV_AUG).T
    subg_b = jnp.broadcast_to(diff_subln_g[0][:, None], (DIFF_DV, TQ))

    cs_p, sn_p = _rotary_tables(np.arange(seq))
    x2d = x_prompt.reshape(bsz * seq, D_MODEL)
    (y_ret, s_fin, dqt, dkb, dkf, dvt, dvf, gr, gd) = _prompt_proj(x2d, g1, w_in_bf, cs_p, sn_p,
                                                                   s_meta, bsz, TM_PROJ)
    y_diff = _diff_attn(dqt, dkb, dvt, mk, mvt, t_diag, t_sub, t_meta, lam_p, subg_b, bsz)
    y_prompt = _tail(x2d, y_ret, y_diff, gr, gd, *wts, TM_TAIL).reshape(bsz, seq, D_MODEL)
    k_rows = jnp.concatenate([jnp.broadcast_to(m_out[6][None], (bsz, N_META, DIFF_W)),
                              dkf.reshape(bsz, seq, DIFF_W)], axis=1)
    v_rows = jnp.concatenate([jnp.broadcast_to(m_out[8][None], (bsz, N_META, DIFF_W)),
                              dvf.reshape(bsz, seq, DIFF_W)], axis=1)

    cs_s, sn_s = _rotary_tables(np.tile(past + np.arange(dseq), dbsz))
    xs2d = x_sample.reshape(dbsz * dseq, D_MODEL)
    (rq, rk, rv, rg, dq, dkb, dkf_s, dvb, dvf_s, gr, gd) = _in_proj(xs2d, g1, w_in_bf, cs_s, sn_s,
                                                                     dbsz * dseq)
    y_ret, s_samp = _retention(rq, rk, rv, rg, state_ret[0], dbsz, dseq)
    ck = cache_k[0].reshape(dbsz, past * DIFF_HEADS, DIFF_DV)
    cv = cache_v[0].reshape(dbsz, past * DIFF_HEADS, DIFF_DV)
    y_diff = _sample_attn(dq, ck, cv, dkb, dvb, t_samp, lam_p, subg, dbsz, past, dseq)
    y_sample = _tail(xs2d, y_ret, y_diff, gr, gd, *wts, dbsz * dseq).reshape(dbsz, dseq, D_MODEL)

    heads = (DIFF_HEADS, DIFF_DV)
    return (y_prompt, y_sample,
            k_rows.reshape((1, bsz, N_META + seq) + heads),
            v_rows.reshape((1, bsz, N_META + seq) + heads),
            s_fin[None],
            dkf_s.reshape((1, dbsz, dseq) + heads),
            dvf_s.reshape((1, dbsz, dseq) + heads),
            s_samp[None])
```

```python
import functools
import math

import numpy as np
import jax
import jax.numpy as jnp
from jax import lax
from jax.experimental import pallas as pl
from jax.experimental.pallas import tpu as pltpu

F32 = jnp.float32
BF16 = jnp.bfloat16

D_MODEL = 1024
CHUNK = 64
N_META = 16
RET_HEADS = 4
RET_DK = D_MODEL // 8
RET_DV = 2 * RET_DK
RET_QK_W = RET_HEADS * RET_DK
RET_V_W = RET_HEADS * RET_DV
ROPE_BASE = 10000.0
RET_EPS = 1e-6
DIFF_HEADS = 8
DIFF_DH = D_MODEL // 16
DIFF_DV = 2 * DIFF_DH
DIFF_W = DIFF_HEADS * DIFF_DV
DIFF_EPS = 1e-5
N_BUCKETS = 32
MAX_DISTANCE = 128
D_FF = ((8 * D_MODEL + 3 * 256 - 1) // (3 * 256)) * 256
NORM_EPS = 1e-6
NEG_INF = -1e30
LAM_INIT = 0.8 - 0.6 * math.exp(-0.3 * 0)
LOG2E = math.log2(math.e)
Q_SCALE = DIFF_DH ** -0.5 * LOG2E
ONES_ROWS = 16
DV_AUG = DIFF_DV + ONES_ROWS

OFF_RQ = 0
OFF_RK = OFF_RQ + RET_QK_W
OFF_RV = OFF_RK + RET_QK_W
OFF_RG = OFF_RV + RET_V_W
OFF_DQ = OFF_RG + RET_V_W
OFF_DK = OFF_DQ + DIFF_W
OFF_DV = OFF_DK + DIFF_W
OFF_GR = OFF_DV + DIFF_W
OFF_GD = OFF_GR + D_MODEL
W_IN = OFF_GD + D_MODEL

LANES = 128
VMEM_LIMIT_BYTES = 56 * 1024 * 1024
TM_PROJ = 256
TM_TAIL = 256
TQ = 256
TK = 256
HEADS_PER_STEP = 4
FF_CHUNK = 512

NT_DIMS = (((1,), (1,)), ((), ()))
TN_DIMS = (((0,), (0,)), ((), ()))


def _cparams(n_axes):
    return pltpu.CompilerParams(dimension_semantics=("arbitrary",) * n_axes,
                                vmem_limit_bytes=VMEM_LIMIT_BYTES)


def _resident(shape):
    nd = len(shape)
    return pl.BlockSpec(shape, lambda *_: (0,) * nd, pipeline_mode=pl.Buffered(1))


def _t5_bucket_np(rel):
    nb = N_BUCKETS // 2
    max_exact = nb // 2
    ret = np.where(rel > 0, nb, 0)
    n = np.abs(rel)
    nf = np.maximum(n, max_exact).astype(np.float64)
    large = max_exact + (np.log(nf / max_exact) / math.log(MAX_DISTANCE / max_exact)
                         * (nb - max_exact)).astype(np.int32)
    large = np.minimum(large, nb - 1)
    return (ret + np.where(n < max_exact, n, large)).astype(np.int32)


def _bias_idx(qpos, kpos, n_real):
    qpos = np.asarray(qpos)[:, None]
    kpos = np.asarray(kpos)[None, :]
    vis = (np.floor_divide(kpos, CHUNK) <= np.floor_divide(qpos, CHUNK))
    vis = vis & (np.arange(kpos.shape[1])[None, :] < n_real)
    return np.where(vis, _t5_bucket_np(kpos - qpos), -1).astype(np.int32)


def _rotary_tables(pos):
    half = RET_DK // 2
    freq = 1.0 / (ROPE_BASE ** jnp.linspace(0.0, 1.0, half, dtype=F32))
    ang = jnp.asarray(pos, F32)[:, None] * freq[None, :]
    cos, sin = jnp.cos(ang), jnp.sin(ang)
    return jnp.concatenate([cos, cos], axis=1), jnp.concatenate([-sin, sin], axis=1)


def _retention_tables(t):
    gam = 1.0 - 2.0 ** (-5.0 - np.arange(RET_HEADS, dtype=np.float64))
    lg = np.log(gam)[:, None, None]
    n = np.arange(t)[:, None]
    m = np.arange(t)[None, :]
    cn, cm = n // CHUNK, m // CHUNK
    expo = np.where(cm == cn, np.abs(n - m), n - m).astype(np.float64)[None]
    dmat = np.where((cm <= cn)[None], np.exp(lg * expo), 0.0)
    qdec = np.exp(lg[:, :, 0] * (np.arange(t)[None, :] + 1.0))[:, :, None]
    kdec = np.exp(lg[:, :, 0] * (t - 1.0 - np.arange(t)[None, :]))[:, :, None]
    sdec = np.exp(lg[:, 0, 0] * t)
    return (jnp.asarray(dmat, F32), jnp.asarray(qdec, F32), jnp.asarray(kdec, F32),
            tuple(float(s) for s in sdec))


def _bias_kernel(rb_ref, *refs, n_tiles, shifted):
    idx_refs, out_refs = refs[:n_tiles], refs[n_tiles:]
    h = pl.program_id(0)
    far = rb_ref[N_BUCKETS // 2 - 1, h]
    for idx_ref, out_ref, sh in zip(idx_refs, out_refs, shifted):
        idx = idx_ref[...]
        acc = jnp.full(idx.shape, NEG_INF, F32)
        for b in range(N_BUCKETS):
            val = (rb_ref[b, h] - far if sh else rb_ref[b, h]) * LOG2E
            acc = jnp.where(idx == b, val, acc)
        out_ref[0] = acc


def _bias_tiles(rel_bias, idx_list, shifted):
    n = len(idx_list)
    in_specs = [pl.BlockSpec(memory_space=pltpu.SMEM)]
    in_specs += [pl.BlockSpec(ix.shape, lambda h: (0, 0)) for ix in idx_list]
    out_specs = [pl.BlockSpec((1,) + ix.shape, lambda h: (h, 0, 0)) for ix in idx_list]
    out_shape = [jax.ShapeDtypeStruct((DIFF_HEADS,) + ix.shape, F32) for ix in idx_list]
    return pl.pallas_call(
        functools.partial(_bias_kernel, n_tiles=n, shifted=tuple(shifted)),
        grid=(DIFF_HEADS,), in_specs=in_specs, out_specs=out_specs, out_shape=out_shape,
        compiler_params=_cparams(1), name="bias_tiles",
    )(rel_bias, *[jnp.asarray(ix) for ix in idx_list])


def _norm_proj(x_ref, g_ref, w_ref):
    x = x_ref[...]
    xn = x * lax.rsqrt(jnp.mean(x * x, axis=-1, keepdims=True) + NORM_EPS) * g_ref[...]
    xn = xn.astype(BF16)
    return lambda lo, n: jnp.dot(xn, w_ref[:, lo:lo + n], preferred_element_type=F32)


def _rotary(u, cs, sn):
    return u * cs + pltpu.roll(u, RET_DK // 2, 1) * sn


def _retention_head(qf, kf, v, gate, state, dm, qd, kd, sdec, with_output=True):
    y = None
    if with_output:
        s = lax.dot_general(qf.astype(BF16), kf.astype(BF16), NT_DIMS,
                            preferred_element_type=F32) * dm
        o = jnp.dot(s.astype(BF16), v, preferred_element_type=F32)
        o = o + jnp.dot((qf * qd).astype(BF16), state.astype(BF16), preferred_element_type=F32)
        y = o * lax.rsqrt(jnp.mean(o * o, axis=-1, keepdims=True) + RET_EPS)
        y = y * (gate * jax.nn.sigmoid(gate))
    kv = lax.dot_general((kf * kd).astype(BF16), v, TN_DIMS, preferred_element_type=F32)
    return y, sdec * state + kv


def _inproj_kernel(x_ref, g_ref, w_ref, cs_ref, sn_ref,
                   rq_ref, rk_ref, rv_ref, rg_ref, dq_ref, dkb_ref, dkf_ref,
                   dvb_ref, dvf_ref, gr_ref, gd_ref):
    proj = _norm_proj(x_ref, g_ref, w_ref)
    cs = cs_ref[...]
    sn = sn_ref[...]
    uq = proj(OFF_RQ, RET_QK_W)
    uk = proj(OFF_RK, RET_QK_W)
    for hh in range(RET_HEADS):
        sl = slice(hh * RET_DK, (hh + 1) * RET_DK)
        rq_ref[:, sl] = (_rotary(uq[:, sl], cs, sn) * (RET_DK ** -0.5)).astype(BF16)
        rk_ref[:, sl] = _rotary(uk[:, sl], cs, sn).astype(BF16)
    rv_ref[...] = proj(OFF_RV, RET_V_W).astype(BF16)
    rg_ref[...] = proj(OFF_RG, RET_V_W)
    dq_ref[...] = (proj(OFF_DQ, DIFF_W) * Q_SCALE).astype(BF16)
    dk = proj(OFF_DK, DIFF_W)
    dkf_ref[...] = dk
    dkb_ref[...] = dk.astype(BF16)
    dv = proj(OFF_DV, DIFF_W)
    dvf_ref[...] = dv
    dvb_ref[...] = dv.astype(BF16)
    gr_ref[...] = proj(OFF_GR, D_MODEL)
    gd_ref[...] = proj(OFF_GD, D_MODEL)


def _in_proj(x2d, g, w_bf, cs, sn, tm):
    rows = x2d.shape[0]
    n_pos = cs.shape[0] // tm
    row = lambda w: pl.BlockSpec((tm, w), lambda i: (i, 0))
    pos = pl.BlockSpec((tm, LANES), lambda i: (i % n_pos, 0))
    outs = [(RET_QK_W, BF16), (RET_QK_W, BF16), (RET_V_W, BF16), (RET_V_W, F32),
            (DIFF_W, BF16), (DIFF_W, BF16), (DIFF_W, F32), (DIFF_W, BF16), (DIFF_W, F32),
            (D_MODEL, F32), (D_MODEL, F32)]
    return pl.pallas_call(
        _inproj_kernel, grid=(rows // tm,),
        in_specs=[row(D_MODEL), _resident((1, D_MODEL)), _resident((D_MODEL, W_IN)), pos, pos],
        out_specs=[row(w) for w, _ in outs],
        out_shape=[jax.ShapeDtypeStruct((rows, w), dt) for w, dt in outs],
        compiler_params=_cparams(1), name="in_proj",
    )(x2d, g, w_bf, cs, sn)


def _prompt_proj_kernel(x_ref, g_ref, w_ref, cs_ref, sn_ref, s0_ref, dm_ref, qd_ref, kd_ref,
                        mk_ref, mv_ref,
                        yret_ref, sfin_ref, dqt_ref, dkb_ref, dvt_ref, gr_ref, gd_ref,
                        krows_ref, vrows_ref,
                        s_scr, stage_scr, row_sem, meta_sem, *, tiles_per_seq, n_steps, sdec):
    i = pl.program_id(0)
    tm = x_ref.shape[0]
    proj = _norm_proj(x_ref, g_ref, w_ref)
    cs = cs_ref[...]
    sn = sn_ref[...]
    slot = i % 2
    stream = i // tiles_per_seq
    rows_out = (krows_ref, vrows_ref)

    def rows_copy(which, step):
        first = (N_META + (step % tiles_per_seq) * tm) * DIFF_HEADS
        return pltpu.make_async_copy(
            stage_scr.at[step % 2, which],
            rows_out[which].at[step // tiles_per_seq, pl.ds(first, tm * DIFF_HEADS), :],
            row_sem.at[step % 2, which])

    def meta_copy(which):
        return pltpu.make_async_copy(
            (mk_ref, mv_ref)[which],
            rows_out[which].at[stream, pl.ds(0, N_META * DIFF_HEADS), :], meta_sem.at[which])

    @pl.when(i % tiles_per_seq == 0)
    def _():
        s_scr[...] = s0_ref[0]
        for which in range(2):
            meta_copy(which).start()

    @pl.when(i >= 2)
    def _():
        for which in range(2):
            rows_copy(which, i - 2).wait()

    uq = proj(OFF_RQ, RET_QK_W)
    uk = proj(OFF_RK, RET_QK_W)
    rv = proj(OFF_RV, RET_V_W).astype(BF16)
    rg = proj(OFF_RG, RET_V_W)
    for hh in range(RET_HEADS):
        qs = slice(hh * RET_DK, (hh + 1) * RET_DK)
        vs = slice(hh * RET_DV, (hh + 1) * RET_DV)
        y, state = _retention_head(_rotary(uq[:, qs], cs, sn) * (RET_DK ** -0.5),
                                   _rotary(uk[:, qs], cs, sn), rv[:, vs], rg[:, vs],
                                   s_scr[hh], dm_ref[hh], qd_ref[hh], kd_ref[hh], sdec[hh])
        yret_ref[:, vs] = y.astype(BF16)
        s_scr[hh] = state

    @pl.when(i % tiles_per_seq == tiles_per_seq - 1)
    def _():
        sfin_ref[0] = s_scr[...]

    dqt_ref[0] = (proj(OFF_DQ, DIFF_W) * Q_SCALE).T.astype(BF16)
    dk = proj(OFF_DK, DIFF_W)
    dkb_ref[...] = dk.astype(BF16)
    dv = proj(OFF_DV, DIFF_W)
    dvt = dv.T.astype(BF16)
    ones = jnp.ones((ONES_ROWS, dvt.shape[1]), BF16)
    for hh in range(DIFF_HEADS):
        hs = slice(hh * DIFF_DV, (hh + 1) * DIFF_DV)
        dvt_ref[0, hh * DV_AUG:hh * DV_AUG + DIFF_DV, :] = dvt[hs]
        dvt_ref[0, hh * DV_AUG + DIFF_DV:(hh + 1) * DV_AUG, :] = ones
        head_rows = pl.ds(hh, tm, stride=DIFF_HEADS)
        stage_scr[slot, 0, head_rows, :] = dk[:, hs]
        stage_scr[slot, 1, head_rows, :] = dv[:, hs]
    for which in range(2):
        rows_copy(which, i).start()
    gr_ref[...] = proj(OFF_GR, D_MODEL)
    gd_ref[...] = proj(OFF_GD, D_MODEL)

    @pl.when(i % tiles_per_seq == 0)
    def _():
        for which in range(2):
            meta_copy(which).wait()

    @pl.when(i == n_steps - 1)
    def _():
        for which in range(2):
            if n_steps >= 2:
                rows_copy(which, i - 1).wait()
            rows_copy(which, i).wait()


def _prompt_proj(x2d, g, w_bf, cs, sn, s0, mk_rows, mv_rows, n_seq, tm):
    rows = x2d.shape[0]
    seq = rows // n_seq
    tps = seq // tm
    dmat, qdec, kdec, sdec = _retention_tables(tm)
    row = lambda w: pl.BlockSpec((tm, w), lambda i: (i, 0))
    pos = pl.BlockSpec((tm, LANES), lambda i: (i % tps, 0))
    tposed = lambda w: pl.BlockSpec((1, w, tm), lambda i: (i, 0, 0))
    hbm = pl.BlockSpec(memory_space=pl.ANY)
    s_shape = (1, RET_HEADS, RET_DK, RET_DV)
    wv = DIFF_HEADS * DV_AUG
    rows_shape = jax.ShapeDtypeStruct((n_seq, (N_META + seq) * DIFF_HEADS, DIFF_DV), F32)
    return pl.pallas_call(
        functools.partial(_prompt_proj_kernel, tiles_per_seq=tps, n_steps=rows // tm, sdec=sdec),
        grid=(rows // tm,),
        in_specs=[row(D_MODEL), _resident((1, D_MODEL)), _resident((D_MODEL, W_IN)), pos, pos,
                  _resident(s_shape), _resident(dmat.shape), _resident(qdec.shape),
                  _resident(kdec.shape), _resident(mk_rows.shape), _resident(mv_rows.shape)],
        out_specs=[row(RET_V_W), pl.BlockSpec(s_shape, lambda i: (i // tps, 0, 0, 0)),
                   tposed(DIFF_W), row(DIFF_W), tposed(wv), row(D_MODEL), row(D_MODEL),
                   hbm, hbm],
        out_shape=[jax.ShapeDtypeStruct((rows, RET_V_W), BF16),
                   jax.ShapeDtypeStruct((n_seq,) + s_shape[1:], F32),
                   jax.ShapeDtypeStruct((rows // tm, DIFF_W, tm), BF16),
                   jax.ShapeDtypeStruct((rows, DIFF_W), BF16),
                   jax.ShapeDtypeStruct((rows // tm, wv, tm), BF16),
                   jax.ShapeDtypeStruct((rows, D_MODEL), F32),
                   jax.ShapeDtypeStruct((rows, D_MODEL), F32),
                   rows_shape, rows_shape],
        scratch_shapes=[pltpu.VMEM(s_shape[1:], F32),
                        pltpu.VMEM((2, 2, tm * DIFF_HEADS, DIFF_DV), F32),
                        pltpu.SemaphoreType.DMA((2, 2)), pltpu.SemaphoreType.DMA((2,))],
        compiler_params=_cparams(1), name="prompt_proj",
    )(x2d, g, w_bf, cs, sn, s0, dmat, qdec, kdec, mk_rows, mv_rows)


def _retention_kernel(rq_ref, rk_ref, rv_ref, rg_ref, s0_ref, dm_ref, qd_ref, kd_ref,
                      y_ref, sfin_ref, s_scr, *, sdec, with_output):
    t = pl.program_id(1)

    @pl.when(t == 0)
    def _():
        s_scr[...] = s0_ref[0]

    for hh in range(RET_HEADS):
        qs = slice(hh * RET_DK, (hh + 1) * RET_DK)
        vs = slice(hh * RET_DV, (hh + 1) * RET_DV)
        y, state = _retention_head(rq_ref[:, qs].astype(F32), rk_ref[:, qs].astype(F32),
                                   rv_ref[:, vs], rg_ref[:, vs], s_scr[hh], dm_ref[hh],
                                   qd_ref[hh], kd_ref[hh], sdec[hh], with_output)
        if with_output:
            y_ref[:, vs] = y.astype(BF16)
        else:
            y_ref[:, vs] = jnp.zeros((y_ref.shape[0], RET_DV), BF16)
        s_scr[hh] = state

    @pl.when(t == pl.num_programs(1) - 1)
    def _():
        sfin_ref[0] = s_scr[...]


def _retention(rq, rk, rv, rg, s0, n_seq, t, with_output=True):
    rows = rq.shape[0]
    nt = rows // n_seq // t
    dmat, qdec, kdec, sdec = _retention_tables(t)
    row = lambda w: pl.BlockSpec((t, w), lambda b, i: (b * nt + i, 0))
    s_shape = (1, RET_HEADS, RET_DK, RET_DV)
    if s0.shape[0] == 1:
        s0_spec = pl.BlockSpec(s_shape, lambda b, i: (0, 0, 0, 0))
    else:
        s0_spec = pl.BlockSpec(s_shape, lambda b, i: (b, 0, 0, 0))
    y, sfin = pl.pallas_call(
        functools.partial(_retention_kernel, sdec=sdec, with_output=with_output),
        grid=(n_seq, nt),
        in_specs=[row(RET_QK_W), row(RET_QK_W), row(RET_V_W), row(RET_V_W), s0_spec,
                  _resident(dmat.shape), _resident(qdec.shape), _resident(kdec.shape)],
        out_specs=[row(RET_V_W), pl.BlockSpec(s_shape, lambda b, i: (b, 0, 0, 0))],
        out_shape=[jax.ShapeDtypeStruct((rows, RET_V_W), BF16),
                   jax.ShapeDtypeStruct((n_seq,) + s_shape[1:], F32)],
        scratch_shapes=[pltpu.VMEM(s_shape[1:], F32)],
        compiler_params=_cparams(2), name="retention",
    )(rq, rk, rv, rg, s0, dmat, qdec, kdec)
    return y, sfin


def _lambda(lam_ref):
    lp = lam_ref[...]
    a = jnp.exp(jnp.sum(lp[0:1] * lp[1:2], axis=-1, keepdims=True))
    b = jnp.exp(jnp.sum(lp[2:3] * lp[3:4], axis=-1, keepdims=True))
    return a - b + LAM_INIT


def _stack_maps(q):
    qf = q.astype(F32)
    lane = lax.broadcasted_iota(jnp.int32, q.shape, 1)
    return jnp.concatenate([jnp.where(lane < DIFF_DH, qf, 0.0),
                            jnp.where(lane < DIFF_DH, 0.0, qf)], axis=0).astype(BF16)


def _mix_and_norm(o_all, lam, subg, n):
    o = o_all[:n] - lam * o_all[n:]
    y = o * lax.rsqrt(jnp.mean(o * o, axis=-1, keepdims=True) + DIFF_EPS) * subg
    return (y * (1.0 - LAM_INIT)).astype(BF16)


def _diff_attn_kernel(qt_ref, k_ref, vt_ref, mk_ref, mvt_ref, tdiag_ref, tsub_ref, tmeta_ref,
                      lam_ref, subg_ref, y_ref, qq_scr, m_scr, acc_scr, s_scr):
    i = pl.program_id(2)
    heads = range(HEADS_PER_STEP)
    hs = lambda h: slice(h * DIFF_DV, (h + 1) * DIFF_DV)
    vs = lambda h: slice(h * DV_AUG, (h + 1) * DV_AUG)

    for h in heads:
        qt = qt_ref[0, hs(h), :].astype(F32)
        row = lax.broadcasted_iota(jnp.int32, qt.shape, 0)
        qq_scr[h] = jnp.concatenate([jnp.where(row < DIFF_DH, qt, 0.0),
                                     jnp.where(row < DIFF_DH, 0.0, qt)], axis=1).astype(BF16)

    def scores(h, k_blk, tile):
        s = jnp.dot(k_blk, qq_scr[h], preferred_element_type=F32)
        return s if tile is None else s + tile

    for h in heads:
        m_scr[h] = jnp.full((1, 2 * TQ), NEG_INF, F32)
        acc_scr[h] = jnp.zeros((DV_AUG, 2 * TQ), F32)

    def meta_state():
        out = []
        for h in heads:
            s = scores(h, mk_ref[:, hs(h)], tmeta_ref[h, jnp.minimum(i, 1)])
            m0 = jnp.max(s, axis=0, keepdims=True)
            out.append((m0, jnp.dot(mvt_ref[vs(h), :], jnp.exp2(s - m0).astype(BF16),
                                    preferred_element_type=F32)))
        return out

    def qk(j):
        off = pl.multiple_of(j * TK, TK)
        return tuple(scores(h, k_ref[pl.ds(off, TK), hs(h)], None) for h in heads)

    def consume(j, s):
        m_prev = [m_scr[h] for h in heads]
        m_new = [jnp.maximum(m_prev[h], jnp.max(s[h], axis=0, keepdims=True)) for h in heads]
        p = [jnp.exp2(s[h] - m_new[h]).astype(BF16) for h in heads]
        alpha = [jnp.exp2(m_prev[h] - m_new[h]) for h in heads]
        for h in heads:
            m_scr[h] = m_new[h]
        pv = [jnp.dot(vt_ref[j, vs(h), :], p[h], preferred_element_type=F32) for h in heads]
        for h in heads:
            acc_scr[h] = alpha[h] * acc_scr[h] + pv[h]

    def qk_to_scratch(j, buf):
        for h, s in enumerate(qk(j)):
            s_scr[buf, h] = s

    def far_pair(t, carry):
        j = 2 * t
        qk_to_scratch(j + 1, 1)
        consume(j, [s_scr[0, h] for h in heads])
        qk_to_scratch(j + 2, 0)
        consume(j + 1, [s_scr[1, h] for h in heads])
        return carry

    j_sub = jnp.maximum(i - 1, 0)
    n_pairs = j_sub // 2
    qk_to_scratch(0, 0)
    lax.fori_loop(0, n_pairs, far_pair, 0)

    def finish(meta):
        lam = _lambda(lam_ref)
        for h in heads:
            m_meta, acc_meta = meta[h]
            m_main = m_scr[h]
            m = jnp.maximum(m_main, m_meta)
            acc = jnp.exp2(m_main - m) * acc_scr[h] + jnp.exp2(m_meta - m) * acc_meta
            o_all = acc[:DIFF_DV] / acc[DIFF_DV:DIFF_DV + 1]
            o = o_all[:, :TQ] - lam * o_all[:, TQ:]
            y = o * lax.rsqrt(jnp.mean(o * o, axis=0, keepdims=True) + DIFF_EPS) * subg_ref[...]
            y_ref[:, hs(h)] = (y * (1.0 - LAM_INIT)).T.astype(BF16)

    pen = jnp.where(i == 0, 3.0 * NEG_INF, 0.0)

    def last_blocks(sub_buf):
        meta = meta_state()
        s_diag = qk(i)
        consume(j_sub, [s_scr[sub_buf, h] + (tsub_ref[h] + pen) for h in heads])
        consume(i, [s_diag[h] + tdiag_ref[h] for h in heads])
        finish(meta)

    @pl.when(j_sub % 2 == 1)
    def _():
        qk_to_scratch(j_sub, 1)
        consume(j_sub - 1, [s_scr[0, h] for h in heads])
        last_blocks(1)

    @pl.when(j_sub % 2 == 0)
    def _():
        last_blocks(0)


def _diff_attn(dqt, dk, dvt, mk, mvt, tdiag, tsub, tmeta, lam_p, subg_b, n_seq):
    rows = dk.shape[0]
    seq = rows // n_seq
    nq = seq // TQ
    hb = HEADS_PER_STEP
    wid = hb * DIFF_DV
    wid_v = hb * DV_AUG
    tile3 = lambda b, g, i: (g, 0, 0)
    slow = lambda shape, imap: pl.BlockSpec(shape, imap, pipeline_mode=pl.Buffered(1))
    return pl.pallas_call(
        _diff_attn_kernel, grid=(n_seq, DIFF_HEADS // hb, nq),
        in_specs=[pl.BlockSpec((1, wid, TQ), lambda b, g, i: (b * nq + i, g, 0)),
                  slow((seq, wid), lambda b, g, i: (b, g)),
                  slow((nq, wid_v, TK), lambda b, g, i: (b, g, 0)),
                  slow((N_META, wid), lambda b, g, i: (0, g)),
                  slow((wid_v, N_META), lambda b, g, i: (g, 0)),
                  slow((hb, TK, 2 * TQ), tile3),
                  slow((hb, TK, 2 * TQ), tile3),
                  slow((hb, 2, N_META, 2 * TQ), lambda b, g, i: (g, 0, 0, 0)),
                  slow((4, DIFF_DH), lambda b, g, i: (0, 0)),
                  slow((DIFF_DV, TQ), lambda b, g, i: (0, 0))],
        out_specs=pl.BlockSpec((TQ, wid), lambda b, g, i: (b * nq + i, g)),
        out_shape=jax.ShapeDtypeStruct((rows, DIFF_W), BF16),
        scratch_shapes=[pltpu.VMEM((hb, DIFF_DV, 2 * TQ), BF16), pltpu.VMEM((hb, 1, 2 * TQ), F32),
                        pltpu.VMEM((hb, DV_AUG, 2 * TQ), F32),
                        pltpu.VMEM((2, hb, TK, 2 * TQ), F32)],
        compiler_params=_cparams(3), name="diff_attn",
    )(dqt, dk, dvt, mk, mvt, tdiag, tsub, tmeta, lam_p, subg_b)


def _sample_attn_kernel(q_ref, ck_ref, cv_ref, nk_ref, nv_ref, tile_ref, lam_ref, subg_ref,
                        y_ref, k_scr, v_scr, *, past, n_new):
    pad = k_scr.shape[1] - past - n_new
    lam = _lambda(lam_ref)
    for h in range(DIFF_HEADS):
        hs = slice(h * DIFF_DV, (h + 1) * DIFF_DV)
        k_scr[h, 0:past] = ck_ref[0, pl.ds(h, past, stride=DIFF_HEADS), :].astype(BF16)
        v_scr[h, 0:past] = cv_ref[0, pl.ds(h, past, stride=DIFF_HEADS), :].astype(BF16)
        k_scr[h, past:past + n_new] = nk_ref[:, hs]
        v_scr[h, past:past + n_new] = nv_ref[:, hs]
        k_scr[h, past + n_new:] = jnp.zeros((pad, LANES), BF16)
        v_scr[h, past + n_new:] = jnp.zeros((pad, LANES), BF16)
        qq = _stack_maps(q_ref[:, hs])
        s = lax.dot_general(qq, k_scr[h], NT_DIMS, preferred_element_type=F32)
        s = (s.reshape(2, n_new, s.shape[-1]) + tile_ref[h][None]).reshape(2 * n_new, s.shape[-1])
        m = jnp.max(s, axis=1, keepdims=True)
        p = jnp.exp2(s - m)
        l = jnp.sum(p, axis=1, keepdims=True)
        o_all = jnp.dot(p.astype(BF16), v_scr[h], preferred_element_type=F32) / l
        y_ref[:, hs] = _mix_and_norm(o_all, lam, subg_ref[...], n_new)


def _sample_attn(dq, ck, cv, nk, nv, tile, lam_p, subg, n_seq, past, n_new):
    kpad = tile.shape[-1]
    row = pl.BlockSpec((n_new, DIFF_W), lambda b: (b, 0))
    cache = pl.BlockSpec((1, past * DIFF_HEADS, DIFF_DV), lambda b: (b, 0, 0))
    return pl.pallas_call(
        functools.partial(_sample_attn_kernel, past=past, n_new=n_new),
        grid=(n_seq,),
        in_specs=[row, cache, cache, row, row, _resident(tile.shape),
                  _resident((4, DIFF_DH)), _resident((1, DIFF_DV))],
        out_specs=row,
        out_shape=jax.ShapeDtypeStruct((n_seq * n_new, DIFF_W), BF16),
        scratch_shapes=[pltpu.VMEM((DIFF_HEADS, kpad, LANES), BF16),
                        pltpu.VMEM((DIFF_HEADS, kpad, LANES), BF16)],
        compiler_params=_cparams(1), name="sample_attn",
    )(dq, ck, cv, nk, nv, tile, lam_p, subg)


def _tail_kernel(h_ref, yr_ref, yd_ref, gr_ref, gd_ref, wrb_ref, wdb_ref, wo_ref, n2_ref,
                 wup_ref, wdn_ref, nf_ref, out_ref):
    a = jnp.dot(yr_ref[...], wrb_ref[...], preferred_element_type=F32)
    b = jnp.dot(yd_ref[...], wdb_ref[...], preferred_element_type=F32)
    merged = jax.nn.sigmoid(gr_ref[...]) * a + jax.nn.sigmoid(gd_ref[...]) * b
    h = h_ref[...] + jnp.dot(merged.astype(BF16), wo_ref[...], preferred_element_type=F32)
    xn = h * lax.rsqrt(jnp.mean(h * h, axis=-1, keepdims=True) + NORM_EPS) * n2_ref[...]
    xn = xn.astype(BF16)
    acc = jnp.zeros_like(h)
    for lo in range(0, D_FF, FF_CHUNK):
        n = min(FF_CHUNK, D_FF - lo)
        gate = jnp.dot(xn, wup_ref[:, lo:lo + n], preferred_element_type=F32)
        up = jnp.dot(xn, wup_ref[:, D_FF + lo:D_FF + lo + n], preferred_element_type=F32)
        act = (gate * jax.nn.sigmoid(gate) * up).astype(BF16)
        acc = acc + jnp.dot(act, wdn_ref[lo:lo + n, :], preferred_element_type=F32)
    h = h + acc
    out_ref[...] = h * lax.rsqrt(jnp.mean(h * h, axis=-1, keepdims=True) + NORM_EPS) * nf_ref[...]


def _tail(h2d, y_ret, y_diff, gr, gd, wrb, wdb, wo, n2, wup, wdn, nf, tm):
    rows = h2d.shape[0]
    row = pl.BlockSpec((tm, D_MODEL), lambda i: (i, 0))
    return pl.pallas_call(
        _tail_kernel, grid=(rows // tm,),
        in_specs=[row, row, row, row, row,
                  _resident(wrb.shape), _resident(wdb.shape), _resident(wo.shape),
                  _resident(n2.shape), _resident(wup.shape), _resident(wdn.shape),
                  _resident(nf.shape)],
        out_specs=row,
        out_shape=jax.ShapeDtypeStruct((rows, D_MODEL), F32),
        compiler_params=_cparams(1), name="tail",
    )(h2d, y_ret, y_diff, gr, gd, wrb, wdb, wo, n2, wup, wdn, nf)


def kernel(x_prompt, x_sample, cache_k, cache_v, state_ret, meta_tokens, rel_bias, norm1_g, w_in,
           lambda_q1, lambda_k1, lambda_q2, lambda_k2, diff_subln_g, w_ret_branch, w_diff_branch,
           w_o, norm2_g, w_ffn_up, w_ffn_down, normf_g):
    assert w_in.shape[0] == 1, "single-layer step only"
    bsz, seq, _ = x_prompt.shape
    dbsz, dseq, _ = x_sample.shape
    past = cache_k.shape[2]
    assert TQ == TK == TM_PROJ and TQ % CHUNK == 0 and DIFF_HEADS % HEADS_PER_STEP == 0
    assert seq % TQ == 0
    assert dseq <= CHUNK and past % CHUNK == 0 and meta_tokens.shape[0] == N_META

    w_in_bf = w_in[0].astype(BF16)
    g1 = norm1_g[0][None, :]
    lam_p = jnp.stack([lambda_q1[0], lambda_k1[0], lambda_q2[0], lambda_k2[0]])
    subg = diff_subln_g[0][None, :]
    wts = (w_ret_branch[0].astype(BF16), w_diff_branch[0].astype(BF16), w_o[0].astype(BF16),
           norm2_g[0][None, :], w_ffn_up[0].astype(BF16), w_ffn_down[0].astype(BF16),
           normf_g[None, :])

    kpad = ((past + dseq + LANES - 1) // LANES) * LANES
    both_maps = lambda ix: np.concatenate([ix.T, ix.T], axis=1)
    idx_diag = both_maps(_bias_idx(np.arange(TQ), np.arange(TK), TK))
    idx_sub = both_maps(_bias_idx(np.arange(TQ) + TK, np.arange(TK), TK))
    idx_meta = np.concatenate(
        [both_maps(_bias_idx(np.arange(TQ) + b * TQ, np.arange(N_META) - N_META, N_META))
         for b in range(2)], axis=0)
    idx_samp = _bias_idx(past + np.arange(dseq), np.arange(kpad), past + dseq)
    t_diag, t_sub, t_meta, t_samp = _bias_tiles(rel_bias, [idx_diag, idx_sub, idx_meta, idx_samp],
                                                [True, True, True, False])
    t_meta = t_meta.reshape(DIFF_HEADS, 2, N_META, 2 * TQ)

    cs_m, sn_m = _rotary_tables(np.arange(-N_META, 0))
    m_out = _in_proj(meta_tokens, g1, w_in_bf, cs_m, sn_m, N_META)
    zero_state = jnp.zeros((1, RET_HEADS, RET_DK, RET_DV), F32)
    _, s_meta = _retention(m_out[0], m_out[1], m_out[2], m_out[3], zero_state, 1, N_META,
                           with_output=False)
    mk = m_out[5]
    mv_aug = jnp.concatenate([m_out[7].reshape(N_META, DIFF_HEADS, DIFF_DV),
                              jnp.ones((N_META, DIFF_HEADS, ONES_ROWS), BF16)], axis=2)
    mvt = mv_aug.reshape(N_META, DIFF_HEADS * DV_AUG).T
    subg_b = jnp.broadcast_to(diff_subln_g[0][:, None], (DIFF_DV, TQ))

    cs_p, sn_p = _rotary_tables(np.arange(seq))
    x2d = x_prompt.reshape(bsz * seq, D_MODEL)
    meta_rows = lambda u: u.reshape(N_META * DIFF_HEADS, DIFF_DV)
    (y_ret, s_fin, dqt, dkb, dvt, gr, gd, k_rows, v_rows) = _prompt_proj(
        x2d, g1, w_in_bf, cs_p, sn_p, s_meta, meta_rows(m_out[6]), meta_rows(m_out[8]), bsz,
        TM_PROJ)
    y_diff = _diff_attn(dqt, dkb, dvt, mk, mvt, t_diag, t_sub, t_meta, lam_p, subg_b, bsz)
    y_prompt = _tail(x2d, y_ret, y_diff, gr, gd, *wts, TM_TAIL).reshape(bsz, seq, D_MODEL)

    cs_s, sn_s = _rotary_tables(np.tile(past + np.arange(dseq), dbsz))
    xs2d = x_sample.reshape(dbsz * dseq, D_MODEL)
    (rq, rk, rv, rg, dq, dkb, dkf_s, dvb, dvf_s, gr, gd) = _in_proj(xs2d, g1, w_in_bf, cs_s, sn_s,
                                                                     dbsz * dseq)
    y_ret, s_samp = _retention(rq, rk, rv, rg, state_ret[0], dbsz, dseq)
    ck = cache_k[0].reshape(dbsz, past * DIFF_HEADS, DIFF_DV)
    cv = cache_v[0].reshape(dbsz, past * DIFF_HEADS, DIFF_DV)
    y_diff = _sample_attn(dq, ck, cv, dkb, dvb, t_samp, lam_p, subg, dbsz, past, dseq)
    y_sample = _tail(xs2d, y_ret, y_diff, gr, gd, *wts, dbsz * dseq).reshape(dbsz, dseq, D_MODEL)

    heads = (DIFF_HEADS, DIFF_DV)
    return (y_prompt, y_sample,
            k_rows.reshape((1, bsz, N_META + seq) + heads),
            v_rows.reshape((1, bsz, N_META + seq) + heads),
            s_fin[None],
            dkf_s.reshape((1, dbsz, dseq) + heads),
            dvf_s.reshape((1, dbsz, dseq) + heads),
            s_samp[None])
```

```python
import functools
import math

import numpy as np
import jax
import jax.numpy as jnp
from jax import lax
from jax.experimental import pallas as pl
from jax.experimental.pallas import tpu as pltpu

F32 = jnp.float32
BF16 = jnp.bfloat16

D_MODEL = 1024
CHUNK = 64
N_META = 16
RET_HEADS = 4
RET_DK = D_MODEL // 8
RET_DV = 2 * RET_DK
RET_QK_W = RET_HEADS * RET_DK
RET_V_W = RET_HEADS * RET_DV
ROPE_BASE = 10000.0
RET_EPS = 1e-6
DIFF_HEADS = 8
DIFF_DH = D_MODEL // 16
DIFF_DV = 2 * DIFF_DH
DIFF_W = DIFF_HEADS * DIFF_DV
DIFF_EPS = 1e-5
N_BUCKETS = 32
MAX_DISTANCE = 128
D_FF = ((8 * D_MODEL + 3 * 256 - 1) // (3 * 256)) * 256
NORM_EPS = 1e-6
NEG_INF = -1e30
LAM_INIT = 0.8 - 0.6 * math.exp(-0.3 * 0)
LOG2E = math.log2(math.e)
Q_SCALE = DIFF_DH ** -0.5 * LOG2E
ONES_ROWS = 16
DV_AUG = DIFF_DV + ONES_ROWS

OFF_RQ = 0
OFF_RK = OFF_RQ + RET_QK_W
OFF_RV = OFF_RK + RET_QK_W
OFF_RG = OFF_RV + RET_V_W
OFF_DQ = OFF_RG + RET_V_W
OFF_DK = OFF_DQ + DIFF_W
OFF_DV = OFF_DK + DIFF_W
OFF_GR = OFF_DV + DIFF_W
OFF_GD = OFF_GR + D_MODEL
W_IN = OFF_GD + D_MODEL

LANES = 128
VMEM_LIMIT_BYTES = 56 * 1024 * 1024
TM_PROJ = 256
TM_TAIL = 256
TQ = 256
TK = 256
HEADS_PER_STEP = 4
SCRATCH_W = 2 * TQ + LANES
FF_CHUNK = 512

NT_DIMS = (((1,), (1,)), ((), ()))
TN_DIMS = (((0,), (0,)), ((), ()))


def _cparams(n_axes):
    return pltpu.CompilerParams(dimension_semantics=("arbitrary",) * n_axes,
                                vmem_limit_bytes=VMEM_LIMIT_BYTES)


def _resident(shape):
    nd = len(shape)
    return pl.BlockSpec(shape, lambda *_: (0,) * nd, pipeline_mode=pl.Buffered(1))


def _t5_bucket_np(rel):
    nb = N_BUCKETS // 2
    max_exact = nb // 2
    ret = np.where(rel > 0, nb, 0)
    n = np.abs(rel)
    nf = np.maximum(n, max_exact).astype(np.float64)
    large = max_exact + (np.log(nf / max_exact) / math.log(MAX_DISTANCE / max_exact)
                         * (nb - max_exact)).astype(np.int32)
    large = np.minimum(large, nb - 1)
    return (ret + np.where(n < max_exact, n, large)).astype(np.int32)


def _bias_idx(qpos, kpos, n_real):
    qpos = np.asarray(qpos)[:, None]
    kpos = np.asarray(kpos)[None, :]
    vis = (np.floor_divide(kpos, CHUNK) <= np.floor_divide(qpos, CHUNK))
    vis = vis & (np.arange(kpos.shape[1])[None, :] < n_real)
    return np.where(vis, _t5_bucket_np(kpos - qpos), -1).astype(np.int32)


def _rotary_tables(pos):
    half = RET_DK // 2
    freq = 1.0 / (ROPE_BASE ** jnp.linspace(0.0, 1.0, half, dtype=F32))
    ang = jnp.asarray(pos, F32)[:, None] * freq[None, :]
    cos, sin = jnp.cos(ang), jnp.sin(ang)
    return jnp.concatenate([cos, cos], axis=1), jnp.concatenate([-sin, sin], axis=1)


def _retention_tables(t):
    gam = 1.0 - 2.0 ** (-5.0 - np.arange(RET_HEADS, dtype=np.float64))
    lg = np.log(gam)[:, None, None]
    n = np.arange(t)[:, None]
    m = np.arange(t)[None, :]
    cn, cm = n // CHUNK, m // CHUNK
    expo = np.where(cm == cn, np.abs(n - m), n - m).astype(np.float64)[None]
    dmat = np.where((cm <= cn)[None], np.exp(lg * expo), 0.0)
    qdec = np.exp(lg[:, :, 0] * (np.arange(t)[None, :] + 1.0))[:, :, None]
    kdec = np.exp(lg[:, :, 0] * (t - 1.0 - np.arange(t)[None, :]))[:, :, None]
    sdec = np.exp(lg[:, 0, 0] * t)
    return (jnp.asarray(dmat, F32), jnp.asarray(qdec, F32), jnp.asarray(kdec, F32),
            tuple(float(s) for s in sdec))


def _bias_kernel(rb_ref, *refs, n_tiles, shifted):
    idx_refs, out_refs = refs[:n_tiles], refs[n_tiles:]
    h = pl.program_id(0)
    far = rb_ref[N_BUCKETS // 2 - 1, h]
    for idx_ref, out_ref, sh in zip(idx_refs, out_refs, shifted):
        idx = idx_ref[...]
        acc = jnp.full(idx.shape, NEG_INF, F32)
        for b in range(N_BUCKETS):
            val = (rb_ref[b, h] - far if sh else rb_ref[b, h]) * LOG2E
            acc = jnp.where(idx == b, val, acc)
        out_ref[0] = acc


def _bias_tiles(rel_bias, idx_list, shifted):
    n = len(idx_list)
    in_specs = [pl.BlockSpec(memory_space=pltpu.SMEM)]
    in_specs += [pl.BlockSpec(ix.shape, lambda h: (0, 0)) for ix in idx_list]
    out_specs = [pl.BlockSpec((1,) + ix.shape, lambda h: (h, 0, 0)) for ix in idx_list]
    out_shape = [jax.ShapeDtypeStruct((DIFF_HEADS,) + ix.shape, F32) for ix in idx_list]
    return pl.pallas_call(
        functools.partial(_bias_kernel, n_tiles=n, shifted=tuple(shifted)),
        grid=(DIFF_HEADS,), in_specs=in_specs, out_specs=out_specs, out_shape=out_shape,
        compiler_params=_cparams(1), name="bias_tiles",
    )(rel_bias, *[jnp.asarray(ix) for ix in idx_list])


def _norm_proj(x_ref, g_ref, w_ref):
    x = x_ref[...]
    xn = x * lax.rsqrt(jnp.mean(x * x, axis=-1, keepdims=True) + NORM_EPS) * g_ref[...]
    xn = xn.astype(BF16)
    return lambda lo, n: jnp.dot(xn, w_ref[:, lo:lo + n], preferred_element_type=F32)


def _rotary(u, cs, sn):
    return u * cs + pltpu.roll(u, RET_DK // 2, 1) * sn


def _retention_head(qf, kf, v, gate, state, dm, qd, kd, sdec, with_output=True):
    y = None
    if with_output:
        s = lax.dot_general(qf.astype(BF16), kf.astype(BF16), NT_DIMS,
                            preferred_element_type=F32) * dm
        o = jnp.dot(s.astype(BF16), v, preferred_element_type=F32)
        o = o + jnp.dot((qf * qd).astype(BF16), state.astype(BF16), preferred_element_type=F32)
        y = o * lax.rsqrt(jnp.mean(o * o, axis=-1, keepdims=True) + RET_EPS)
        y = y * (gate * jax.nn.sigmoid(gate))
    kv = lax.dot_general((kf * kd).astype(BF16), v, TN_DIMS, preferred_element_type=F32)
    return y, sdec * state + kv


def _inproj_kernel(x_ref, g_ref, w_ref, cs_ref, sn_ref,
                   rq_ref, rk_ref, rv_ref, rg_ref, dq_ref, dkb_ref, dkf_ref,
                   dvb_ref, dvf_ref, gr_ref, gd_ref):
    proj = _norm_proj(x_ref, g_ref, w_ref)
    cs = cs_ref[...]
    sn = sn_ref[...]
    uq = proj(OFF_RQ, RET_QK_W)
    uk = proj(OFF_RK, RET_QK_W)
    for hh in range(RET_HEADS):
        sl = slice(hh * RET_DK, (hh + 1) * RET_DK)
        rq_ref[:, sl] = (_rotary(uq[:, sl], cs, sn) * (RET_DK ** -0.5)).astype(BF16)
        rk_ref[:, sl] = _rotary(uk[:, sl], cs, sn).astype(BF16)
    rv_ref[...] = proj(OFF_RV, RET_V_W).astype(BF16)
    rg_ref[...] = proj(OFF_RG, RET_V_W)
    dq_ref[...] = (proj(OFF_DQ, DIFF_W) * Q_SCALE).astype(BF16)
    dk = proj(OFF_DK, DIFF_W)
    dkf_ref[...] = dk
    dkb_ref[...] = dk.astype(BF16)
    dv = proj(OFF_DV, DIFF_W)
    dvf_ref[...] = dv
    dvb_ref[...] = dv.astype(BF16)
    gr_ref[...] = proj(OFF_GR, D_MODEL)
    gd_ref[...] = proj(OFF_GD, D_MODEL)


def _in_proj(x2d, g, w_bf, cs, sn, tm):
    rows = x2d.shape[0]
    n_pos = cs.shape[0] // tm
    row = lambda w: pl.BlockSpec((tm, w), lambda i: (i, 0))
    pos = pl.BlockSpec((tm, LANES), lambda i: (i % n_pos, 0))
    outs = [(RET_QK_W, BF16), (RET_QK_W, BF16), (RET_V_W, BF16), (RET_V_W, F32),
            (DIFF_W, BF16), (DIFF_W, BF16), (DIFF_W, F32), (DIFF_W, BF16), (DIFF_W, F32),
            (D_MODEL, F32), (D_MODEL, F32)]
    return pl.pallas_call(
        _inproj_kernel, grid=(rows // tm,),
        in_specs=[row(D_MODEL), _resident((1, D_MODEL)), _resident((D_MODEL, W_IN)), pos, pos],
        out_specs=[row(w) for w, _ in outs],
        out_shape=[jax.ShapeDtypeStruct((rows, w), dt) for w, dt in outs],
        compiler_params=_cparams(1), name="in_proj",
    )(x2d, g, w_bf, cs, sn)


def _prompt_proj_kernel(x_ref, g_ref, w_ref, cs_ref, sn_ref, s0_ref, dm_ref, qd_ref, kd_ref,
                        mk_ref, mv_ref,
                        yret_ref, sfin_ref, dqt_ref, dkb_ref, dvt_ref, gr_ref, gd_ref,
                        krows_ref, vrows_ref,
                        s_scr, stage_scr, row_sem, meta_sem, *, tiles_per_seq, n_steps, sdec):
    i = pl.program_id(0)
    tm = x_ref.shape[0]
    proj = _norm_proj(x_ref, g_ref, w_ref)
    cs = cs_ref[...]
    sn = sn_ref[...]
    slot = i % 2
    stream = i // tiles_per_seq
    rows_out = (krows_ref, vrows_ref)

    def rows_copy(which, step):
        first = (N_META + (step % tiles_per_seq) * tm) * DIFF_HEADS
        return pltpu.make_async_copy(
            stage_scr.at[step % 2, which],
            rows_out[which].at[step // tiles_per_seq, pl.ds(first, tm * DIFF_HEADS), :],
            row_sem.at[step % 2, which])

    def meta_copy(which):
        return pltpu.make_async_copy(
            (mk_ref, mv_ref)[which],
            rows_out[which].at[stream, pl.ds(0, N_META * DIFF_HEADS), :], meta_sem.at[which])

    @pl.when(i % tiles_per_seq == 0)
    def _():
        s_scr[...] = s0_ref[0]
        for which in range(2):
            meta_copy(which).start()

    @pl.when(i >= 2)
    def _():
        for which in range(2):
            rows_copy(which, i - 2).wait()

    uq = proj(OFF_RQ, RET_QK_W)
    uk = proj(OFF_RK, RET_QK_W)
    rv = proj(OFF_RV, RET_V_W).astype(BF16)
    rg = proj(OFF_RG, RET_V_W)
    for hh in range(RET_HEADS):
        qs = slice(hh * RET_DK, (hh + 1) * RET_DK)
        vs = slice(hh * RET_DV, (hh + 1) * RET_DV)
        y, state = _retention_head(_rotary(uq[:, qs], cs, sn) * (RET_DK ** -0.5),
                                   _rotary(uk[:, qs], cs, sn), rv[:, vs], rg[:, vs],
                                   s_scr[hh], dm_ref[hh], qd_ref[hh], kd_ref[hh], sdec[hh])
        yret_ref[:, vs] = y.astype(BF16)
        s_scr[hh] = state

    @pl.when(i % tiles_per_seq == tiles_per_seq - 1)
    def _():
        sfin_ref[0] = s_scr[...]

    dqt_ref[0] = (proj(OFF_DQ, DIFF_W) * Q_SCALE).T.astype(BF16)
    dk = proj(OFF_DK, DIFF_W)
    dkb_ref[...] = dk.astype(BF16)
    dv = proj(OFF_DV, DIFF_W)
    dvt = dv.T.astype(BF16)
    ones = jnp.ones((ONES_ROWS, dvt.shape[1]), BF16)
    for hh in range(DIFF_HEADS):
        hs = slice(hh * DIFF_DV, (hh + 1) * DIFF_DV)
        dvt_ref[0, hh * DV_AUG:hh * DV_AUG + DIFF_DV, :] = dvt[hs]
        dvt_ref[0, hh * DV_AUG + DIFF_DV:(hh + 1) * DV_AUG, :] = ones
        head_rows = pl.ds(hh, tm, stride=DIFF_HEADS)
        stage_scr[slot, 0, head_rows, :] = dk[:, hs]
        stage_scr[slot, 1, head_rows, :] = dv[:, hs]
    for which in range(2):
        rows_copy(which, i).start()
    gr_ref[...] = proj(OFF_GR, D_MODEL)
    gd_ref[...] = proj(OFF_GD, D_MODEL)

    @pl.when(i % tiles_per_seq == 0)
    def _():
        for which in range(2):
            meta_copy(which).wait()

    @pl.when(i == n_steps - 1)
    def _():
        for which in range(2):
            if n_steps >= 2:
                rows_copy(which, i - 1).wait()
            rows_copy(which, i).wait()


def _prompt_proj(x2d, g, w_bf, cs, sn, s0, mk_rows, mv_rows, n_seq, tm):
    rows = x2d.shape[0]
    seq = rows // n_seq
    tps = seq // tm
    dmat, qdec, kdec, sdec = _retention_tables(tm)
    row = lambda w: pl.BlockSpec((tm, w), lambda i: (i, 0))
    pos = pl.BlockSpec((tm, LANES), lambda i: (i % tps, 0))
    tposed = lambda w: pl.BlockSpec((1, w, tm), lambda i: (i, 0, 0))
    hbm = pl.BlockSpec(memory_space=pl.ANY)
    s_shape = (1, RET_HEADS, RET_DK, RET_DV)
    wv = DIFF_HEADS * DV_AUG
    rows_shape = jax.ShapeDtypeStruct((n_seq, (N_META + seq) * DIFF_HEADS, DIFF_DV), F32)
    return pl.pallas_call(
        functools.partial(_prompt_proj_kernel, tiles_per_seq=tps, n_steps=rows // tm, sdec=sdec),
        grid=(rows // tm,),
        in_specs=[row(D_MODEL), _resident((1, D_MODEL)), _resident((D_MODEL, W_IN)), pos, pos,
                  _resident(s_shape), _resident(dmat.shape), _resident(qdec.shape),
                  _resident(kdec.shape), _resident(mk_rows.shape), _resident(mv_rows.shape)],
        out_specs=[row(RET_V_W), pl.BlockSpec(s_shape, lambda i: (i // tps, 0, 0, 0)),
                   tposed(DIFF_W), row(DIFF_W), tposed(wv), row(D_MODEL), row(D_MODEL),
                   hbm, hbm],
        out_shape=[jax.ShapeDtypeStruct((rows, RET_V_W), BF16),
                   jax.ShapeDtypeStruct((n_seq,) + s_shape[1:], F32),
                   jax.ShapeDtypeStruct((rows // tm, DIFF_W, tm), BF16),
                   jax.ShapeDtypeStruct((rows, DIFF_W), BF16),
                   jax.ShapeDtypeStruct((rows // tm, wv, tm), BF16),
                   jax.ShapeDtypeStruct((rows, D_MODEL), F32),
                   jax.ShapeDtypeStruct((rows, D_MODEL), F32),
                   rows_shape, rows_shape],
        scratch_shapes=[pltpu.VMEM(s_shape[1:], F32),
                        pltpu.VMEM((2, 2, tm * DIFF_HEADS, DIFF_DV), F32),
                        pltpu.SemaphoreType.DMA((2, 2)), pltpu.SemaphoreType.DMA((2,))],
        compiler_params=_cparams(1), name="prompt_proj",
    )(x2d, g, w_bf, cs, sn, s0, dmat, qdec, kdec, mk_rows, mv_rows)


def _retention_kernel(rq_ref, rk_ref, rv_ref, rg_ref, s0_ref, dm_ref, qd_ref, kd_ref,
                      y_ref, sfin_ref, s_scr, *, sdec, with_output):
    t = pl.program_id(1)

    @pl.when(t == 0)
    def _():
        s_scr[...] = s0_ref[0]

    for hh in range(RET_HEADS):
        qs = slice(hh * RET_DK, (hh + 1) * RET_DK)
        vs = slice(hh * RET_DV, (hh + 1) * RET_DV)
        y, state = _retention_head(rq_ref[:, qs].astype(F32), rk_ref[:, qs].astype(F32),
                                   rv_ref[:, vs], rg_ref[:, vs], s_scr[hh], dm_ref[hh],
                                   qd_ref[hh], kd_ref[hh], sdec[hh], with_output)
        if with_output:
            y_ref[:, vs] = y.astype(BF16)
        else:
            y_ref[:, vs] = jnp.zeros((y_ref.shape[0], RET_DV), BF16)
        s_scr[hh] = state

    @pl.when(t == pl.num_programs(1) - 1)
    def _():
        sfin_ref[0] = s_scr[...]


def _retention(rq, rk, rv, rg, s0, n_seq, t, with_output=True):
    rows = rq.shape[0]
    nt = rows // n_seq // t
    dmat, qdec, kdec, sdec = _retention_tables(t)
    row = lambda w: pl.BlockSpec((t, w), lambda b, i: (b * nt + i, 0))
    s_shape = (1, RET_HEADS, RET_DK, RET_DV)
    if s0.shape[0] == 1:
        s0_spec = pl.BlockSpec(s_shape, lambda b, i: (0, 0, 0, 0))
    else:
        s0_spec = pl.BlockSpec(s_shape, lambda b, i: (b, 0, 0, 0))
    y, sfin = pl.pallas_call(
        functools.partial(_retention_kernel, sdec=sdec, with_output=with_output),
        grid=(n_seq, nt),
        in_specs=[row(RET_QK_W), row(RET_QK_W), row(RET_V_W), row(RET_V_W), s0_spec,
                  _resident(dmat.shape), _resident(qdec.shape), _resident(kdec.shape)],
        out_specs=[row(RET_V_W), pl.BlockSpec(s_shape, lambda b, i: (b, 0, 0, 0))],
        out_shape=[jax.ShapeDtypeStruct((rows, RET_V_W), BF16),
                   jax.ShapeDtypeStruct((n_seq,) + s_shape[1:], F32)],
        scratch_shapes=[pltpu.VMEM(s_shape[1:], F32)],
        compiler_params=_cparams(2), name="retention",
    )(rq, rk, rv, rg, s0, dmat, qdec, kdec)
    return y, sfin


def _lambda(lam_ref):
    lp = lam_ref[...]
    a = jnp.exp(jnp.sum(lp[0:1] * lp[1:2], axis=-1, keepdims=True))
    b = jnp.exp(jnp.sum(lp[2:3] * lp[3:4], axis=-1, keepdims=True))
    return a - b + LAM_INIT


def _stack_maps(q):
    qf = q.astype(F32)
    lane = lax.broadcasted_iota(jnp.int32, q.shape, 1)
    return jnp.concatenate([jnp.where(lane < DIFF_DH, qf, 0.0),
                            jnp.where(lane < DIFF_DH, 0.0, qf)], axis=0).astype(BF16)


def _mix_and_norm(o_all, lam, subg, n):
    o = o_all[:n] - lam * o_all[n:]
    y = o * lax.rsqrt(jnp.mean(o * o, axis=-1, keepdims=True) + DIFF_EPS) * subg
    return (y * (1.0 - LAM_INIT)).astype(BF16)


def _diff_attn_kernel(qt_ref, k_ref, vt_ref, mk_ref, mvt_ref, tdiag_ref, tsub_ref, tmeta_ref,
                      lam_ref, subg_ref, y_ref, qq_scr, m_scr, acc_scr, s_scr):
    i = pl.program_id(2)
    heads = range(HEADS_PER_STEP)
    hs = lambda h: slice(h * DIFF_DV, (h + 1) * DIFF_DV)
    vs = lambda h: slice(h * DV_AUG, (h + 1) * DV_AUG)
    W = 2 * TQ

    for h in heads:
        qt = qt_ref[0, hs(h), :].astype(F32)
        row = lax.broadcasted_iota(jnp.int32, qt.shape, 0)
        qq_scr[h, :, :W] = jnp.concatenate([jnp.where(row < DIFF_DH, qt, 0.0),
                                            jnp.where(row < DIFF_DH, 0.0, qt)],
                                           axis=1).astype(BF16)

    def scores(h, k_blk, tile):
        s = jnp.dot(k_blk, qq_scr[h, :, :W], preferred_element_type=F32)
        return s if tile is None else s + tile

    for h in heads:
        m_scr[h] = jnp.full((1, W), NEG_INF, F32)
        acc_scr[h, :, :W] = jnp.zeros((DV_AUG, W), F32)

    def meta_state():
        out = []
        for h in heads:
            s = scores(h, mk_ref[:, hs(h)], tmeta_ref[h, jnp.minimum(i, 1)])
            m0 = jnp.max(s, axis=0, keepdims=True)
            out.append((m0, jnp.dot(mvt_ref[vs(h), :], jnp.exp2(s - m0).astype(BF16),
                                    preferred_element_type=F32)))
        return out

    def qk(j):
        off = pl.multiple_of(j * TK, TK)
        return tuple(scores(h, k_ref[pl.ds(off, TK), hs(h)], None) for h in heads)

    def consume(j, s):
        m_prev = [m_scr[h] for h in heads]
        m_new = [jnp.maximum(m_prev[h], jnp.max(s[h], axis=0, keepdims=True)) for h in heads]
        p = [jnp.exp2(s[h] - m_new[h]).astype(BF16) for h in heads]
        alpha = [jnp.exp2(m_prev[h] - m_new[h]) for h in heads]
        for h in heads:
            m_scr[h] = m_new[h]
        pv = [jnp.dot(vt_ref[j, vs(h), :], p[h], preferred_element_type=F32) for h in heads]
        for h in heads:
            acc_scr[h, :, :W] = alpha[h] * acc_scr[h, :, :W] + pv[h]

    def qk_to_scratch(j, buf, tiles=None):
        off = pl.multiple_of(j * TK, TK)
        for h in heads:
            s_scr[buf, h, :, :W] = scores(h, k_ref[pl.ds(off, TK), hs(h)],
                                          None if tiles is None else tiles[h])

    def from_scratch(buf):
        return [s_scr[buf, h, :, :W] for h in heads]

    def far_pair(t, carry):
        j = 2 * t
        qk_to_scratch(j + 1, 1)
        consume(j, from_scratch(0))
        qk_to_scratch(j + 2, 0)
        consume(j + 1, from_scratch(1))
        return carry

    j_sub = jnp.maximum(i - 1, 0)
    n_pairs = j_sub // 2
    qk_to_scratch(0, 0)
    lax.fori_loop(0, n_pairs, far_pair, 0)

    def finish(meta):
        lam = _lambda(lam_ref)
        for h in heads:
            m_meta, acc_meta = meta[h]
            m_main = m_scr[h]
            m = jnp.maximum(m_main, m_meta)
            acc = jnp.exp2(m_main - m) * acc_scr[h, :, :W] + jnp.exp2(m_meta - m) * acc_meta
            o_all = acc[:DIFF_DV] / acc[DIFF_DV:DIFF_DV + 1]
            o = o_all[:, :TQ] - lam * o_all[:, TQ:]
            y = o * lax.rsqrt(jnp.mean(o * o, axis=0, keepdims=True) + DIFF_EPS) * subg_ref[...]
            y_ref[:, hs(h)] = (y * (1.0 - LAM_INIT)).T.astype(BF16)

    @pl.when(j_sub % 2 == 1)
    def _():
        meta = meta_state()
        qk_to_scratch(j_sub, 1, tsub_ref)
        consume(j_sub - 1, from_scratch(0))
        qk_to_scratch(i, 0, tdiag_ref)
        consume(j_sub, from_scratch(1))
        consume(i, from_scratch(0))
        finish(meta)

    @pl.when(j_sub % 2 == 0)
    def _():
        meta = meta_state()
        qk_to_scratch(i, 1, tdiag_ref)
        pen = jnp.where(i == 0, 3.0 * NEG_INF, 0.0)
        consume(j_sub, [s_scr[0, h, :, :W] + (tsub_ref[h] + pen) for h in heads])
        consume(i, from_scratch(1))
        finish(meta)


def _diff_attn(dqt, dk, dvt, mk, mvt, tdiag, tsub, tmeta, lam_p, subg_b, n_seq):
    rows = dk.shape[0]
    seq = rows // n_seq
    nq = seq // TQ
    hb = HEADS_PER_STEP
    wid = hb * DIFF_DV
    wid_v = hb * DV_AUG
    tile3 = lambda b, g, i: (g, 0, 0)
    slow = lambda shape, imap: pl.BlockSpec(shape, imap, pipeline_mode=pl.Buffered(1))
    return pl.pallas_call(
        _diff_attn_kernel, grid=(n_seq, DIFF_HEADS // hb, nq),
        in_specs=[pl.BlockSpec((1, wid, TQ), lambda b, g, i: (b * nq + i, g, 0)),
                  slow((seq, wid), lambda b, g, i: (b, g)),
                  slow((nq, wid_v, TK), lambda b, g, i: (b, g, 0)),
                  slow((N_META, wid), lambda b, g, i: (0, g)),
                  slow((wid_v, N_META), lambda b, g, i: (g, 0)),
                  slow((hb, TK, 2 * TQ), tile3),
                  slow((hb, TK, 2 * TQ), tile3),
                  slow((hb, 2, N_META, 2 * TQ), lambda b, g, i: (g, 0, 0, 0)),
                  slow((4, DIFF_DH), lambda b, g, i: (0, 0)),
                  slow((DIFF_DV, TQ), lambda b, g, i: (0, 0))],
        out_specs=pl.BlockSpec((TQ, wid), lambda b, g, i: (b * nq + i, g)),
        out_shape=jax.ShapeDtypeStruct((rows, DIFF_W), BF16),
        scratch_shapes=[pltpu.VMEM((hb, DIFF_DV, SCRATCH_W), BF16),
                        pltpu.VMEM((hb, 1, 2 * TQ), F32),
                        pltpu.VMEM((hb, DV_AUG, SCRATCH_W), F32),
                        pltpu.VMEM((2, hb, TK, SCRATCH_W), F32)],
        compiler_params=_cparams(3), name="diff_attn",
    )(dqt, dk, dvt, mk, mvt, tdiag, tsub, tmeta, lam_p, subg_b)


def _sample_attn_kernel(q_ref, ck_ref, cv_ref, nk_ref, nv_ref, tile_ref, lam_ref, subg_ref,
                        y_ref, k_scr, v_scr, *, past, n_new):
    pad = k_scr.shape[1] - past - n_new
    lam = _lambda(lam_ref)
    for h in range(DIFF_HEADS):
        hs = slice(h * DIFF_DV, (h + 1) * DIFF_DV)
        k_scr[h, 0:past] = ck_ref[0, pl.ds(h, past, stride=DIFF_HEADS), :].astype(BF16)
        v_scr[h, 0:past] = cv_ref[0, pl.ds(h, past, stride=DIFF_HEADS), :].astype(BF16)
        k_scr[h, past:past + n_new] = nk_ref[:, hs]
        v_scr[h, past:past + n_new] = nv_ref[:, hs]
        k_scr[h, past + n_new:] = jnp.zeros((pad, LANES), BF16)
        v_scr[h, past + n_new:] = jnp.zeros((pad, LANES), BF16)
        qq = _stack_maps(q_ref[:, hs])
        s = lax.dot_general(qq, k_scr[h], NT_DIMS, preferred_element_type=F32)
        s = (s.reshape(2, n_new, s.shape[-1]) + tile_ref[h][None]).reshape(2 * n_new, s.shape[-1])
        m = jnp.max(s, axis=1, keepdims=True)
        p = jnp.exp2(s - m)
        l = jnp.sum(p, axis=1, keepdims=True)
        o_all = jnp.dot(p.astype(BF16), v_scr[h], preferred_element_type=F32) / l
        y_ref[:, hs] = _mix_and_norm(o_all, lam, subg_ref[...], n_new)


def _sample_attn(dq, ck, cv, nk, nv, tile, lam_p, subg, n_seq, past, n_new):
    kpad = tile.shape[-1]
    row = pl.BlockSpec((n_new, DIFF_W), lambda b: (b, 0))
    cache = pl.BlockSpec((1, past * DIFF_HEADS, DIFF_DV), lambda b: (b, 0, 0))
    return pl.pallas_call(
        functools.partial(_sample_attn_kernel, past=past, n_new=n_new),
        grid=(n_seq,),
        in_specs=[row, cache, cache, row, row, _resident(tile.shape),
                  _resident((4, DIFF_DH)), _resident((1, DIFF_DV))],
        out_specs=row,
        out_shape=jax.ShapeDtypeStruct((n_seq * n_new, DIFF_W), BF16),
        scratch_shapes=[pltpu.VMEM((DIFF_HEADS, kpad, LANES), BF16),
                        pltpu.VMEM((DIFF_HEADS, kpad, LANES), BF16)],
        compiler_params=_cparams(1), name="sample_attn",
    )(dq, ck, cv, nk, nv, tile, lam_p, subg)


def _tail_kernel(h_ref, yr_ref, yd_ref, gr_ref, gd_ref, wrb_ref, wdb_ref, wo_ref, n2_ref,
                 wup_ref, wdn_ref, nf_ref, out_ref):
    a = jnp.dot(yr_ref[...], wrb_ref[...], preferred_element_type=F32)
    b = jnp.dot(yd_ref[...], wdb_ref[...], preferred_element_type=F32)
    merged = jax.nn.sigmoid(gr_ref[...]) * a + jax.nn.sigmoid(gd_ref[...]) * b
    h = h_ref[...] + jnp.dot(merged.astype(BF16), wo_ref[...], preferred_element_type=F32)
    xn = h * lax.rsqrt(jnp.mean(h * h, axis=-1, keepdims=True) + NORM_EPS) * n2_ref[...]
    xn = xn.astype(BF16)
    acc = jnp.zeros_like(h)
    for lo in range(0, D_FF, FF_CHUNK):
        n = min(FF_CHUNK, D_FF - lo)
        gate = jnp.dot(xn, wup_ref[:, lo:lo + n], preferred_element_type=F32)
        up = jnp.dot(xn, wup_ref[:, D_FF + lo:D_FF + lo + n], preferred_element_type=F32)
        act = (gate * jax.nn.sigmoid(gate) * up).astype(BF16)
        acc = acc + jnp.dot(act, wdn_ref[lo:lo + n, :], preferred_element_type=F32)
    h = h + acc
    out_ref[...] = h * lax.rsqrt(jnp.mean(h * h, axis=-1, keepdims=True) + NORM_EPS) * nf_ref[...]


def _tail(h2d, y_ret, y_diff, gr, gd, wrb, wdb, wo, n2, wup, wdn, nf, tm):
    rows = h2d.shape[0]
    row = pl.BlockSpec((tm, D_MODEL), lambda i: (i, 0))
    return pl.pallas_call(
        _tail_kernel, grid=(rows // tm,),
        in_specs=[row, row, row, row, row,
                  _resident(wrb.shape), _resident(wdb.shape), _resident(wo.shape),
                  _resident(n2.shape), _resident(wup.shape), _resident(wdn.shape),
                  _resident(nf.shape)],
        out_specs=row,
        out_shape=jax.ShapeDtypeStruct((rows, D_MODEL), F32),
        compiler_params=_cparams(1), name="tail",
    )(h2d, y_ret, y_diff, gr, gd, wrb, wdb, wo, n2, wup, wdn, nf)


def kernel(x_prompt, x_sample, cache_k, cache_v, state_ret, meta_tokens, rel_bias, norm1_g, w_in,
           lambda_q1, lambda_k1, lambda_q2, lambda_k2, diff_subln_g, w_ret_branch, w_diff_branch,
           w_o, norm2_g, w_ffn_up, w_ffn_down, normf_g):
    assert w_in.shape[0] == 1, "single-layer step only"
    bsz, seq, _ = x_prompt.shape
    dbsz, dseq, _ = x_sample.shape
    past = cache_k.shape[2]
    assert TQ == TK == TM_PROJ and TQ % CHUNK == 0 and DIFF_HEADS % HEADS_PER_STEP == 0
    assert seq % TQ == 0
    assert dseq <= CHUNK and past % CHUNK == 0 and meta_tokens.shape[0] == N_META

    w_in_bf = w_in[0].astype(BF16)
    g1 = norm1_g[0][None, :]
    lam_p = jnp.stack([lambda_q1[0], lambda_k1[0], lambda_q2[0], lambda_k2[0]])
    subg = diff_subln_g[0][None, :]
    wts = (w_ret_branch[0].astype(BF16), w_diff_branch[0].astype(BF16), w_o[0].astype(BF16),
           norm2_g[0][None, :], w_ffn_up[0].astype(BF16), w_ffn_down[0].astype(BF16),
           normf_g[None, :])

    kpad = ((past + dseq + LANES - 1) // LANES) * LANES
    both_maps = lambda ix: np.concatenate([ix.T, ix.T], axis=1)
    idx_diag = both_maps(_bias_idx(np.arange(TQ), np.arange(TK), TK))
    idx_sub = both_maps(_bias_idx(np.arange(TQ) + TK, np.arange(TK), TK))
    idx_meta = np.concatenate(
        [both_maps(_bias_idx(np.arange(TQ) + b * TQ, np.arange(N_META) - N_META, N_META))
         for b in range(2)], axis=0)
    idx_samp = _bias_idx(past + np.arange(dseq), np.arange(kpad), past + dseq)
    t_diag, t_sub, t_meta, t_samp = _bias_tiles(rel_bias, [idx_diag, idx_sub, idx_meta, idx_samp],
                                                [True, True, True, False])
    t_meta = t_meta.reshape(DIFF_HEADS, 2, N_META, 2 * TQ)

    cs_m, sn_m = _rotary_tables(np.arange(-N_META, 0))
    m_out = _in_proj(meta_tokens, g1, w_in_bf, cs_m, sn_m, N_META)
    zero_state = jnp.zeros((1, RET_HEADS, RET_DK, RET_DV), F32)
    _, s_meta = _retention(m_out[0], m_out[1], m_out[2], m_out[3], zero_state, 1, N_META,
                           with_output=False)
    mk = m_out[5]
    mv_aug = jnp.concatenate([m_out[7].reshape(N_META, DIFF_HEADS, DIFF_DV),
                              jnp.ones((N_META, DIFF_HEADS, ONES_ROWS), BF16)], axis=2)
    mvt = mv_aug.reshape(N_META, DIFF_HEADS * DV_AUG).T
    subg_b = jnp.broadcast_to(diff_subln_g[0][:, None], (DIFF_DV, TQ))

    cs_p, sn_p = _rotary_tables(np.arange(seq))
    x2d = x_prompt.reshape(bsz * seq, D_MODEL)
    meta_rows = lambda u: u.reshape(N_META * DIFF_HEADS, DIFF_DV)
    (y_ret, s_fin, dqt, dkb, dvt, gr, gd, k_rows, v_rows) = _prompt_proj(
        x2d, g1, w_in_bf, cs_p, sn_p, s_meta, meta_rows(m_out[6]), meta_rows(m_out[8]), bsz,
        TM_PROJ)
    y_diff = _diff_attn(dqt, dkb, dvt, mk, mvt, t_diag, t_sub, t_meta, lam_p, subg_b, bsz)
    y_prompt = _tail(x2d, y_ret, y_diff, gr, gd, *wts, TM_TAIL).reshape(bsz, seq, D_MODEL)

    cs_s, sn_s = _rotary_tables(np.tile(past + np.arange(dseq), dbsz))
    xs2d = x_sample.reshape(dbsz * dseq, D_MODEL)
    (rq, rk, rv, rg, dq, dkb, dkf_s, dvb, dvf_s, gr, gd) = _in_proj(xs2d, g1, w_in_bf, cs_s, sn_s,
                                                                     dbsz * dseq)
    y_ret, s_samp = _retention(rq, rk, rv, rg, state_ret[0], dbsz, dseq)
    ck = cache_k[0].reshape(dbsz, past * DIFF_HEADS, DIFF_DV)
    cv = cache_v[0].reshape(dbsz, past * DIFF_HEADS, DIFF_DV)
    y_diff = _sample_attn(dq, ck, cv, dkb, dvb, t_samp, lam_p, subg, dbsz, past, dseq)
    y_sample = _tail(xs2d, y_ret, y_diff, gr, gd, *wts, dbsz * dseq).reshape(dbsz, dseq, D_MODEL)

    heads = (DIFF_HEADS, DIFF_DV)
    return (y_prompt, y_sample,
            k_rows.reshape((1, bsz, N_META + seq) + heads),
            v_rows.reshape((1, bsz, N_META + seq) + heads),
            s_fin[None],
            dkf_s.reshape((1, dbsz, dseq) + heads),
            dvf_s.reshape((1, dbsz, dseq) + heads),
            s_samp[None])
```

```python
import functools
import math

import numpy as np
import jax
import jax.numpy as jnp
from jax import lax
from jax.experimental import pallas as pl
from jax.experimental.pallas import tpu as pltpu

F32 = jnp.float32
BF16 = jnp.bfloat16

D_MODEL = 1024
CHUNK = 64
N_META = 16
RET_HEADS = 4
RET_DK = D_MODEL // 8
RET_DV = 2 * RET_DK
RET_QK_W = RET_HEADS * RET_DK
RET_V_W = RET_HEADS * RET_DV
ROPE_BASE = 10000.0
RET_EPS = 1e-6
DIFF_HEADS = 8
DIFF_DH = D_MODEL // 16
DIFF_DV = 2 * DIFF_DH
DIFF_W = DIFF_HEADS * DIFF_DV
DIFF_EPS = 1e-5
N_BUCKETS = 32
MAX_DISTANCE = 128
D_FF = ((8 * D_MODEL + 3 * 256 - 1) // (3 * 256)) * 256
NORM_EPS = 1e-6
NEG_INF = -1e30
LAM_INIT = 0.8 - 0.6 * math.exp(-0.3 * 0)
LOG2E = math.log2(math.e)
Q_SCALE = DIFF_DH ** -0.5 * LOG2E
ONES_ROWS = 16
DV_AUG = DIFF_DV + ONES_ROWS

OFF_RQ = 0
OFF_RK = OFF_RQ + RET_QK_W
OFF_RV = OFF_RK + RET_QK_W
OFF_RG = OFF_RV + RET_V_W
OFF_DQ = OFF_RG + RET_V_W
OFF_DK = OFF_DQ + DIFF_W
OFF_DV = OFF_DK + DIFF_W
OFF_GR = OFF_DV + DIFF_W
OFF_GD = OFF_GR + D_MODEL
W_IN = OFF_GD + D_MODEL

LANES = 128
VMEM_LIMIT_BYTES = 56 * 1024 * 1024
TM_PROJ = 256
TM_TAIL = 256
TQ = 256
TK = 256
HEADS_PER_STEP = 4
FF_CHUNK = 512

NT_DIMS = (((1,), (1,)), ((), ()))
TN_DIMS = (((0,), (0,)), ((), ()))


def _cparams(n_axes):
    return pltpu.CompilerParams(dimension_semantics=("arbitrary",) * n_axes,
                                vmem_limit_bytes=VMEM_LIMIT_BYTES)


def _resident(shape):
    nd = len(shape)
    return pl.BlockSpec(shape, lambda *_: (0,) * nd, pipeline_mode=pl.Buffered(1))


def _t5_bucket_np(rel):
    nb = N_BUCKETS // 2
    max_exact = nb // 2
    ret = np.where(rel > 0, nb, 0)
    n = np.abs(rel)
    nf = np.maximum(n, max_exact).astype(np.float64)
    large = max_exact + (np.log(nf / max_exact) / math.log(MAX_DISTANCE / max_exact)
                         * (nb - max_exact)).astype(np.int32)
    large = np.minimum(large, nb - 1)
    return (ret + np.where(n < max_exact, n, large)).astype(np.int32)


def _bias_idx(qpos, kpos, n_real):
    qpos = np.asarray(qpos)[:, None]
    kpos = np.asarray(kpos)[None, :]
    vis = (np.floor_divide(kpos, CHUNK) <= np.floor_divide(qpos, CHUNK))
    vis = vis & (np.arange(kpos.shape[1])[None, :] < n_real)
    return np.where(vis, _t5_bucket_np(kpos - qpos), -1).astype(np.int32)


def _rotary_tables(pos):
    half = RET_DK // 2
    freq = 1.0 / (ROPE_BASE ** jnp.linspace(0.0, 1.0, half, dtype=F32))
    ang = jnp.asarray(pos, F32)[:, None] * freq[None, :]
    cos, sin = jnp.cos(ang), jnp.sin(ang)
    return jnp.concatenate([cos, cos], axis=1), jnp.concatenate([-sin, sin], axis=1)


def _retention_tables(t):
    gam = 1.0 - 2.0 ** (-5.0 - np.arange(RET_HEADS, dtype=np.float64))
    lg = np.log(gam)[:, None, None]
    n = np.arange(t)[:, None]
    m = np.arange(t)[None, :]
    cn, cm = n // CHUNK, m // CHUNK
    expo = np.where(cm == cn, np.abs(n - m), n - m).astype(np.float64)[None]
    dmat = np.where((cm <= cn)[None], np.exp(lg * expo), 0.0)
    qdec = np.exp(lg[:, :, 0] * (np.arange(t)[None, :] + 1.0))[:, :, None]
    kdec = np.exp(lg[:, :, 0] * (t - 1.0 - np.arange(t)[None, :]))[:, :, None]
    sdec = np.exp(lg[:, 0, 0] * t)
    return (jnp.asarray(dmat, F32), jnp.asarray(qdec, F32), jnp.asarray(kdec, F32),
            tuple(float(s) for s in sdec))


def _bias_kernel(rb_ref, *refs, n_tiles, shifted):
    idx_refs, out_refs = refs[:n_tiles], refs[n_tiles:]
    h = pl.program_id(0)
    far = rb_ref[N_BUCKETS // 2 - 1, h]
    for idx_ref, out_ref, sh in zip(idx_refs, out_refs, shifted):
        idx = idx_ref[...]
        acc = jnp.full(idx.shape, NEG_INF, F32)
        for b in range(N_BUCKETS):
            val = (rb_ref[b, h] - far if sh else rb_ref[b, h]) * LOG2E
            acc = jnp.where(idx == b, val, acc)
        out_ref[0] = acc


def _bias_tiles(rel_bias, idx_list, shifted):
    n = len(idx_list)
    in_specs = [pl.BlockSpec(memory_space=pltpu.SMEM)]
    in_specs += [pl.BlockSpec(ix.shape, lambda h: (0, 0)) for ix in idx_list]
    out_specs = [pl.BlockSpec((1,) + ix.shape, lambda h: (h, 0, 0)) for ix in idx_list]
    out_shape = [jax.ShapeDtypeStruct((DIFF_HEADS,) + ix.shape, F32) for ix in idx_list]
    return pl.pallas_call(
        functools.partial(_bias_kernel, n_tiles=n, shifted=tuple(shifted)),
        grid=(DIFF_HEADS,), in_specs=in_specs, out_specs=out_specs, out_shape=out_shape,
        compiler_params=_cparams(1), name="bias_tiles",
    )(rel_bias, *[jnp.asarray(ix) for ix in idx_list])


def _norm_proj(x_ref, g_ref, w_ref):
    x = x_ref[...]
    xn = x * lax.rsqrt(jnp.mean(x * x, axis=-1, keepdims=True) + NORM_EPS) * g_ref[...]
    xn = xn.astype(BF16)
    return lambda lo, n: jnp.dot(xn, w_ref[:, lo:lo + n], preferred_element_type=F32)


def _rotary(u, cs, sn):
    return u * cs + pltpu.roll(u, RET_DK // 2, 1) * sn


def _retention_head(qf, kf, v, gate, state, dm, qd, kd, sdec, with_output=True):
    y = None
    if with_output:
        s = lax.dot_general(qf.astype(BF16), kf.astype(BF16), NT_DIMS,
                            preferred_element_type=F32) * dm
        o = jnp.dot(s.astype(BF16), v, preferred_element_type=F32)
        o = o + jnp.dot((qf * qd).astype(BF16), state.astype(BF16), preferred_element_type=F32)
        y = o * lax.rsqrt(jnp.mean(o * o, axis=-1, keepdims=True) + RET_EPS)
        y = y * (gate * jax.nn.sigmoid(gate))
    kv = lax.dot_general((kf * kd).astype(BF16), v, TN_DIMS, preferred_element_type=F32)
    return y, sdec * state + kv


def _inproj_kernel(x_ref, g_ref, w_ref, cs_ref, sn_ref,
                   rq_ref, rk_ref, rv_ref, rg_ref, dq_ref, dkb_ref, dkf_ref,
                   dvb_ref, dvf_ref, gr_ref, gd_ref):
    proj = _norm_proj(x_ref, g_ref, w_ref)
    cs = cs_ref[...]
    sn = sn_ref[...]
    uq = proj(OFF_RQ, RET_QK_W)
    uk = proj(OFF_RK, RET_QK_W)
    for hh in range(RET_HEADS):
        sl = slice(hh * RET_DK, (hh + 1) * RET_DK)
        rq_ref[:, sl] = (_rotary(uq[:, sl], cs, sn) * (RET_DK ** -0.5)).astype(BF16)
        rk_ref[:, sl] = _rotary(uk[:, sl], cs, sn).astype(BF16)
    rv_ref[...] = proj(OFF_RV, RET_V_W).astype(BF16)
    rg_ref[...] = proj(OFF_RG, RET_V_W)
    dq_ref[...] = (proj(OFF_DQ, DIFF_W) * Q_SCALE).astype(BF16)
    dk = proj(OFF_DK, DIFF_W)
    dkf_ref[...] = dk
    dkb_ref[...] = dk.astype(BF16)
    dv = proj(OFF_DV, DIFF_W)
    dvf_ref[...] = dv
    dvb_ref[...] = dv.astype(BF16)
    gr_ref[...] = proj(OFF_GR, D_MODEL)
    gd_ref[...] = proj(OFF_GD, D_MODEL)


def _in_proj(x2d, g, w_bf, cs, sn, tm):
    rows = x2d.shape[0]
    n_pos = cs.shape[0] // tm
    row = lambda w: pl.BlockSpec((tm, w), lambda i: (i, 0))
    pos = pl.BlockSpec((tm, LANES), lambda i: (i % n_pos, 0))
    outs = [(RET_QK_W, BF16), (RET_QK_W, BF16), (RET_V_W, BF16), (RET_V_W, F32),
            (DIFF_W, BF16), (DIFF_W, BF16), (DIFF_W, F32), (DIFF_W, BF16), (DIFF_W, F32),
            (D_MODEL, F32), (D_MODEL, F32)]
    return pl.pallas_call(
        _inproj_kernel, grid=(rows // tm,),
        in_specs=[row(D_MODEL), _resident((1, D_MODEL)), _resident((D_MODEL, W_IN)), pos, pos],
        out_specs=[row(w) for w, _ in outs],
        out_shape=[jax.ShapeDtypeStruct((rows, w), dt) for w, dt in outs],
        compiler_params=_cparams(1), name="in_proj",
    )(x2d, g, w_bf, cs, sn)


def _prompt_proj_kernel(x_ref, g_ref, w_ref, cs_ref, sn_ref, s0_ref, dm_ref, qd_ref, kd_ref,
                        mk_ref, mv_ref,
                        yret_ref, sfin_ref, dqt_ref, dkb_ref, dvt_ref, gr_ref, gd_ref,
                        krows_ref, vrows_ref,
                        s_scr, stage_scr, row_sem, meta_sem, *, tiles_per_seq, n_steps, sdec):
    i = pl.program_id(0)
    tm = x_ref.shape[0]
    proj = _norm_proj(x_ref, g_ref, w_ref)
    cs = cs_ref[...]
    sn = sn_ref[...]
    slot = i % 2
    stream = i // tiles_per_seq
    rows_out = (krows_ref, vrows_ref)

    def rows_copy(which, step):
        first = (N_META + (step % tiles_per_seq) * tm) * DIFF_HEADS
        return pltpu.make_async_copy(
            stage_scr.at[step % 2, which],
            rows_out[which].at[step // tiles_per_seq, pl.ds(first, tm * DIFF_HEADS), :],
            row_sem.at[step % 2, which])

    def meta_copy(which):
        return pltpu.make_async_copy(
            (mk_ref, mv_ref)[which],
            rows_out[which].at[stream, pl.ds(0, N_META * DIFF_HEADS), :], meta_sem.at[which])

    @pl.when(i % tiles_per_seq == 0)
    def _():
        s_scr[...] = s0_ref[0]
        for which in range(2):
            meta_copy(which).start()

    @pl.when(i >= 2)
    def _():
        for which in range(2):
            rows_copy(which, i - 2).wait()

    uq = proj(OFF_RQ, RET_QK_W)
    uk = proj(OFF_RK, RET_QK_W)
    rv = proj(OFF_RV, RET_V_W).astype(BF16)
    rg = proj(OFF_RG, RET_V_W)
    for hh in range(RET_HEADS):
        qs = slice(hh * RET_DK, (hh + 1) * RET_DK)
        vs = slice(hh * RET_DV, (hh + 1) * RET_DV)
        y, state = _retention_head(_rotary(uq[:, qs], cs, sn) * (RET_DK ** -0.5),
                                   _rotary(uk[:, qs], cs, sn), rv[:, vs], rg[:, vs],
                                   s_scr[hh], dm_ref[hh], qd_ref[hh], kd_ref[hh], sdec[hh])
        yret_ref[:, vs] = y.astype(BF16)
        s_scr[hh] = state

    @pl.when(i % tiles_per_seq == tiles_per_seq - 1)
    def _():
        sfin_ref[0] = s_scr[...]

    dqt_ref[0] = (proj(OFF_DQ, DIFF_W) * Q_SCALE).T.astype(BF16)
    dk = proj(OFF_DK, DIFF_W)
    dkb_ref[...] = dk.astype(BF16)
    dv = proj(OFF_DV, DIFF_W)
    dvt = dv.T.astype(BF16)
    ones = jnp.ones((ONES_ROWS, dvt.shape[1]), BF16)
    for hh in range(DIFF_HEADS):
        hs = slice(hh * DIFF_DV, (hh + 1) * DIFF_DV)
        dvt_ref[0, hh * DV_AUG:hh * DV_AUG + DIFF_DV, :] = dvt[hs]
        dvt_ref[0, hh * DV_AUG + DIFF_DV:(hh + 1) * DV_AUG, :] = ones
        head_rows = pl.ds(hh, tm, stride=DIFF_HEADS)
        stage_scr[slot, 0, head_rows, :] = dk[:, hs]
        stage_scr[slot, 1, head_rows, :] = dv[:, hs]
    for which in range(2):
        rows_copy(which, i).start()
    gr_ref[...] = proj(OFF_GR, D_MODEL)
    gd_ref[...] = proj(OFF_GD, D_MODEL)

    @pl.when(i % tiles_per_seq == 0)
    def _():
        for which in range(2):
            meta_copy(which).wait()

    @pl.when(i == n_steps - 1)
    def _():
        for which in range(2):
            if n_steps >= 2:
                rows_copy(which, i - 1).wait()
            rows_copy(which, i).wait()


def _prompt_proj(x2d, g, w_bf, cs, sn, s0, mk_rows, mv_rows, n_seq, tm):
    rows = x2d.shape[0]
    seq = rows // n_seq
    tps = seq // tm
    dmat, qdec, kdec, sdec = _retention_tables(tm)
    row = lambda w: pl.BlockSpec((tm, w), lambda i: (i, 0))
    pos = pl.BlockSpec((tm, LANES), lambda i: (i % tps, 0))
    tposed = lambda w: pl.BlockSpec((1, w, tm), lambda i: (i, 0, 0))
    hbm = pl.BlockSpec(memory_space=pl.ANY)
    s_shape = (1, RET_HEADS, RET_DK, RET_DV)
    wv = DIFF_HEADS * DV_AUG
    rows_shape = jax.ShapeDtypeStruct((n_seq, (N_META + seq) * DIFF_HEADS, DIFF_DV), F32)
    return pl.pallas_call(
        functools.partial(_prompt_proj_kernel, tiles_per_seq=tps, n_steps=rows // tm, sdec=sdec),
        grid=(rows // tm,),
        in_specs=[row(D_MODEL), _resident((1, D_MODEL)), _resident((D_MODEL, W_IN)), pos, pos,
                  _resident(s_shape), _resident(dmat.shape), _resident(qdec.shape),
                  _resident(kdec.shape), _resident(mk_rows.shape), _resident(mv_rows.shape)],
        out_specs=[row(RET_V_W), pl.BlockSpec(s_shape, lambda i: (i // tps, 0, 0, 0)),
                   tposed(DIFF_W), row(DIFF_W), tposed(wv), row(D_MODEL), row(D_MODEL),
                   hbm, hbm],
        out_shape=[jax.ShapeDtypeStruct((rows, RET_V_W), BF16),
                   jax.ShapeDtypeStruct((n_seq,) + s_shape[1:], F32),
                   jax.ShapeDtypeStruct((rows // tm, DIFF_W, tm), BF16),
                   jax.ShapeDtypeStruct((rows, DIFF_W), BF16),
                   jax.ShapeDtypeStruct((rows // tm, wv, tm), BF16),
                   jax.ShapeDtypeStruct((rows, D_MODEL), F32),
                   jax.ShapeDtypeStruct((rows, D_MODEL), F32),
                   rows_shape, rows_shape],
        scratch_shapes=[pltpu.VMEM(s_shape[1:], F32),
                        pltpu.VMEM((2, 2, tm * DIFF_HEADS, DIFF_DV), F32),
                        pltpu.SemaphoreType.DMA((2, 2)), pltpu.SemaphoreType.DMA((2,))],
        compiler_params=_cparams(1), name="prompt_proj",
    )(x2d, g, w_bf, cs, sn, s0, dmat, qdec, kdec, mk_rows, mv_rows)


def _retention_kernel(rq_ref, rk_ref, rv_ref, rg_ref, s0_ref, dm_ref, qd_ref, kd_ref,
                      y_ref, sfin_ref, s_scr, *, sdec, with_output):
    t = pl.program_id(1)

    @pl.when(t == 0)
    def _():
        s_scr[...] = s0_ref[0]

    for hh in range(RET_HEADS):
        qs = slice(hh * RET_DK, (hh + 1) * RET_DK)
        vs = slice(hh * RET_DV, (hh + 1) * RET_DV)
        y, state = _retention_head(rq_ref[:, qs].astype(F32), rk_ref[:, qs].astype(F32),
                                   rv_ref[:, vs], rg_ref[:, vs], s_scr[hh], dm_ref[hh],
                                   qd_ref[hh], kd_ref[hh], sdec[hh], with_output)
        if with_output:
            y_ref[:, vs] = y.astype(BF16)
        else:
            y_ref[:, vs] = jnp.zeros((y_ref.shape[0], RET_DV), BF16)
        s_scr[hh] = state

    @pl.when(t == pl.num_programs(1) - 1)
    def _():
        sfin_ref[0] = s_scr[...]


def _retention(rq, rk, rv, rg, s0, n_seq, t, with_output=True):
    rows = rq.shape[0]
    nt = rows // n_seq // t
    dmat, qdec, kdec, sdec = _retention_tables(t)
    row = lambda w: pl.BlockSpec((t, w), lambda b, i: (b * nt + i, 0))
    s_shape = (1, RET_HEADS, RET_DK, RET_DV)
    if s0.shape[0] == 1:
        s0_spec = pl.BlockSpec(s_shape, lambda b, i: (0, 0, 0, 0))
    else:
        s0_spec = pl.BlockSpec(s_shape, lambda b, i: (b, 0, 0, 0))
    y, sfin = pl.pallas_call(
        functools.partial(_retention_kernel, sdec=sdec, with_output=with_output),
        grid=(n_seq, nt),
        in_specs=[row(RET_QK_W), row(RET_QK_W), row(RET_V_W), row(RET_V_W), s0_spec,
                  _resident(dmat.shape), _resident(qdec.shape), _resident(kdec.shape)],
        out_specs=[row(RET_V_W), pl.BlockSpec(s_shape, lambda b, i: (b, 0, 0, 0))],
        out_shape=[jax.ShapeDtypeStruct((rows, RET_V_W), BF16),
                   jax.ShapeDtypeStruct((n_seq,) + s_shape[1:], F32)],
        scratch_shapes=[pltpu.VMEM(s_shape[1:], F32)],
        compiler_params=_cparams(2), name="retention",
    )(rq, rk, rv, rg, s0, dmat, qdec, kdec)
    return y, sfin


def _lambda(lam_ref):
    lp = lam_ref[...]
    a = jnp.exp(jnp.sum(lp[0:1] * lp[1:2], axis=-1, keepdims=True))
    b = jnp.exp(jnp.sum(lp[2:3] * lp[3:4], axis=-1, keepdims=True))
    return a - b + LAM_INIT


def _stack_maps(q):
    qf = q.astype(F32)
    lane = lax.broadcasted_iota(jnp.int32, q.shape, 1)
    return jnp.concatenate([jnp.where(lane < DIFF_DH, qf, 0.0),
                            jnp.where(lane < DIFF_DH, 0.0, qf)], axis=0).astype(BF16)


def _mix_and_norm(o_all, lam, subg, n):
    o = o_all[:n] - lam * o_all[n:]
    y = o * lax.rsqrt(jnp.mean(o * o, axis=-1, keepdims=True) + DIFF_EPS) * subg
    return (y * (1.0 - LAM_INIT)).astype(BF16)


def _diff_attn_kernel(qt_ref, k_ref, vt_ref, mk_ref, mvt_ref, tdiag_ref, tsub_ref, tmeta_ref,
                      lam_ref, subg_ref, y_ref, qq_scr, m_scr, acc_scr, s_scr):
    i = pl.program_id(2)
    heads = range(HEADS_PER_STEP)
    hs = lambda h: slice(h * DIFF_DV, (h + 1) * DIFF_DV)
    vs = lambda h: slice(h * DV_AUG, (h + 1) * DV_AUG)

    for h in heads:
        qt = qt_ref[0, hs(h), :].astype(F32)
        row = lax.broadcasted_iota(jnp.int32, qt.shape, 0)
        qq_scr[h] = jnp.concatenate([jnp.where(row < DIFF_DH, qt, 0.0),
                                     jnp.where(row < DIFF_DH, 0.0, qt)], axis=1).astype(BF16)

    def scores(h, k_blk, tile):
        s = jnp.dot(k_blk, qq_scr[h], preferred_element_type=F32)
        return s if tile is None else s + tile

    for h in heads:
        m_scr[h] = jnp.full((1, 2 * TQ), NEG_INF, F32)
        acc_scr[h] = jnp.zeros((DV_AUG, 2 * TQ), F32)

    def meta_state():
        out = []
        for h in heads:
            s = scores(h, mk_ref[:, hs(h)], tmeta_ref[h, jnp.minimum(i, 1)])
            m0 = jnp.max(s, axis=0, keepdims=True)
            out.append((m0, jnp.dot(mvt_ref[vs(h), :], jnp.exp2(s - m0).astype(BF16),
                                    preferred_element_type=F32)))
        return out

    def qk(j):
        off = pl.multiple_of(j * TK, TK)
        return tuple(scores(h, k_ref[pl.ds(off, TK), hs(h)], None) for h in heads)

    def consume(j, s):
        m_prev = [m_scr[h] for h in heads]
        m_new = [jnp.maximum(m_prev[h], jnp.max(s[h], axis=0, keepdims=True)) for h in heads]
        p = [jnp.exp2(s[h] - m_new[h]).astype(BF16) for h in heads]
        alpha = [jnp.exp2(m_prev[h] - m_new[h]) for h in heads]
        for h in heads:
            m_scr[h] = m_new[h]
        pv = [jnp.dot(vt_ref[j, vs(h), :], p[h], preferred_element_type=F32) for h in heads]
        for h in heads:
            acc_scr[h] = alpha[h] * acc_scr[h] + pv[h]

    def qk_to_scratch(j, buf, tiles=None):
        off = pl.multiple_of(j * TK, TK)
        for h in heads:
            s_scr[buf, h] = scores(h, k_ref[pl.ds(off, TK), hs(h)],
                                   None if tiles is None else tiles[h])

    def from_scratch(buf):
        return [s_scr[buf, h] for h in heads]

    def pair_steps(j):
        qk_to_scratch(j + 1, 1)
        consume(j, from_scratch(0))
        qk_to_scratch(j + 2, 0)
        consume(j + 1, from_scratch(1))

    def far_quad(t, carry):
        pair_steps(4 * t)
        pair_steps(4 * t + 2)
        return carry

    def far_pair(t, carry):
        pair_steps(2 * t)
        return carry

    j_sub = jnp.maximum(i - 1, 0)
    n_pairs = j_sub // 2
    n_quads = n_pairs // 2
    qk_to_scratch(0, 0)
    lax.fori_loop(0, n_quads, far_quad, 0)
    lax.fori_loop(2 * n_quads, n_pairs, far_pair, 0)

    def finish(meta):
        lam = _lambda(lam_ref)
        for h in heads:
            m_meta, acc_meta = meta[h]
            m_main = m_scr[h]
            m = jnp.maximum(m_main, m_meta)
            acc = jnp.exp2(m_main - m) * acc_scr[h] + jnp.exp2(m_meta - m) * acc_meta
            o_all = acc[:DIFF_DV] / acc[DIFF_DV:DIFF_DV + 1]
            o = o_all[:, :TQ] - lam * o_all[:, TQ:]
            y = o * lax.rsqrt(jnp.mean(o * o, axis=0, keepdims=True) + DIFF_EPS) * subg_ref[...]
            y_ref[:, hs(h)] = (y * (1.0 - LAM_INIT)).T.astype(BF16)

    @pl.when(j_sub % 2 == 1)
    def _():
        meta = meta_state()
        qk_to_scratch(j_sub, 1, tsub_ref)
        consume(j_sub - 1, from_scratch(0))
        qk_to_scratch(i, 0, tdiag_ref)
        consume(j_sub, from_scratch(1))
        consume(i, from_scratch(0))
        finish(meta)

    @pl.when(j_sub % 2 == 0)
    def _():
        meta = meta_state()
        qk_to_scratch(i, 1, tdiag_ref)
        pen = jnp.where(i == 0, 3.0 * NEG_INF, 0.0)
        consume(j_sub, [s_scr[0, h] + (tsub_ref[h] + pen) for h in heads])
        consume(i, from_scratch(1))
        finish(meta)


def _diff_attn(dqt, dk, dvt, mk, mvt, tdiag, tsub, tmeta, lam_p, subg_b, n_seq):
    rows = dk.shape[0]
    seq = rows // n_seq
    nq = seq // TQ
    hb = HEADS_PER_STEP
    wid = hb * DIFF_DV
    wid_v = hb * DV_AUG
    tile3 = lambda b, g, i: (g, 0, 0)
    slow = lambda shape, imap: pl.BlockSpec(shape, imap, pipeline_mode=pl.Buffered(1))
    return pl.pallas_call(
        _diff_attn_kernel, grid=(n_seq, DIFF_HEADS // hb, nq),
        in_specs=[pl.BlockSpec((1, wid, TQ), lambda b, g, i: (b * nq + i, g, 0)),
                  slow((seq, wid), lambda b, g, i: (b, g)),
                  slow((nq, wid_v, TK), lambda b, g, i: (b, g, 0)),
                  slow((N_META, wid), lambda b, g, i: (0, g)),
                  slow((wid_v, N_META), lambda b, g, i: (g, 0)),
                  slow((hb, TK, 2 * TQ), tile3),
                  slow((hb, TK, 2 * TQ), tile3),
                  slow((hb, 2, N_META, 2 * TQ), lambda b, g, i: (g, 0, 0, 0)),
                  slow((4, DIFF_DH), lambda b, g, i: (0, 0)),
                  slow((DIFF_DV, TQ), lambda b, g, i: (0, 0))],
        out_specs=pl.BlockSpec((TQ, wid), lambda b, g, i: (b * nq + i, g)),
        out_shape=jax.ShapeDtypeStruct((rows, DIFF_W), BF16),
        scratch_shapes=[pltpu.VMEM((hb, DIFF_DV, 2 * TQ), BF16),
                        pltpu.VMEM((hb, 1, 2 * TQ), F32),
                        pltpu.VMEM((hb, DV_AUG, 2 * TQ), F32),
                        pltpu.VMEM((2, hb, TK, 2 * TQ), F32)],
        compiler_params=_cparams(3), name="diff_attn",
    )(dqt, dk, dvt, mk, mvt, tdiag, tsub, tmeta, lam_p, subg_b)


def _sample_attn_kernel(q_ref, ck_ref, cv_ref, nk_ref, nv_ref, tile_ref, lam_ref, subg_ref,
                        y_ref, k_scr, v_scr, *, past, n_new):
    pad = k_scr.shape[1] - past - n_new
    lam = _lambda(lam_ref)
    for h in range(DIFF_HEADS):
        hs = slice(h * DIFF_DV, (h + 1) * DIFF_DV)
        k_scr[h, 0:past] = ck_ref[0, pl.ds(h, past, stride=DIFF_HEADS), :].astype(BF16)
        v_scr[h, 0:past] = cv_ref[0, pl.ds(h, past, stride=DIFF_HEADS), :].astype(BF16)
        k_scr[h, past:past + n_new] = nk_ref[:, hs]
        v_scr[h, past:past + n_new] = nv_ref[:, hs]
        k_scr[h, past + n_new:] = jnp.zeros((pad, LANES), BF16)
        v_scr[h, past + n_new:] = jnp.zeros((pad, LANES), BF16)
        qq = _stack_maps(q_ref[:, hs])
        s = lax.dot_general(qq, k_scr[h], NT_DIMS, preferred_element_type=F32)
        s = (s.reshape(2, n_new, s.shape[-1]) + tile_ref[h][None]).reshape(2 * n_new, s.shape[-1])
        m = jnp.max(s, axis=1, keepdims=True)
        p = jnp.exp2(s - m)
        l = jnp.sum(p, axis=1, keepdims=True)
        o_all = jnp.dot(p.astype(BF16), v_scr[h], preferred_element_type=F32) / l
        y_ref[:, hs] = _mix_and_norm(o_all, lam, subg_ref[...], n_new)


def _sample_attn(dq, ck, cv, nk, nv, tile, lam_p, subg, n_seq, past, n_new):
    kpad = tile.shape[-1]
    row = pl.BlockSpec((n_new, DIFF_W), lambda b: (b, 0))
    cache = pl.BlockSpec((1, past * DIFF_HEADS, DIFF_DV), lambda b: (b, 0, 0))
    return pl.pallas_call(
        functools.partial(_sample_attn_kernel, past=past, n_new=n_new),
        grid=(n_seq,),
        in_specs=[row, cache, cache, row, row, _resident(tile.shape),
                  _resident((4, DIFF_DH)), _resident((1, DIFF_DV))],
        out_specs=row,
        out_shape=jax.ShapeDtypeStruct((n_seq * n_new, DIFF_W), BF16),
        scratch_shapes=[pltpu.VMEM((DIFF_HEADS, kpad, LANES), BF16),
                        pltpu.VMEM((DIFF_HEADS, kpad, LANES), BF16)],
        compiler_params=_cparams(1), name="sample_attn",
    )(dq, ck, cv, nk, nv, tile, lam_p, subg)


def _tail_kernel(h_ref, yr_ref, yd_ref, gr_ref, gd_ref, wrb_ref, wdb_ref, wo_ref, n2_ref,
                 wup_ref, wdn_ref, nf_ref, out_ref):
    a = jnp.dot(yr_ref[...], wrb_ref[...], preferred_element_type=F32)
    b = jnp.dot(yd_ref[...], wdb_ref[...], preferred_element_type=F32)
    merged = jax.nn.sigmoid(gr_ref[...]) * a + jax.nn.sigmoid(gd_ref[...]) * b
    h = h_ref[...] + jnp.dot(merged.astype(BF16), wo_ref[...], preferred_element_type=F32)
    xn = h * lax.rsqrt(jnp.mean(h * h, axis=-1, keepdims=True) + NORM_EPS) * n2_ref[...]
    xn = xn.astype(BF16)
    acc = jnp.zeros_like(h)
    for lo in range(0, D_FF, FF_CHUNK):
        n = min(FF_CHUNK, D_FF - lo)
        gate = jnp.dot(xn, wup_ref[:, lo:lo + n], preferred_element_type=F32)
        up = jnp.dot(xn, wup_ref[:, D_FF + lo:D_FF + lo + n], preferred_element_type=F32)
        act = (gate * jax.nn.sigmoid(gate) * up).astype(BF16)
        acc = acc + jnp.dot(act, wdn_ref[lo:lo + n, :], preferred_element_type=F32)
    h = h + acc
    out_ref[...] = h * lax.rsqrt(jnp.mean(h * h, axis=-1, keepdims=True) + NORM_EPS) * nf_ref[...]


def _tail(h2d, y_ret, y_diff, gr, gd, wrb, wdb, wo, n2, wup, wdn, nf, tm):
    rows = h2d.shape[0]
    row = pl.BlockSpec((tm, D_MODEL), lambda i: (i, 0))
    return pl.pallas_call(
        _tail_kernel, grid=(rows // tm,),
        in_specs=[row, row, row, row, row,
                  _resident(wrb.shape), _resident(wdb.shape), _resident(wo.shape),
                  _resident(n2.shape), _resident(wup.shape), _resident(wdn.shape),
                  _resident(nf.shape)],
        out_specs=row,
        out_shape=jax.ShapeDtypeStruct((rows, D_MODEL), F32),
        compiler_params=_cparams(1), name="tail",
    )(h2d, y_ret, y_diff, gr, gd, wrb, wdb, wo, n2, wup, wdn, nf)


def kernel(x_prompt, x_sample, cache_k, cache_v, state_ret, meta_tokens, rel_bias, norm1_g, w_in,
           lambda_q1, lambda_k1, lambda_q2, lambda_k2, diff_subln_g, w_ret_branch, w_diff_branch,
           w_o, norm2_g, w_ffn_up, w_ffn_down, normf_g):
    assert w_in.shape[0] == 1, "single-layer step only"
    bsz, seq, _ = x_prompt.shape
    dbsz, dseq, _ = x_sample.shape
    past = cache_k.shape[2]
    assert TQ == TK == TM_PROJ and TQ % CHUNK == 0 and DIFF_HEADS % HEADS_PER_STEP == 0
    assert seq % TQ == 0
    assert dseq <= CHUNK and past % CHUNK == 0 and meta_tokens.shape[0] == N_META

    w_in_bf = w_in[0].astype(BF16)
    g1 = norm1_g[0][None, :]
    lam_p = jnp.stack([lambda_q1[0], lambda_k1[0], lambda_q2[0], lambda_k2[0]])
    subg = diff_subln_g[0][None, :]
    wts = (w_ret_branch[0].astype(BF16), w_diff_branch[0].astype(BF16), w_o[0].astype(BF16),
           norm2_g[0][None, :], w_ffn_up[0].astype(BF16), w_ffn_down[0].astype(BF16),
           normf_g[None, :])

    kpad = ((past + dseq + LANES - 1) // LANES) * LANES
    both_maps = lambda ix: np.concatenate([ix.T, ix.T], axis=1)
    idx_diag = both_maps(_bias_idx(np.arange(TQ), np.arange(TK), TK))
    idx_sub = both_maps(_bias_idx(np.arange(TQ) + TK, np.arange(TK), TK))
    idx_meta = np.concatenate(
        [both_maps(_bias_idx(np.arange(TQ) + b * TQ, np.arange(N_META) - N_META, N_META))
         for b in range(2)], axis=0)
    idx_samp = _bias_idx(past + np.arange(dseq), np.arange(kpad), past + dseq)
    t_diag, t_sub, t_meta, t_samp = _bias_tiles(rel_bias, [idx_diag, idx_sub, idx_meta, idx_samp],
                                                [True, True, True, False])
    t_meta = t_meta.reshape(DIFF_HEADS, 2, N_META, 2 * TQ)

    cs_m, sn_m = _rotary_tables(np.arange(-N_META, 0))
    m_out = _in_proj(meta_tokens, g1, w_in_bf, cs_m, sn_m, N_META)
    zero_state = jnp.zeros((1, RET_HEADS, RET_DK, RET_DV), F32)
    _, s_meta = _retention(m_out[0], m_out[1], m_out[2], m_out[3], zero_state, 1, N_META,
                           with_output=False)
    mk = m_out[5]
    mv_aug = jnp.concatenate([m_out[7].reshape(N_META, DIFF_HEADS, DIFF_DV),
                              jnp.ones((N_META, DIFF_HEADS, ONES_ROWS), BF16)], axis=2)
    mvt = mv_aug.reshape(N_META, DIFF_HEADS * DV_AUG).T
    subg_b = jnp.broadcast_to(diff_subln_g[0][:, None], (DIFF_DV, TQ))

    cs_p, sn_p = _rotary_tables(np.arange(seq))
    x2d = x_prompt.reshape(bsz * seq, D_MODEL)
    meta_rows = lambda u: u.reshape(N_META * DIFF_HEADS, DIFF_DV)
    (y_ret, s_fin, dqt, dkb, dvt, gr, gd, k_rows, v_rows) = _prompt_proj(
        x2d, g1, w_in_bf, cs_p, sn_p, s_meta, meta_rows(m_out[6]), meta_rows(m_out[8]), bsz,
        TM_PROJ)
    y_diff = _diff_attn(dqt, dkb, dvt, mk, mvt, t_diag, t_sub, t_meta, lam_p, subg_b, bsz)
    y_prompt = _tail(x2d, y_ret, y_diff, gr, gd, *wts, TM_TAIL).reshape(bsz, seq, D_MODEL)

    cs_s, sn_s = _rotary_tables(np.tile(past + np.arange(dseq), dbsz))
    xs2d = x_sample.reshape(dbsz * dseq, D_MODEL)
    (rq, rk, rv, rg, dq, dkb, dkf_s, dvb, dvf_s, gr, gd) = _in_proj(xs2d, g1, w_in_bf, cs_s, sn_s,
                                                                     dbsz * dseq)
    y_ret, s_samp = _retention(rq, rk, rv, rg, state_ret[0], dbsz, dseq)
    ck = cache_k[0].reshape(dbsz, past * DIFF_HEADS, DIFF_DV)
    cv = cache_v[0].reshape(dbsz, past * DIFF_HEADS, DIFF_DV)
    y_diff = _sample_attn(dq, ck, cv, dkb, dvb, t_samp, lam_p, subg, dbsz, past, dseq)
    y_sample = _tail(xs2d, y_ret, y_diff, gr, gd, *wts, dbsz * dseq).reshape(dbsz, dseq, D_MODEL)

    heads = (DIFF_HEADS, DIFF_DV)
    return (y_prompt, y_sample,
            k_rows.reshape((1, bsz, N_META + seq) + heads),
            v_rows.reshape((1, bsz, N_META + seq) + heads),
            s_fin[None],
            dkf_s.reshape((1, dbsz, dseq) + heads),
            dvf_s.reshape((1, dbsz, dseq) + heads),
            s_samp[None])
```

```python
import functools
import math

import numpy as np
import jax
import jax.numpy as jnp
from jax import lax
from jax.experimental import pallas as pl
from jax.experimental.pallas import tpu as pltpu

F32 = jnp.float32
BF16 = jnp.bfloat16

D_MODEL = 1024
CHUNK = 64
N_META = 16
RET_HEADS = 4
RET_DK = D_MODEL // 8
RET_DV = 2 * RET_DK
RET_QK_W = RET_HEADS * RET_DK
RET_V_W = RET_HEADS * RET_DV
ROPE_BASE = 10000.0
RET_EPS = 1e-6
DIFF_HEADS = 8
DIFF_DH = D_MODEL // 16
DIFF_DV = 2 * DIFF_DH
DIFF_W = DIFF_HEADS * DIFF_DV
DIFF_EPS = 1e-5
N_BUCKETS = 32
MAX_DISTANCE = 128
D_FF = ((8 * D_MODEL + 3 * 256 - 1) // (3 * 256)) * 256
NORM_EPS = 1e-6
NEG_INF = -1e30
LAM_INIT = 0.8 - 0.6 * math.exp(-0.3 * 0)
LOG2E = math.log2(math.e)
Q_SCALE = DIFF_DH ** -0.5 * LOG2E
ONES_ROWS = 16
DV_AUG = DIFF_DV + ONES_ROWS

OFF_RQ = 0
OFF_RK = OFF_RQ + RET_QK_W
OFF_RV = OFF_RK + RET_QK_W
OFF_RG = OFF_RV + RET_V_W
OFF_DQ = OFF_RG + RET_V_W
OFF_DK = OFF_DQ + DIFF_W
OFF_DV = OFF_DK + DIFF_W
OFF_GR = OFF_DV + DIFF_W
OFF_GD = OFF_GR + D_MODEL
W_IN = OFF_GD + D_MODEL

LANES = 128
VMEM_LIMIT_BYTES = 56 * 1024 * 1024
TM_PROJ = 256
TM_TAIL = 256
TQ = 256
TK = 256
HEADS_PER_STEP = 4
FF_CHUNK = 512

NT_DIMS = (((1,), (1,)), ((), ()))
TN_DIMS = (((0,), (0,)), ((), ()))


def _cparams(n_axes):
    return pltpu.CompilerParams(dimension_semantics=("arbitrary",) * n_axes,
                                vmem_limit_bytes=VMEM_LIMIT_BYTES)


def _resident(shape):
    nd = len(shape)
    return pl.BlockSpec(shape, lambda *_: (0,) * nd, pipeline_mode=pl.Buffered(1))


def _t5_bucket_np(rel):
    nb = N_BUCKETS // 2
    max_exact = nb // 2
    ret = np.where(rel > 0, nb, 0)
    n = np.abs(rel)
    nf = np.maximum(n, max_exact).astype(np.float64)
    large = max_exact + (np.log(nf / max_exact) / math.log(MAX_DISTANCE / max_exact)
                         * (nb - max_exact)).astype(np.int32)
    large = np.minimum(large, nb - 1)
    return (ret + np.where(n < max_exact, n, large)).astype(np.int32)


def _bias_idx(qpos, kpos, n_real):
    qpos = np.asarray(qpos)[:, None]
    kpos = np.asarray(kpos)[None, :]
    vis = (np.floor_divide(kpos, CHUNK) <= np.floor_divide(qpos, CHUNK))
    vis = vis & (np.arange(kpos.shape[1])[None, :] < n_real)
    return np.where(vis, _t5_bucket_np(kpos - qpos), -1).astype(np.int32)


def _rotary_tables(pos):
    half = RET_DK // 2
    freq = 1.0 / (ROPE_BASE ** jnp.linspace(0.0, 1.0, half, dtype=F32))
    ang = jnp.asarray(pos, F32)[:, None] * freq[None, :]
    cos, sin = jnp.cos(ang), jnp.sin(ang)
    return jnp.concatenate([cos, cos], axis=1), jnp.concatenate([-sin, sin], axis=1)


def _retention_tables(t):
    gam = 1.0 - 2.0 ** (-5.0 - np.arange(RET_HEADS, dtype=np.float64))
    lg = np.log(gam)[:, None, None]
    n = np.arange(t)[:, None]
    m = np.arange(t)[None, :]
    cn, cm = n // CHUNK, m // CHUNK
    expo = np.where(cm == cn, np.abs(n - m), n - m).astype(np.float64)[None]
    dmat = np.where((cm <= cn)[None], np.exp(lg * expo), 0.0)
    qdec = np.exp(lg[:, :, 0] * (np.arange(t)[None, :] + 1.0))[:, :, None]
    kdec = np.exp(lg[:, :, 0] * (t - 1.0 - np.arange(t)[None, :]))[:, :, None]
    sdec = np.exp(lg[:, 0, 0] * t)
    return (jnp.asarray(dmat, F32), jnp.asarray(qdec, F32), jnp.asarray(kdec, F32),
            tuple(float(s) for s in sdec))


def _bias_kernel(rb_ref, *refs, n_tiles, shifted):
    idx_refs, out_refs = refs[:n_tiles], refs[n_tiles:]
    h = pl.program_id(0)
    far = rb_ref[N_BUCKETS // 2 - 1, h]
    for idx_ref, out_ref, sh in zip(idx_refs, out_refs, shifted):
        idx = idx_ref[...]
        acc = jnp.full(idx.shape, NEG_INF, F32)
        for b in range(N_BUCKETS):
            val = (rb_ref[b, h] - far if sh else rb_ref[b, h]) * LOG2E
            acc = jnp.where(idx == b, val, acc)
        out_ref[0] = acc


def _bias_tiles(rel_bias, idx_list, shifted):
    n = len(idx_list)
    in_specs = [pl.BlockSpec(memory_space=pltpu.SMEM)]
    in_specs += [pl.BlockSpec(ix.shape, lambda h: (0, 0)) for ix in idx_list]
    out_specs = [pl.BlockSpec((1,) + ix.shape, lambda h: (h, 0, 0)) for ix in idx_list]
    out_shape = [jax.ShapeDtypeStruct((DIFF_HEADS,) + ix.shape, F32) for ix in idx_list]
    return pl.pallas_call(
        functools.partial(_bias_kernel, n_tiles=n, shifted=tuple(shifted)),
        grid=(DIFF_HEADS,), in_specs=in_specs, out_specs=out_specs, out_shape=out_shape,
        compiler_params=_cparams(1), name="bias_tiles",
    )(rel_bias, *[jnp.asarray(ix) for ix in idx_list])


def _norm_proj(x_ref, g_ref, w_ref):
    x = x_ref[...]
    xn = x * lax.rsqrt(jnp.mean(x * x, axis=-1, keepdims=True) + NORM_EPS) * g_ref[...]
    xn = xn.astype(BF16)
    return lambda lo, n: jnp.dot(xn, w_ref[:, lo:lo + n], preferred_element_type=F32)


def _rotary(u, cs, sn):
    return u * cs + pltpu.roll(u, RET_DK // 2, 1) * sn


def _retention_head(qf, kf, v, gate, state, dm, qd, kd, sdec, with_output=True):
    y = None
    if with_output:
        s = lax.dot_general(qf.astype(BF16), kf.astype(BF16), NT_DIMS,
                            preferred_element_type=F32) * dm
        o = jnp.dot(s.astype(BF16), v, preferred_element_type=F32)
        o = o + jnp.dot((qf * qd).astype(BF16), state.astype(BF16), preferred_element_type=F32)
        y = o * lax.rsqrt(jnp.mean(o * o, axis=-1, keepdims=True) + RET_EPS)
        y = y * (gate * jax.nn.sigmoid(gate))
    kv = lax.dot_general((kf * kd).astype(BF16), v, TN_DIMS, preferred_element_type=F32)
    return y, sdec * state + kv


def _inproj_kernel(x_ref, g_ref, w_ref, cs_ref, sn_ref,
                   rq_ref, rk_ref, rv_ref, rg_ref, dq_ref, dkb_ref, dkf_ref,
                   dvb_ref, dvf_ref, gr_ref, gd_ref):
    proj = _norm_proj(x_ref, g_ref, w_ref)
    cs = cs_ref[...]
    sn = sn_ref[...]
    uq = proj(OFF_RQ, RET_QK_W)
    uk = proj(OFF_RK, RET_QK_W)
    for hh in range(RET_HEADS):
        sl = slice(hh * RET_DK, (hh + 1) * RET_DK)
        rq_ref[:, sl] = (_rotary(uq[:, sl], cs, sn) * (RET_DK ** -0.5)).astype(BF16)
        rk_ref[:, sl] = _rotary(uk[:, sl], cs, sn).astype(BF16)
    rv_ref[...] = proj(OFF_RV, RET_V_W).astype(BF16)
    rg_ref[...] = proj(OFF_RG, RET_V_W)
    dq_ref[...] = (proj(OFF_DQ, DIFF_W) * Q_SCALE).astype(BF16)
    dk = proj(OFF_DK, DIFF_W)
    dkf_ref[...] = dk
    dkb_ref[...] = dk.astype(BF16)
    dv = proj(OFF_DV, DIFF_W)
    dvf_ref[...] = dv
    dvb_ref[...] = dv.astype(BF16)
    gr_ref[...] = proj(OFF_GR, D_MODEL)
    gd_ref[...] = proj(OFF_GD, D_MODEL)


def _in_proj(x2d, g, w_bf, cs, sn, tm):
    rows = x2d.shape[0]
    n_pos = cs.shape[0] // tm
    row = lambda w: pl.BlockSpec((tm, w), lambda i: (i, 0))
    pos = pl.BlockSpec((tm, LANES), lambda i: (i % n_pos, 0))
    outs = [(RET_QK_W, BF16), (RET_QK_W, BF16), (RET_V_W, BF16), (RET_V_W, F32),
            (DIFF_W, BF16), (DIFF_W, BF16), (DIFF_W, F32), (DIFF_W, BF16), (DIFF_W, F32),
            (D_MODEL, F32), (D_MODEL, F32)]
    return pl.pallas_call(
        _inproj_kernel, grid=(rows // tm,),
        in_specs=[row(D_MODEL), _resident((1, D_MODEL)), _resident((D_MODEL, W_IN)), pos, pos],
        out_specs=[row(w) for w, _ in outs],
        out_shape=[jax.ShapeDtypeStruct((rows, w), dt) for w, dt in outs],
        compiler_params=_cparams(1), name="in_proj",
    )(x2d, g, w_bf, cs, sn)


def _prompt_proj_kernel(x_ref, g_ref, w_ref, cs_ref, sn_ref, s0_ref, dm_ref, qd_ref, kd_ref,
                        mk_ref, mv_ref,
                        yret_ref, sfin_ref, dqt_ref, dkb_ref, dvt_ref, gr_ref, gd_ref,
                        krows_ref, vrows_ref,
                        s_scr, stage_scr, row_sem, meta_sem, *, tiles_per_seq, n_steps, sdec):
    i = pl.program_id(0)
    tm = x_ref.shape[0]
    proj = _norm_proj(x_ref, g_ref, w_ref)
    cs = cs_ref[...]
    sn = sn_ref[...]
    slot = i % 2
    stream = i // tiles_per_seq
    rows_out = (krows_ref, vrows_ref)

    def rows_copy(which, step):
        first = (N_META + (step % tiles_per_seq) * tm) * DIFF_HEADS
        return pltpu.make_async_copy(
            stage_scr.at[step % 2, which],
            rows_out[which].at[step // tiles_per_seq, pl.ds(first, tm * DIFF_HEADS), :],
            row_sem.at[step % 2, which])

    def meta_copy(which):
        return pltpu.make_async_copy(
            (mk_ref, mv_ref)[which],
            rows_out[which].at[stream, pl.ds(0, N_META * DIFF_HEADS), :], meta_sem.at[which])

    @pl.when(i % tiles_per_seq == 0)
    def _():
        s_scr[...] = s0_ref[0]
        for which in range(2):
            meta_copy(which).start()

    @pl.when(i >= 2)
    def _():
        for which in range(2):
            rows_copy(which, i - 2).wait()

    uq = proj(OFF_RQ, RET_QK_W)
    uk = proj(OFF_RK, RET_QK_W)
    rv = proj(OFF_RV, RET_V_W).astype(BF16)
    rg = proj(OFF_RG, RET_V_W)
    for hh in range(RET_HEADS):
        qs = slice(hh * RET_DK, (hh + 1) * RET_DK)
        vs = slice(hh * RET_DV, (hh + 1) * RET_DV)
        y, state = _retention_head(_rotary(uq[:, qs], cs, sn) * (RET_DK ** -0.5),
                                   _rotary(uk[:, qs], cs, sn), rv[:, vs], rg[:, vs],
                                   s_scr[hh], dm_ref[hh], qd_ref[hh], kd_ref[hh], sdec[hh])
        yret_ref[:, vs] = y.astype(BF16)
        s_scr[hh] = state

    @pl.when(i % tiles_per_seq == tiles_per_seq - 1)
    def _():
        sfin_ref[0] = s_scr[...]

    dqt_ref[0] = (proj(OFF_DQ, DIFF_W) * Q_SCALE).T.astype(BF16)
    dk = proj(OFF_DK, DIFF_W)
    dkb_ref[...] = dk.astype(BF16)
    dv = proj(OFF_DV, DIFF_W)
    dvt = dv.T.astype(BF16)
    ones = jnp.ones((ONES_ROWS, dvt.shape[1]), BF16)
    for hh in range(DIFF_HEADS):
        hs = slice(hh * DIFF_DV, (hh + 1) * DIFF_DV)
        dvt_ref[0, hh * DV_AUG:hh * DV_AUG + DIFF_DV, :] = dvt[hs]
        dvt_ref[0, hh * DV_AUG + DIFF_DV:(hh + 1) * DV_AUG, :] = ones
        head_rows = pl.ds(hh, tm, stride=DIFF_HEADS)
        stage_scr[slot, 0, head_rows, :] = dk[:, hs]
        stage_scr[slot, 1, head_rows, :] = dv[:, hs]
    for which in range(2):
        rows_copy(which, i).start()
    gr_ref[...] = proj(OFF_GR, D_MODEL)
    gd_ref[...] = proj(OFF_GD, D_MODEL)

    @pl.when(i % tiles_per_seq == 0)
    def _():
        for which in range(2):
            meta_copy(which).wait()

    @pl.when(i == n_steps - 1)
    def _():
        for which in range(2):
            if n_steps >= 2:
                rows_copy(which, i - 1).wait()
            rows_copy(which, i).wait()


def _prompt_proj(x2d, g, w_bf, cs, sn, s0, mk_rows, mv_rows, n_seq, tm):
    rows = x2d.shape[0]
    seq = rows // n_seq
    tps = seq // tm
    dmat, qdec, kdec, sdec = _retention_tables(tm)
    row = lambda w: pl.BlockSpec((tm, w), lambda i: (i, 0))
    pos = pl.BlockSpec((tm, LANES), lambda i: (i % tps, 0))
    tposed = lambda w: pl.BlockSpec((1, w, tm), lambda i: (i, 0, 0))
    hbm = pl.BlockSpec(memory_space=pl.ANY)
    s_shape = (1, RET_HEADS, RET_DK, RET_DV)
    wv = DIFF_HEADS * DV_AUG
    rows_shape = jax.ShapeDtypeStruct((n_seq, (N_META + seq) * DIFF_HEADS, DIFF_DV), F32)
    return pl.pallas_call(
        functools.partial(_prompt_proj_kernel, tiles_per_seq=tps, n_steps=rows // tm, sdec=sdec),
        grid=(rows // tm,),
        in_specs=[row(D_MODEL), _resident((1, D_MODEL)), _resident((D_MODEL, W_IN)), pos, pos,
                  _resident(s_shape), _resident(dmat.shape), _resident(qdec.shape),
                  _resident(kdec.shape), _resident(mk_rows.shape), _resident(mv_rows.shape)],
        out_specs=[row(RET_V_W), pl.BlockSpec(s_shape, lambda i: (i // tps, 0, 0, 0)),
                   tposed(DIFF_W), row(DIFF_W), tposed(wv), row(D_MODEL), row(D_MODEL),
                   hbm, hbm],
        out_shape=[jax.ShapeDtypeStruct((rows, RET_V_W), BF16),
                   jax.ShapeDtypeStruct((n_seq,) + s_shape[1:], F32),
                   jax.ShapeDtypeStruct((rows // tm, DIFF_W, tm), BF16),
                   jax.ShapeDtypeStruct((rows, DIFF_W), BF16),
                   jax.ShapeDtypeStruct((rows // tm, wv, tm), BF16),
                   jax.ShapeDtypeStruct((rows, D_MODEL), F32),
                   jax.ShapeDtypeStruct((rows, D_MODEL), F32),
                   rows_shape, rows_shape],
        scratch_shapes=[pltpu.VMEM(s_shape[1:], F32),
                        pltpu.VMEM((2, 2, tm * DIFF_HEADS, DIFF_DV), F32),
                        pltpu.SemaphoreType.DMA((2, 2)), pltpu.SemaphoreType.DMA((2,))],
        compiler_params=_cparams(1), name="prompt_proj",
    )(x2d, g, w_bf, cs, sn, s0, dmat, qdec, kdec, mk_rows, mv_rows)


def _retention_kernel(rq_ref, rk_ref, rv_ref, rg_ref, s0_ref, dm_ref, qd_ref, kd_ref,
                      y_ref, sfin_ref, s_scr, *, sdec, with_output):
    t = pl.program_id(1)

    @pl.when(t == 0)
    def _():
        s_scr[...] = s0_ref[0]

    for hh in range(RET_HEADS):
        qs = slice(hh * RET_DK, (hh + 1) * RET_DK)
        vs = slice(hh * RET_DV, (hh + 1) * RET_DV)
        y, state = _retention_head(rq_ref[:, qs].astype(F32), rk_ref[:, qs].astype(F32),
                                   rv_ref[:, vs], rg_ref[:, vs], s_scr[hh], dm_ref[hh],
                                   qd_ref[hh], kd_ref[hh], sdec[hh], with_output)
        if with_output:
            y_ref[:, vs] = y.astype(BF16)
        else:
            y_ref[:, vs] = jnp.zeros((y_ref.shape[0], RET_DV), BF16)
        s_scr[hh] = state

    @pl.when(t == pl.num_programs(1) - 1)
    def _():
        sfin_ref[0] = s_scr[...]


def _retention(rq, rk, rv, rg, s0, n_seq, t, with_output=True):
    rows = rq.shape[0]
    nt = rows // n_seq // t
    dmat, qdec, kdec, sdec = _retention_tables(t)
    row = lambda w: pl.BlockSpec((t, w), lambda b, i: (b * nt + i, 0))
    s_shape = (1, RET_HEADS, RET_DK, RET_DV)
    if s0.shape[0] == 1:
        s0_spec = pl.BlockSpec(s_shape, lambda b, i: (0, 0, 0, 0))
    else:
        s0_spec = pl.BlockSpec(s_shape, lambda b, i: (b, 0, 0, 0))
    y, sfin = pl.pallas_call(
        functools.partial(_retention_kernel, sdec=sdec, with_output=with_output),
        grid=(n_seq, nt),
        in_specs=[row(RET_QK_W), row(RET_QK_W), row(RET_V_W), row(RET_V_W), s0_spec,
                  _resident(dmat.shape), _resident(qdec.shape), _resident(kdec.shape)],
        out_specs=[row(RET_V_W), pl.BlockSpec(s_shape, lambda b, i: (b, 0, 0, 0))],
        out_shape=[jax.ShapeDtypeStruct((rows, RET_V_W), BF16),
                   jax.ShapeDtypeStruct((n_seq,) + s_shape[1:], F32)],
        scratch_shapes=[pltpu.VMEM(s_shape[1:], F32)],
        compiler_params=_cparams(2), name="retention",
    )(rq, rk, rv, rg, s0, dmat, qdec, kdec)
    return y, sfin


def _lambda(lam_ref):
    lp = lam_ref[...]
    a = jnp.exp(jnp.sum(lp[0:1] * lp[1:2], axis=-1, keepdims=True))
    b = jnp.exp(jnp.sum(lp[2:3] * lp[3:4], axis=-1, keepdims=True))
    return a - b + LAM_INIT


def _stack_maps(q):
    qf = q.astype(F32)
    lane = lax.broadcasted_iota(jnp.int32, q.shape, 1)
    return jnp.concatenate([jnp.where(lane < DIFF_DH, qf, 0.0),
                            jnp.where(lane < DIFF_DH, 0.0, qf)], axis=0).astype(BF16)


def _mix_and_norm(o_all, lam, subg, n):
    o = o_all[:n] - lam * o_all[n:]
    y = o * lax.rsqrt(jnp.mean(o * o, axis=-1, keepdims=True) + DIFF_EPS) * subg
    return (y * (1.0 - LAM_INIT)).astype(BF16)


def _diff_attn_kernel(qt_ref, k_ref, vt_ref, mk_ref, mvt_ref, tdiag_ref, tsub_ref, tmeta_ref,
                      lam_ref, subg_ref, y_ref, qq_scr, m_scr, acc_scr, s_scr):
    i = pl.program_id(2)
    heads = range(HEADS_PER_STEP)
    hs = lambda h: slice(h * DIFF_DV, (h + 1) * DIFF_DV)
    vs = lambda h: slice(h * DV_AUG, (h + 1) * DV_AUG)

    for h in heads:
        qt = qt_ref[0, hs(h), :].astype(F32)
        row = lax.broadcasted_iota(jnp.int32, qt.shape, 0)
        qq_scr[h] = jnp.concatenate([jnp.where(row < DIFF_DH, qt, 0.0),
                                     jnp.where(row < DIFF_DH, 0.0, qt)], axis=1).astype(BF16)

    def scores(h, k_blk, tile):
        s = jnp.dot(k_blk, qq_scr[h], preferred_element_type=F32)
        return s if tile is None else s + tile

    for h in heads:
        m_scr[h] = jnp.full((1, 2 * TQ), NEG_INF, F32)
        acc_scr[h] = jnp.zeros((DV_AUG, 2 * TQ), F32)

    def meta_state():
        out = []
        for h in heads:
            s = scores(h, mk_ref[:, hs(h)], tmeta_ref[h, jnp.minimum(i, 1)])
            m0 = jnp.max(s, axis=0, keepdims=True)
            out.append((m0, jnp.dot(mvt_ref[vs(h), :], jnp.exp2(s - m0).astype(BF16),
                                    preferred_element_type=F32)))
        return out

    def qk(j):
        off = pl.multiple_of(j * TK, TK)
        return tuple(scores(h, k_ref[pl.ds(off, TK), hs(h)], None) for h in heads)

    def consume(j, s):
        m_prev = [m_scr[h] for h in heads]
        m_new = [jnp.maximum(m_prev[h], jnp.max(s[h], axis=0, keepdims=True)) for h in heads]
        p = [jnp.exp2(s[h] - m_new[h]).astype(BF16) for h in heads]
        alpha = [jnp.exp2(m_prev[h] - m_new[h]) for h in heads]
        for h in heads:
            m_scr[h] = m_new[h]
        pv = [jnp.dot(vt_ref[j, vs(h), :], p[h], preferred_element_type=F32) for h in heads]
        for h in heads:
            acc_scr[h] = alpha[h] * acc_scr[h] + pv[h]

    def qk_to_scratch(j, buf, tiles=None):
        off = pl.multiple_of(j * TK, TK)
        for h in heads:
            s_scr[buf, h] = scores(h, k_ref[pl.ds(off, TK), hs(h)],
                                   None if tiles is None else tiles[h])

    def from_scratch(buf):
        return [s_scr[buf, h] for h in heads]

    def pair_steps(j):
        qk_to_scratch(j + 1, 1)
        consume(j, from_scratch(0))
        qk_to_scratch(j + 2, 0)
        consume(j + 1, from_scratch(1))

    def far_pairs(n):
        def body(t, carry):
            for k in range(n):
                pair_steps(2 * (n * t + k))
            return carry
        return body

    j_sub = jnp.maximum(i - 1, 0)
    n_pairs = j_sub // 2
    qk_to_scratch(0, 0)
    lax.fori_loop(0, n_pairs // 4, far_pairs(4), 0)
    lax.fori_loop(2 * (n_pairs // 4), n_pairs // 2, far_pairs(2), 0)
    lax.fori_loop(2 * (n_pairs // 2), n_pairs, far_pairs(1), 0)

    def finish(meta):
        lam = _lambda(lam_ref)
        for h in heads:
            m_meta, acc_meta = meta[h]
            m_main = m_scr[h]
            m = jnp.maximum(m_main, m_meta)
            acc = jnp.exp2(m_main - m) * acc_scr[h] + jnp.exp2(m_meta - m) * acc_meta
            o_all = acc[:DIFF_DV] / acc[DIFF_DV:DIFF_DV + 1]
            o = o_all[:, :TQ] - lam * o_all[:, TQ:]
            y = o * lax.rsqrt(jnp.mean(o * o, axis=0, keepdims=True) + DIFF_EPS) * subg_ref[...]
            y_ref[:, hs(h)] = (y * (1.0 - LAM_INIT)).T.astype(BF16)

    @pl.when(j_sub % 2 == 1)
    def _():
        meta = meta_state()
        qk_to_scratch(j_sub, 1, tsub_ref)
        consume(j_sub - 1, from_scratch(0))
        qk_to_scratch(i, 0, tdiag_ref)
        consume(j_sub, from_scratch(1))
        consume(i, from_scratch(0))
        finish(meta)

    @pl.when(j_sub % 2 == 0)
    def _():
        meta = meta_state()
        qk_to_scratch(i, 1, tdiag_ref)
        pen = jnp.where(i == 0, 3.0 * NEG_INF, 0.0)
        consume(j_sub, [s_scr[0, h] + (tsub_ref[h] + pen) for h in heads])
        consume(i, from_scratch(1))
        finish(meta)


def _diff_attn(dqt, dk, dvt, mk, mvt, tdiag, tsub, tmeta, lam_p, subg_b, n_seq):
    rows = dk.shape[0]
    seq = rows // n_seq
    nq = seq // TQ
    hb = HEADS_PER_STEP
    wid = hb * DIFF_DV
    wid_v = hb * DV_AUG
    tile3 = lambda b, g, i: (g, 0, 0)
    slow = lambda shape, imap: pl.BlockSpec(shape, imap, pipeline_mode=pl.Buffered(1))
    return pl.pallas_call(
        _diff_attn_kernel, grid=(n_seq, DIFF_HEADS // hb, nq),
        in_specs=[pl.BlockSpec((1, wid, TQ), lambda b, g, i: (b * nq + i, g, 0)),
                  slow((seq, wid), lambda b, g, i: (b, g)),
                  slow((nq, wid_v, TK), lambda b, g, i: (b, g, 0)),
                  slow((N_META, wid), lambda b, g, i: (0, g)),
                  slow((wid_v, N_META), lambda b, g, i: (g, 0)),
                  slow((hb, TK, 2 * TQ), tile3),
                  slow((hb, TK, 2 * TQ), tile3),
                  slow((hb, 2, N_META, 2 * TQ), lambda b, g, i: (g, 0, 0, 0)),
                  slow((4, DIFF_DH), lambda b, g, i: (0, 0)),
                  slow((DIFF_DV, TQ), lambda b, g, i: (0, 0))],
        out_specs=pl.BlockSpec((TQ, wid), lambda b, g, i: (b * nq + i, g)),
        out_shape=jax.ShapeDtypeStruct((rows, DIFF_W), BF16),
        scratch_shapes=[pltpu.VMEM((hb, DIFF_DV, 2 * TQ), BF16),
                        pltpu.VMEM((hb, 1, 2 * TQ), F32),
                        pltpu.VMEM((hb, DV_AUG, 2 * TQ), F32),
                        pltpu.VMEM((2, hb, TK, 2 * TQ), F32)],
        compiler_params=_cparams(3), name="diff_attn",
    )(dqt, dk, dvt, mk, mvt, tdiag, tsub, tmeta, lam_p, subg_b)


def _sample_attn_kernel(q_ref, ck_ref, cv_ref, nk_ref, nv_ref, tile_ref, lam_ref, subg_ref,
                        y_ref, k_scr, v_scr, *, past, n_new):
    pad = k_scr.shape[1] - past - n_new
    lam = _lambda(lam_ref)
    for h in range(DIFF_HEADS):
        hs = slice(h * DIFF_DV, (h + 1) * DIFF_DV)
        k_scr[h, 0:past] = ck_ref[0, pl.ds(h, past, stride=DIFF_HEADS), :].astype(BF16)
        v_scr[h, 0:past] = cv_ref[0, pl.ds(h, past, stride=DIFF_HEADS), :].astype(BF16)
        k_scr[h, past:past + n_new] = nk_ref[:, hs]
        v_scr[h, past:past + n_new] = nv_ref[:, hs]
        k_scr[h, past + n_new:] = jnp.zeros((pad, LANES), BF16)
        v_scr[h, past + n_new:] = jnp.zeros((pad, LANES), BF16)
        qq = _stack_maps(q_ref[:, hs])
        s = lax.dot_general(qq, k_scr[h], NT_DIMS, preferred_element_type=F32)
        s = (s.reshape(2, n_new, s.shape[-1]) + tile_ref[h][None]).reshape(2 * n_new, s.shape[-1])
        m = jnp.max(s, axis=1, keepdims=True)
        p = jnp.exp2(s - m)
        l = jnp.sum(p, axis=1, keepdims=True)
        o_all = jnp.dot(p.astype(BF16), v_scr[h], preferred_element_type=F32) / l
        y_ref[:, hs] = _mix_and_norm(o_all, lam, subg_ref[...], n_new)


def _sample_attn(dq, ck, cv, nk, nv, tile, lam_p, subg, n_seq, past, n_new):
    kpad = tile.shape[-1]
    row = pl.BlockSpec((n_new, DIFF_W), lambda b: (b, 0))
    cache = pl.BlockSpec((1, past * DIFF_HEADS, DIFF_DV), lambda b: (b, 0, 0))
    return pl.pallas_call(
        functools.partial(_sample_attn_kernel, past=past, n_new=n_new),
        grid=(n_seq,),
        in_specs=[row, cache, cache, row, row, _resident(tile.shape),
                  _resident((4, DIFF_DH)), _resident((1, DIFF_DV))],
        out_specs=row,
        out_shape=jax.ShapeDtypeStruct((n_seq * n_new, DIFF_W), BF16),
        scratch_shapes=[pltpu.VMEM((DIFF_HEADS, kpad, LANES), BF16),
                        pltpu.VMEM((DIFF_HEADS, kpad, LANES), BF16)],
        compiler_params=_cparams(1), name="sample_attn",
    )(dq, ck, cv, nk, nv, tile, lam_p, subg)


def _tail_kernel(h_ref, yr_ref, yd_ref, gr_ref, gd_ref, wrb_ref, wdb_ref, wo_ref, n2_ref,
                 wup_ref, wdn_ref, nf_ref, out_ref):
    a = jnp.dot(yr_ref[...], wrb_ref[...], preferred_element_type=F32)
    b = jnp.dot(yd_ref[...], wdb_ref[...], preferred_element_type=F32)
    merged = jax.nn.sigmoid(gr_ref[...]) * a + jax.nn.sigmoid(gd_ref[...]) * b
    h = h_ref[...] + jnp.dot(merged.astype(BF16), wo_ref[...], preferred_element_type=F32)
    xn = h * lax.rsqrt(jnp.mean(h * h, axis=-1, keepdims=True) + NORM_EPS) * n2_ref[...]
    xn = xn.astype(BF16)
    acc = jnp.zeros_like(h)
    for lo in range(0, D_FF, FF_CHUNK):
        n = min(FF_CHUNK, D_FF - lo)
        gate = jnp.dot(xn, wup_ref[:, lo:lo + n], preferred_element_type=F32)
        up = jnp.dot(xn, wup_ref[:, D_FF + lo:D_FF + lo + n], preferred_element_type=F32)
        act = (gate * jax.nn.sigmoid(gate) * up).astype(BF16)
        acc = acc + jnp.dot(act, wdn_ref[lo:lo + n, :], preferred_element_type=F32)
    h = h + acc
    out_ref[...] = h * lax.rsqrt(jnp.mean(h * h, axis=-1, keepdims=True) + NORM_EPS) * nf_ref[...]


def _tail(h2d, y_ret, y_diff, gr, gd, wrb, wdb, wo, n2, wup, wdn, nf, tm):
    rows = h2d.shape[0]
    row = pl.BlockSpec((tm, D_MODEL), lambda i: (i, 0))
    return pl.pallas_call(
        _tail_kernel, grid=(rows // tm,),
        in_specs=[row, row, row, row, row,
                  _resident(wrb.shape), _resident(wdb.shape), _resident(wo.shape),
                  _resident(n2.shape), _resident(wup.shape), _resident(wdn.shape),
                  _resident(nf.shape)],
        out_specs=row,
        out_shape=jax.ShapeDtypeStruct((rows, D_MODEL), F32),
        compiler_params=_cparams(1), name="tail",
    )(h2d, y_ret, y_diff, gr, gd, wrb, wdb, wo, n2, wup, wdn, nf)


def kernel(x_prompt, x_sample, cache_k, cache_v, state_ret, meta_tokens, rel_bias, norm1_g, w_in,
           lambda_q1, lambda_k1, lambda_q2, lambda_k2, diff_subln_g, w_ret_branch, w_diff_branch,
           w_o, norm2_g, w_ffn_up, w_ffn_down, normf_g):
    assert w_in.shape[0] == 1, "single-layer step only"
    bsz, seq, _ = x_prompt.shape
    dbsz, dseq, _ = x_sample.shape
    past = cache_k.shape[2]
    assert TQ == TK == TM_PROJ and TQ % CHUNK == 0 and DIFF_HEADS % HEADS_PER_STEP == 0
    assert seq % TQ == 0
    assert dseq <= CHUNK and past % CHUNK == 0 and meta_tokens.shape[0] == N_META

    w_in_bf = w_in[0].astype(BF16)
    g1 = norm1_g[0][None, :]
    lam_p = jnp.stack([lambda_q1[0], lambda_k1[0], lambda_q2[0], lambda_k2[0]])
    subg = diff_subln_g[0][None, :]
    wts = (w_ret_branch[0].astype(BF16), w_diff_branch[0].astype(BF16), w_o[0].astype(BF16),
           norm2_g[0][None, :], w_ffn_up[0].astype(BF16), w_ffn_down[0].astype(BF16),
           normf_g[None, :])

    kpad = ((past + dseq + LANES - 1) // LANES) * LANES
    both_maps = lambda ix: np.concatenate([ix.T, ix.T], axis=1)
    idx_diag = both_maps(_bias_idx(np.arange(TQ), np.arange(TK), TK))
    idx_sub = both_maps(_bias_idx(np.arange(TQ) + TK, np.arange(TK), TK))
    idx_meta = np.concatenate(
        [both_maps(_bias_idx(np.arange(TQ) + b * TQ, np.arange(N_META) - N_META, N_META))
         for b in range(2)], axis=0)
    idx_samp = _bias_idx(past + np.arange(dseq), np.arange(kpad), past + dseq)
    t_diag, t_sub, t_meta, t_samp = _bias_tiles(rel_bias, [idx_diag, idx_sub, idx_meta, idx_samp],
                                                [True, True, True, False])
    t_meta = t_meta.reshape(DIFF_HEADS, 2, N_META, 2 * TQ)

    cs_m, sn_m = _rotary_tables(np.arange(-N_META, 0))
    m_out = _in_proj(meta_tokens, g1, w_in_bf, cs_m, sn_m, N_META)
    zero_state = jnp.zeros((1, RET_HEADS, RET_DK, RET_DV), F32)
    _, s_meta = _retention(m_out[0], m_out[1], m_out[2], m_out[3], zero_state, 1, N_META,
                           with_output=False)
    mk = m_out[5]
    mv_aug = jnp.concatenate([m_out[7].reshape(N_META, DIFF_HEADS, DIFF_DV),
                              jnp.ones((N_META, DIFF_HEADS, ONES_ROWS), BF16)], axis=2)
    mvt = mv_aug.reshape(N_META, DIFF_HEADS * DV_AUG).T
    subg_b = jnp.broadcast_to(diff_subln_g[0][:, None], (DIFF_DV, TQ))

    cs_p, sn_p = _rotary_tables(np.arange(seq))
    x2d = x_prompt.reshape(bsz * seq, D_MODEL)
    meta_rows = lambda u: u.reshape(N_META * DIFF_HEADS, DIFF_DV)
    (y_ret, s_fin, dqt, dkb, dvt, gr, gd, k_rows, v_rows) = _prompt_proj(
        x2d, g1, w_in_bf, cs_p, sn_p, s_meta, meta_rows(m_out[6]), meta_rows(m_out[8]), bsz,
        TM_PROJ)
    y_diff = _diff_attn(dqt, dkb, dvt, mk, mvt, t_diag, t_sub, t_meta, lam_p, subg_b, bsz)
    y_prompt = _tail(x2d, y_ret, y_diff, gr, gd, *wts, TM_TAIL).reshape(bsz, seq, D_MODEL)

    cs_s, sn_s = _rotary_tables(np.tile(past + np.arange(dseq), dbsz))
    xs2d = x_sample.reshape(dbsz * dseq, D_MODEL)
    (rq, rk, rv, rg, dq, dkb, dkf_s, dvb, dvf_s, gr, gd) = _in_proj(xs2d, g1, w_in_bf, cs_s, sn_s,
                                                                     dbsz * dseq)
    y_ret, s_samp = _retention(rq, rk, rv, rg, state_ret[0], dbsz, dseq)
    ck = cache_k[0].reshape(dbsz, past * DIFF_HEADS, DIFF_DV)
    cv = cache_v[0].reshape(dbsz, past * DIFF_HEADS, DIFF_DV)
    y_diff = _sample_attn(dq, ck, cv, dkb, dvb, t_samp, lam_p, subg, dbsz, past, dseq)
    y_sample = _tail(xs2d, y_ret, y_diff, gr, gd, *wts, dbsz * dseq).reshape(dbsz, dseq, D_MODEL)

    heads = (DIFF_HEADS, DIFF_DV)
    return (y_prompt, y_sample,
            k_rows.reshape((1, bsz, N_META + seq) + heads),
            v_rows.reshape((1, bsz, N_META + seq) + heads),
            s_fin[None],
            dkf_s.reshape((1, dbsz, dseq) + heads),
            dvf_s.reshape((1, dbsz, dseq) + heads),
            s_samp[None])
```

```python
import functools
import math

import numpy as np
import jax
import jax.numpy as jnp
from jax import lax
from jax.experimental import pallas as pl
from jax.experimental.pallas import tpu as pltpu

F32 = jnp.float32
BF16 = jnp.bfloat16

D_MODEL = 1024
CHUNK = 64
N_META = 16
RET_HEADS = 4
RET_DK = D_MODEL // 8
RET_DV = 2 * RET_DK
RET_QK_W = RET_HEADS * RET_DK
RET_V_W = RET_HEADS * RET_DV
ROPE_BASE = 10000.0
RET_EPS = 1e-6
DIFF_HEADS = 8
DIFF_DH = D_MODEL // 16
DIFF_DV = 2 * DIFF_DH
DIFF_W = DIFF_HEADS * DIFF_DV
DIFF_EPS = 1e-5
N_BUCKETS = 32
MAX_DISTANCE = 128
D_FF = ((8 * D_MODEL + 3 * 256 - 1) // (3 * 256)) * 256
NORM_EPS = 1e-6
NEG_INF = -1e30
LAM_INIT = 0.8 - 0.6 * math.exp(-0.3 * 0)
LOG2E = math.log2(math.e)
Q_SCALE = DIFF_DH ** -0.5 * LOG2E
ONES_ROWS = 16
DV_AUG = DIFF_DV + ONES_ROWS

OFF_RQ = 0
OFF_RK = OFF_RQ + RET_QK_W
OFF_RV = OFF_RK + RET_QK_W
OFF_RG = OFF_RV + RET_V_W
OFF_DQ = OFF_RG + RET_V_W
OFF_DK = OFF_DQ + DIFF_W
OFF_DV = OFF_DK + DIFF_W
OFF_GR = OFF_DV + DIFF_W
OFF_GD = OFF_GR + D_MODEL
W_IN = OFF_GD + D_MODEL

LANES = 128
VMEM_LIMIT_BYTES = 56 * 1024 * 1024
TM_PROJ = 256
TM_TAIL = 256
TQ = 256
TK = 256
HEADS_PER_STEP = 4
FF_CHUNK = 512

NT_DIMS = (((1,), (1,)), ((), ()))
TN_DIMS = (((0,), (0,)), ((), ()))


def _cparams(n_axes):
    return pltpu.CompilerParams(dimension_semantics=("arbitrary",) * n_axes,
                                vmem_limit_bytes=VMEM_LIMIT_BYTES)


def _resident(shape):
    nd = len(shape)
    return pl.BlockSpec(shape, lambda *_: (0,) * nd, pipeline_mode=pl.Buffered(1))


def _t5_bucket_np(rel):
    nb = N_BUCKETS // 2
    max_exact = nb // 2
    ret = np.where(rel > 0, nb, 0)
    n = np.abs(rel)
    nf = np.maximum(n, max_exact).astype(np.float64)
    large = max_exact + (np.log(nf / max_exact) / math.log(MAX_DISTANCE / max_exact)
                         * (nb - max_exact)).astype(np.int32)
    large = np.minimum(large, nb - 1)
    return (ret + np.where(n < max_exact, n, large)).astype(np.int32)


def _bias_idx(qpos, kpos, n_real):
    qpos = np.asarray(qpos)[:, None]
    kpos = np.asarray(kpos)[None, :]
    vis = (np.floor_divide(kpos, CHUNK) <= np.floor_divide(qpos, CHUNK))
    vis = vis & (np.arange(kpos.shape[1])[None, :] < n_real)
    return np.where(vis, _t5_bucket_np(kpos - qpos), -1).astype(np.int32)


def _rotary_tables(pos):
    half = RET_DK // 2
    freq = 1.0 / (ROPE_BASE ** jnp.linspace(0.0, 1.0, half, dtype=F32))
    ang = jnp.asarray(pos, F32)[:, None] * freq[None, :]
    cos, sin = jnp.cos(ang), jnp.sin(ang)
    return jnp.concatenate([cos, cos], axis=1), jnp.concatenate([-sin, sin], axis=1)


def _retention_tables(t):
    gam = 1.0 - 2.0 ** (-5.0 - np.arange(RET_HEADS, dtype=np.float64))
    lg = np.log(gam)[:, None, None]
    n = np.arange(t)[:, None]
    m = np.arange(t)[None, :]
    cn, cm = n // CHUNK, m // CHUNK
    expo = np.where(cm == cn, np.abs(n - m), n - m).astype(np.float64)[None]
    dmat = np.where((cm <= cn)[None], np.exp(lg * expo), 0.0)
    qdec = np.exp(lg[:, :, 0] * (np.arange(t)[None, :] + 1.0))[:, :, None]
    kdec = np.exp(lg[:, :, 0] * (t - 1.0 - np.arange(t)[None, :]))[:, :, None]
    sdec = np.exp(lg[:, 0, 0] * t)
    return (jnp.asarray(dmat, F32), jnp.asarray(qdec, F32), jnp.asarray(kdec, F32),
            tuple(float(s) for s in sdec))


def _bias_kernel(rb_ref, *refs, n_tiles, shifted):
    idx_refs, out_refs = refs[:n_tiles], refs[n_tiles:]
    h = pl.program_id(0)
    far = rb_ref[N_BUCKETS // 2 - 1, h]
    for idx_ref, out_ref, sh in zip(idx_refs, out_refs, shifted):
        idx = idx_ref[...]
        acc = jnp.full(idx.shape, NEG_INF, F32)
        for b in range(N_BUCKETS):
            val = (rb_ref[b, h] - far if sh else rb_ref[b, h]) * LOG2E
            acc = jnp.where(idx == b, val, acc)
        out_ref[0] = acc


def _bias_tiles(rel_bias, idx_list, shifted):
    n = len(idx_list)
    in_specs = [pl.BlockSpec(memory_space=pltpu.SMEM)]
    in_specs += [pl.BlockSpec(ix.shape, lambda h: (0, 0)) for ix in idx_list]
    out_specs = [pl.BlockSpec((1,) + ix.shape, lambda h: (h, 0, 0)) for ix in idx_list]
    out_shape = [jax.ShapeDtypeStruct((DIFF_HEADS,) + ix.shape, F32) for ix in idx_list]
    return pl.pallas_call(
        functools.partial(_bias_kernel, n_tiles=n, shifted=tuple(shifted)),
        grid=(DIFF_HEADS,), in_specs=in_specs, out_specs=out_specs, out_shape=out_shape,
        compiler_params=_cparams(1), name="bias_tiles",
    )(rel_bias, *[jnp.asarray(ix) for ix in idx_list])


def _norm_proj(x_ref, g_ref, w_ref):
    x = x_ref[...]
    xn = x * lax.rsqrt(jnp.mean(x * x, axis=-1, keepdims=True) + NORM_EPS) * g_ref[...]
    xn = xn.astype(BF16)
    return lambda lo, n: jnp.dot(xn, w_ref[:, lo:lo + n], preferred_element_type=F32)


def _rotary(u, cs, sn):
    return u * cs + pltpu.roll(u, RET_DK // 2, 1) * sn


def _retention_head(qf, kf, v, gate, state, dm, qd, kd, sdec, with_output=True):
    y = None
    if with_output:
        s = lax.dot_general(qf.astype(BF16), kf.astype(BF16), NT_DIMS,
                            preferred_element_type=F32) * dm
        o = jnp.dot(s.astype(BF16), v, preferred_element_type=F32)
        o = o + jnp.dot((qf * qd).astype(BF16), state.astype(BF16), preferred_element_type=F32)
        y = o * lax.rsqrt(jnp.mean(o * o, axis=-1, keepdims=True) + RET_EPS)
        y = y * (gate * jax.nn.sigmoid(gate))
    kv = lax.dot_general((kf * kd).astype(BF16), v, TN_DIMS, preferred_element_type=F32)
    return y, sdec * state + kv


def _inproj_kernel(x_ref, g_ref, w_ref, cs_ref, sn_ref,
                   rq_ref, rk_ref, rv_ref, rg_ref, dq_ref, dkb_ref, dkf_ref,
                   dvb_ref, dvf_ref, gr_ref, gd_ref):
    proj = _norm_proj(x_ref, g_ref, w_ref)
    cs = cs_ref[...]
    sn = sn_ref[...]
    uq = proj(OFF_RQ, RET_QK_W)
    uk = proj(OFF_RK, RET_QK_W)
    for hh in range(RET_HEADS):
        sl = slice(hh * RET_DK, (hh + 1) * RET_DK)
        rq_ref[:, sl] = (_rotary(uq[:, sl], cs, sn) * (RET_DK ** -0.5)).astype(BF16)
        rk_ref[:, sl] = _rotary(uk[:, sl], cs, sn).astype(BF16)
    rv_ref[...] = proj(OFF_RV, RET_V_W).astype(BF16)
    rg_ref[...] = proj(OFF_RG, RET_V_W)
    dq_ref[...] = (proj(OFF_DQ, DIFF_W) * Q_SCALE).astype(BF16)
    dk = proj(OFF_DK, DIFF_W)
    dkf_ref[...] = dk
    dkb_ref[...] = dk.astype(BF16)
    dv = proj(OFF_DV, DIFF_W)
    dvf_ref[...] = dv
    dvb_ref[...] = dv.astype(BF16)
    gr_ref[...] = proj(OFF_GR, D_MODEL)
    gd_ref[...] = proj(OFF_GD, D_MODEL)


def _in_proj(x2d, g, w_bf, cs, sn, tm):
    rows = x2d.shape[0]
    n_pos = cs.shape[0] // tm
    row = lambda w: pl.BlockSpec((tm, w), lambda i: (i, 0))
    pos = pl.BlockSpec((tm, LANES), lambda i: (i % n_pos, 0))
    outs = [(RET_QK_W, BF16), (RET_QK_W, BF16), (RET_V_W, BF16), (RET_V_W, F32),
            (DIFF_W, BF16), (DIFF_W, BF16), (DIFF_W, F32), (DIFF_W, BF16), (DIFF_W, F32),
            (D_MODEL, F32), (D_MODEL, F32)]
    return pl.pallas_call(
        _inproj_kernel, grid=(rows // tm,),
        in_specs=[row(D_MODEL), _resident((1, D_MODEL)), _resident((D_MODEL, W_IN)), pos, pos],
        out_specs=[row(w) for w, _ in outs],
        out_shape=[jax.ShapeDtypeStruct((rows, w), dt) for w, dt in outs],
        compiler_params=_cparams(1), name="in_proj",
    )(x2d, g, w_bf, cs, sn)


def _prompt_proj_kernel(x_ref, g_ref, w_ref, cs_ref, sn_ref, s0_ref, dm_ref, qd_ref, kd_ref,
                        mk_ref, mv_ref,
                        yret_ref, sfin_ref, dqt_ref, dkb_ref, dvt_ref, gr_ref, gd_ref,
                        krows_ref, vrows_ref,
                        s_scr, stage_scr, row_sem, meta_sem, *, tiles_per_seq, n_steps, sdec):
    i = pl.program_id(0)
    tm = x_ref.shape[0]
    proj = _norm_proj(x_ref, g_ref, w_ref)
    cs = cs_ref[...]
    sn = sn_ref[...]
    slot = i % 2
    stream = i // tiles_per_seq
    rows_out = (krows_ref, vrows_ref)

    def rows_copy(which, step):
        first = (N_META + (step % tiles_per_seq) * tm) * DIFF_HEADS
        return pltpu.make_async_copy(
            stage_scr.at[step % 2, which],
            rows_out[which].at[step // tiles_per_seq, pl.ds(first, tm * DIFF_HEADS), :],
            row_sem.at[step % 2, which])

    def meta_copy(which):
        return pltpu.make_async_copy(
            (mk_ref, mv_ref)[which],
            rows_out[which].at[stream, pl.ds(0, N_META * DIFF_HEADS), :], meta_sem.at[which])

    @pl.when(i % tiles_per_seq == 0)
    def _():
        s_scr[...] = s0_ref[0]
        for which in range(2):
            meta_copy(which).start()

    @pl.when(i >= 2)
    def _():
        for which in range(2):
            rows_copy(which, i - 2).wait()

    uq = proj(OFF_RQ, RET_QK_W)
    uk = proj(OFF_RK, RET_QK_W)
    rv = proj(OFF_RV, RET_V_W).astype(BF16)
    rg = proj(OFF_RG, RET_V_W)
    for hh in range(RET_HEADS):
        qs = slice(hh * RET_DK, (hh + 1) * RET_DK)
        vs = slice(hh * RET_DV, (hh + 1) * RET_DV)
        y, state = _retention_head(_rotary(uq[:, qs], cs, sn) * (RET_DK ** -0.5),
                                   _rotary(uk[:, qs], cs, sn), rv[:, vs], rg[:, vs],
                                   s_scr[hh], dm_ref[hh], qd_ref[hh], kd_ref[hh], sdec[hh])
        yret_ref[:, vs] = y.astype(BF16)
        s_scr[hh] = state

    @pl.when(i % tiles_per_seq == tiles_per_seq - 1)
    def _():
        sfin_ref[0] = s_scr[...]

    dqt_ref[0] = (proj(OFF_DQ, DIFF_W) * Q_SCALE).T.astype(BF16)
    dk = proj(OFF_DK, DIFF_W)
    dkb_ref[...] = dk.astype(BF16)
    dv = proj(OFF_DV, DIFF_W)
    dvt = dv.T.astype(BF16)
    ones = jnp.ones((ONES_ROWS, dvt.shape[1]), BF16)
    for hh in range(DIFF_HEADS):
        hs = slice(hh * DIFF_DV, (hh + 1) * DIFF_DV)
        dvt_ref[0, hh * DV_AUG:hh * DV_AUG + DIFF_DV, :] = dvt[hs]
        dvt_ref[0, hh * DV_AUG + DIFF_DV:(hh + 1) * DV_AUG, :] = ones
        head_rows = pl.ds(hh, tm, stride=DIFF_HEADS)
        stage_scr[slot, 0, head_rows, :] = dk[:, hs]
        stage_scr[slot, 1, head_rows, :] = dv[:, hs]
    for which in range(2):
        rows_copy(which, i).start()
    gr_ref[...] = proj(OFF_GR, D_MODEL)
    gd_ref[...] = proj(OFF_GD, D_MODEL)

    @pl.when(i % tiles_per_seq == 0)
    def _():
        for which in range(2):
            meta_copy(which).wait()

    @pl.when(i == n_steps - 1)
    def _():
        for which in range(2):
            if n_steps >= 2:
                rows_copy(which, i - 1).wait()
            rows_copy(which, i).wait()


def _prompt_proj(x2d, g, w_bf, cs, sn, s0, mk_rows, mv_rows, n_seq, tm):
    rows = x2d.shape[0]
    seq = rows // n_seq
    tps = seq // tm
    dmat, qdec, kdec, sdec = _retention_tables(tm)
    row = lambda w: pl.BlockSpec((tm, w), lambda i: (i, 0))
    pos = pl.BlockSpec((tm, LANES), lambda i: (i % tps, 0))
    tposed = lambda w: pl.BlockSpec((1, w, tm), lambda i: (i, 0, 0))
    hbm = pl.BlockSpec(memory_space=pl.ANY)
    s_shape = (1, RET_HEADS, RET_DK, RET_DV)
    wv = DIFF_HEADS * DV_AUG
    rows_shape = jax.ShapeDtypeStruct((n_seq, (N_META + seq) * DIFF_HEADS, DIFF_DV), F32)
    return pl.pallas_call(
        functools.partial(_prompt_proj_kernel, tiles_per_seq=tps, n_steps=rows // tm, sdec=sdec),
        grid=(rows // tm,),
        in_specs=[row(D_MODEL), _resident((1, D_MODEL)), _resident((D_MODEL, W_IN)), pos, pos,
                  _resident(s_shape), _resident(dmat.shape), _resident(qdec.shape),
                  _resident(kdec.shape), _resident(mk_rows.shape), _resident(mv_rows.shape)],
        out_specs=[row(RET_V_W), pl.BlockSpec(s_shape, lambda i: (i // tps, 0, 0, 0)),
                   tposed(DIFF_W), row(DIFF_W), tposed(wv), row(D_MODEL), row(D_MODEL),
                   hbm, hbm],
        out_shape=[jax.ShapeDtypeStruct((rows, RET_V_W), BF16),
                   jax.ShapeDtypeStruct((n_seq,) + s_shape[1:], F32),
                   jax.ShapeDtypeStruct((rows // tm, DIFF_W, tm), BF16),
                   jax.ShapeDtypeStruct((rows, DIFF_W), BF16),
                   jax.ShapeDtypeStruct((rows // tm, wv, tm), BF16),
                   jax.ShapeDtypeStruct((rows, D_MODEL), F32),
                   jax.ShapeDtypeStruct((rows, D_MODEL), F32),
                   rows_shape, rows_shape],
        scratch_shapes=[pltpu.VMEM(s_shape[1:], F32),
                        pltpu.VMEM((2, 2, tm * DIFF_HEADS, DIFF_DV), F32),
                        pltpu.SemaphoreType.DMA((2, 2)), pltpu.SemaphoreType.DMA((2,))],
        compiler_params=_cparams(1), name="prompt_proj",
    )(x2d, g, w_bf, cs, sn, s0, dmat, qdec, kdec, mk_rows, mv_rows)


def _retention_kernel(rq_ref, rk_ref, rv_ref, rg_ref, s0_ref, dm_ref, qd_ref, kd_ref,
                      y_ref, sfin_ref, s_scr, *, sdec, with_output):
    t = pl.program_id(1)

    @pl.when(t == 0)
    def _():
        s_scr[...] = s0_ref[0]

    for hh in range(RET_HEADS):
        qs = slice(hh * RET_DK, (hh + 1) * RET_DK)
        vs = slice(hh * RET_DV, (hh + 1) * RET_DV)
        y, state = _retention_head(rq_ref[:, qs].astype(F32), rk_ref[:, qs].astype(F32),
                                   rv_ref[:, vs], rg_ref[:, vs], s_scr[hh], dm_ref[hh],
                                   qd_ref[hh], kd_ref[hh], sdec[hh], with_output)
        if with_output:
            y_ref[:, vs] = y.astype(BF16)
        else:
            y_ref[:, vs] = jnp.zeros((y_ref.shape[0], RET_DV), BF16)
        s_scr[hh] = state

    @pl.when(t == pl.num_programs(1) - 1)
    def _():
        sfin_ref[0] = s_scr[...]


def _retention(rq, rk, rv, rg, s0, n_seq, t, with_output=True):
    rows = rq.shape[0]
    nt = rows // n_seq // t
    dmat, qdec, kdec, sdec = _retention_tables(t)
    row = lambda w: pl.BlockSpec((t, w), lambda b, i: (b * nt + i, 0))
    s_shape = (1, RET_HEADS, RET_DK, RET_DV)
    if s0.shape[0] == 1:
        s0_spec = pl.BlockSpec(s_shape, lambda b, i: (0, 0, 0, 0))
    else:
        s0_spec = pl.BlockSpec(s_shape, lambda b, i: (b, 0, 0, 0))
    y, sfin = pl.pallas_call(
        functools.partial(_retention_kernel, sdec=sdec, with_output=with_output),
        grid=(n_seq, nt),
        in_specs=[row(RET_QK_W), row(RET_QK_W), row(RET_V_W), row(RET_V_W), s0_spec,
                  _resident(dmat.shape), _resident(qdec.shape), _resident(kdec.shape)],
        out_specs=[row(RET_V_W), pl.BlockSpec(s_shape, lambda b, i: (b, 0, 0, 0))],
        out_shape=[jax.ShapeDtypeStruct((rows, RET_V_W), BF16),
                   jax.ShapeDtypeStruct((n_seq,) + s_shape[1:], F32)],
        scratch_shapes=[pltpu.VMEM(s_shape[1:], F32)],
        compiler_params=_cparams(2), name="retention",
    )(rq, rk, rv, rg, s0, dmat, qdec, kdec)
    return y, sfin


def _lambda(lam_ref):
    lp = lam_ref[...]
    a = jnp.exp(jnp.sum(lp[0:1] * lp[1:2], axis=-1, keepdims=True))
    b = jnp.exp(jnp.sum(lp[2:3] * lp[3:4], axis=-1, keepdims=True))
    return a - b + LAM_INIT


def _stack_maps(q):
    qf = q.astype(F32)
    lane = lax.broadcasted_iota(jnp.int32, q.shape, 1)
    return jnp.concatenate([jnp.where(lane < DIFF_DH, qf, 0.0),
                            jnp.where(lane < DIFF_DH, 0.0, qf)], axis=0).astype(BF16)


def _mix_and_norm(o_all, lam, subg, n):
    o = o_all[:n] - lam * o_all[n:]
    y = o * lax.rsqrt(jnp.mean(o * o, axis=-1, keepdims=True) + DIFF_EPS) * subg
    return (y * (1.0 - LAM_INIT)).astype(BF16)


def _diff_attn_kernel(qt_ref, k_ref, vt_ref, mk_ref, mvt_ref, tdiag_ref, tsub_ref, tmeta_ref,
                      lam_ref, subg_ref, y_ref, qq_scr, m_scr, acc_scr, s_scr):
    i = pl.program_id(2)
    heads = range(HEADS_PER_STEP)
    hs = lambda h: slice(h * DIFF_DV, (h + 1) * DIFF_DV)
    vs = lambda h: slice(h * DV_AUG, (h + 1) * DV_AUG)

    for h in heads:
        qt = qt_ref[0, hs(h), :].astype(F32)
        row = lax.broadcasted_iota(jnp.int32, qt.shape, 0)
        qq_scr[h] = jnp.concatenate([jnp.where(row < DIFF_DH, qt, 0.0),
                                     jnp.where(row < DIFF_DH, 0.0, qt)], axis=1).astype(BF16)

    def scores(h, k_blk, tile):
        s = jnp.dot(k_blk, qq_scr[h], preferred_element_type=F32)
        return s if tile is None else s + tile

    for h in heads:
        m_scr[h] = jnp.full((1, 2 * TQ), NEG_INF, F32)
        acc_scr[h] = jnp.zeros((DV_AUG, 2 * TQ), F32)

    def meta_scores():
        return [scores(h, mk_ref[:, hs(h)], tmeta_ref[h, jnp.minimum(i, 1)]) for h in heads]

    def consume(j, s, s_meta=None):
        if s_meta is not None:
            s = [jnp.concatenate([s[h], s_meta[h]], axis=0) for h in heads]
        m_prev = [m_scr[h] for h in heads]
        m_new = [jnp.maximum(m_prev[h], jnp.max(s[h], axis=0, keepdims=True)) for h in heads]
        p = [jnp.exp2(s[h] - m_new[h]).astype(BF16) for h in heads]
        alpha = [jnp.exp2(m_prev[h] - m_new[h]) for h in heads]
        for h in heads:
            m_scr[h] = m_new[h]
        pv = [jnp.dot(vt_ref[j, vs(h), :], p[h][:TK], preferred_element_type=F32) for h in heads]
        if s_meta is not None:
            pv = [pv[h] + jnp.dot(mvt_ref[vs(h), :], p[h][TK:], preferred_element_type=F32)
                  for h in heads]
        for h in heads:
            acc_scr[h] = alpha[h] * acc_scr[h] + pv[h]

    def qk_to_scratch(j, buf, tiles=None):
        off = pl.multiple_of(j * TK, TK)
        for h in heads:
            s_scr[buf, h] = scores(h, k_ref[pl.ds(off, TK), hs(h)],
                                   None if tiles is None else tiles[h])

    def from_scratch(buf):
        return [s_scr[buf, h] for h in heads]

    def pair_steps(j):
        qk_to_scratch(j + 1, 1)
        consume(j, from_scratch(0))
        qk_to_scratch(j + 2, 0)
        consume(j + 1, from_scratch(1))

    def far_pairs(n):
        def body(t, carry):
            for k in range(n):
                pair_steps(2 * (n * t + k))
            return carry
        return body

    j_sub = jnp.maximum(i - 1, 0)
    n_pairs = j_sub // 2
    qk_to_scratch(0, 0)
    lax.fori_loop(0, n_pairs // 4, far_pairs(4), 0)
    lax.fori_loop(2 * (n_pairs // 4), n_pairs // 2, far_pairs(2), 0)
    lax.fori_loop(2 * (n_pairs // 2), n_pairs, far_pairs(1), 0)

    def finish():
        lam = _lambda(lam_ref)
        for h in heads:
            o_all = acc_scr[h, :DIFF_DV, :] / acc_scr[h, DIFF_DV:DIFF_DV + 1, :]
            o = o_all[:, :TQ] - lam * o_all[:, TQ:]
            y = o * lax.rsqrt(jnp.mean(o * o, axis=0, keepdims=True) + DIFF_EPS) * subg_ref[...]
            y_ref[:, hs(h)] = (y * (1.0 - LAM_INIT)).T.astype(BF16)

    @pl.when(j_sub % 2 == 1)
    def _():
        s_meta = meta_scores()
        qk_to_scratch(j_sub, 1, tsub_ref)
        consume(j_sub - 1, from_scratch(0))
        qk_to_scratch(i, 0, tdiag_ref)
        consume(j_sub, from_scratch(1))
        consume(i, from_scratch(0), s_meta)
        finish()

    @pl.when(j_sub % 2 == 0)
    def _():
        s_meta = meta_scores()
        qk_to_scratch(i, 1, tdiag_ref)
        pen = jnp.where(i == 0, 3.0 * NEG_INF, 0.0)
        consume(j_sub, [s_scr[0, h] + (tsub_ref[h] + pen) for h in heads])
        consume(i, from_scratch(1), s_meta)
        finish()


def _diff_attn(dqt, dk, dvt, mk, mvt, tdiag, tsub, tmeta, lam_p, subg_b, n_seq):
    rows = dk.shape[0]
    seq = rows // n_seq
    nq = seq // TQ
    hb = HEADS_PER_STEP
    wid = hb * DIFF_DV
    wid_v = hb * DV_AUG
    tile3 = lambda b, g, i: (g, 0, 0)
    slow = lambda shape, imap: pl.BlockSpec(shape, imap, pipeline_mode=pl.Buffered(1))
    return pl.pallas_call(
        _diff_attn_kernel, grid=(n_seq, DIFF_HEADS // hb, nq),
        in_specs=[pl.BlockSpec((1, wid, TQ), lambda b, g, i: (b * nq + i, g, 0)),
                  slow((seq, wid), lambda b, g, i: (b, g)),
                  slow((nq, wid_v, TK), lambda b, g, i: (b, g, 0)),
                  slow((N_META, wid), lambda b, g, i: (0, g)),
                  slow((wid_v, N_META), lambda b, g, i: (g, 0)),
                  slow((hb, TK, 2 * TQ), tile3),
                  slow((hb, TK, 2 * TQ), tile3),
                  slow((hb, 2, N_META, 2 * TQ), lambda b, g, i: (g, 0, 0, 0)),
                  slow((4, DIFF_DH), lambda b, g, i: (0, 0)),
                  slow((DIFF_DV, TQ), lambda b, g, i: (0, 0))],
        out_specs=pl.BlockSpec((TQ, wid), lambda b, g, i: (b * nq + i, g)),
        out_shape=jax.ShapeDtypeStruct((rows, DIFF_W), BF16),
        scratch_shapes=[pltpu.VMEM((hb, DIFF_DV, 2 * TQ), BF16),
                        pltpu.VMEM((hb, 1, 2 * TQ), F32),
                        pltpu.VMEM((hb, DV_AUG, 2 * TQ), F32),
                        pltpu.VMEM((2, hb, TK, 2 * TQ), F32)],
        compiler_params=_cparams(3), name="diff_attn",
    )(dqt, dk, dvt, mk, mvt, tdiag, tsub, tmeta, lam_p, subg_b)


def _sample_attn_kernel(q_ref, ck_ref, cv_ref, nk_ref, nv_ref, tile_ref, lam_ref, subg_ref,
                        y_ref, k_scr, v_scr, *, past, n_new):
    pad = k_scr.shape[1] - past - n_new
    lam = _lambda(lam_ref)
    for h in range(DIFF_HEADS):
        hs = slice(h * DIFF_DV, (h + 1) * DIFF_DV)
        k_scr[h, 0:past] = ck_ref[0, pl.ds(h, past, stride=DIFF_HEADS), :].astype(BF16)
        v_scr[h, 0:past] = cv_ref[0, pl.ds(h, past, stride=DIFF_HEADS), :].astype(BF16)
        k_scr[h, past:past + n_new] = nk_ref[:, hs]
        v_scr[h, past:past + n_new] = nv_ref[:, hs]
        k_scr[h, past + n_new:] = jnp.zeros((pad, LANES), BF16)
        v_scr[h, past + n_new:] = jnp.zeros((pad, LANES), BF16)
        qq = _stack_maps(q_ref[:, hs])
        s = lax.dot_general(qq, k_scr[h], NT_DIMS, preferred_element_type=F32)
        s = (s.reshape(2, n_new, s.shape[-1]) + tile_ref[h][None]).reshape(2 * n_new, s.shape[-1])
        m = jnp.max(s, axis=1, keepdims=True)
        p = jnp.exp2(s - m)
        l = jnp.sum(p, axis=1, keepdims=True)
        o_all = jnp.dot(p.astype(BF16), v_scr[h], preferred_element_type=F32) / l
        y_ref[:, hs] = _mix_and_norm(o_all, lam, subg_ref[...], n_new)


def _sample_attn(dq, ck, cv, nk, nv, tile, lam_p, subg, n_seq, past, n_new):
    kpad = tile.shape[-1]
    row = pl.BlockSpec((n_new, DIFF_W), lambda b: (b, 0))
    cache = pl.BlockSpec((1, past * DIFF_HEADS, DIFF_DV), lambda b: (b, 0, 0))
    return pl.pallas_call(
        functools.partial(_sample_attn_kernel, past=past, n_new=n_new),
        grid=(n_seq,),
        in_specs=[row, cache, cache, row, row, _resident(tile.shape),
                  _resident((4, DIFF_DH)), _resident((1, DIFF_DV))],
        out_specs=row,
        out_shape=jax.ShapeDtypeStruct((n_seq * n_new, DIFF_W), BF16),
        scratch_shapes=[pltpu.VMEM((DIFF_HEADS, kpad, LANES), BF16),
                        pltpu.VMEM((DIFF_HEADS, kpad, LANES), BF16)],
        compiler_params=_cparams(1), name="sample_attn",
    )(dq, ck, cv, nk, nv, tile, lam_p, subg)


def _tail_kernel(h_ref, yr_ref, yd_ref, gr_ref, gd_ref, wrb_ref, wdb_ref, wo_ref, n2_ref,
                 wup_ref, wdn_ref, nf_ref, out_ref):
    a = jnp.dot(yr_ref[...], wrb_ref[...], preferred_element_type=F32)
    b = jnp.dot(yd_ref[...], wdb_ref[...], preferred_element_type=F32)
    merged = jax.nn.sigmoid(gr_ref[...]) * a + jax.nn.sigmoid(gd_ref[...]) * b
    h = h_ref[...] + jnp.dot(merged.astype(BF16), wo_ref[...], preferred_element_type=F32)
    xn = h * lax.rsqrt(jnp.mean(h * h, axis=-1, keepdims=True) + NORM_EPS) * n2_ref[...]
    xn = xn.astype(BF16)
    acc = jnp.zeros_like(h)
    for lo in range(0, D_FF, FF_CHUNK):
        n = min(FF_CHUNK, D_FF - lo)
        gate = jnp.dot(xn, wup_ref[:, lo:lo + n], preferred_element_type=F32)
        up = jnp.dot(xn, wup_ref[:, D_FF + lo:D_FF + lo + n], preferred_element_type=F32)
        act = (gate * jax.nn.sigmoid(gate) * up).astype(BF16)
        acc = acc + jnp.dot(act, wdn_ref[lo:lo + n, :], preferred_element_type=F32)
    h = h + acc
    out_ref[...] = h * lax.rsqrt(jnp.mean(h * h, axis=-1, keepdims=True) + NORM_EPS) * nf_ref[...]


def _tail(h2d, y_ret, y_diff, gr, gd, wrb, wdb, wo, n2, wup, wdn, nf, tm):
    rows = h2d.shape[0]
    row = pl.BlockSpec((tm, D_MODEL), lambda i: (i, 0))
    return pl.pallas_call(
        _tail_kernel, grid=(rows // tm,),
        in_specs=[row, row, row, row, row,
                  _resident(wrb.shape), _resident(wdb.shape), _resident(wo.shape),
                  _resident(n2.shape), _resident(wup.shape), _resident(wdn.shape),
                  _resident(nf.shape)],
        out_specs=row,
        out_shape=jax.ShapeDtypeStruct((rows, D_MODEL), F32),
        compiler_params=_cparams(1), name="tail",
    )(h2d, y_ret, y_diff, gr, gd, wrb, wdb, wo, n2, wup, wdn, nf)


def kernel(x_prompt, x_sample, cache_k, cache_v, state_ret, meta_tokens, rel_bias, norm1_g, w_in,
           lambda_q1, lambda_k1, lambda_q2, lambda_k2, diff_subln_g, w_ret_branch, w_diff_branch,
           w_o, norm2_g, w_ffn_up, w_ffn_down, normf_g):
    assert w_in.shape[0] == 1, "single-layer step only"
    bsz, seq, _ = x_prompt.shape
    dbsz, dseq, _ = x_sample.shape
    past = cache_k.shape[2]
    assert TQ == TK == TM_PROJ and TQ % CHUNK == 0 and DIFF_HEADS % HEADS_PER_STEP == 0
    assert seq % TQ == 0
    assert dseq <= CHUNK and past % CHUNK == 0 and meta_tokens.shape[0] == N_META

    w_in_bf = w_in[0].astype(BF16)
    g1 = norm1_g[0][None, :]
    lam_p = jnp.stack([lambda_q1[0], lambda_k1[0], lambda_q2[0], lambda_k2[0]])
    subg = diff_subln_g[0][None, :]
    wts = (w_ret_branch[0].astype(BF16), w_diff_branch[0].astype(BF16), w_o[0].astype(BF16),
           norm2_g[0][None, :], w_ffn_up[0].astype(BF16), w_ffn_down[0].astype(BF16),
           normf_g[None, :])

    kpad = ((past + dseq + LANES - 1) // LANES) * LANES
    both_maps = lambda ix: np.concatenate([ix.T, ix.T], axis=1)
    idx_diag = both_maps(_bias_idx(np.arange(TQ), np.arange(TK), TK))
    idx_sub = both_maps(_bias_idx(np.arange(TQ) + TK, np.arange(TK), TK))
    idx_meta = np.concatenate(
        [both_maps(_bias_idx(np.arange(TQ) + b * TQ, np.arange(N_META) - N_META, N_META))
         for b in range(2)], axis=0)
    idx_samp = _bias_idx(past + np.arange(dseq), np.arange(kpad), past + dseq)
    t_diag, t_sub, t_meta, t_samp = _bias_tiles(rel_bias, [idx_diag, idx_sub, idx_meta, idx_samp],
                                                [True, True, True, False])
    t_meta = t_meta.reshape(DIFF_HEADS, 2, N_META, 2 * TQ)

    cs_m, sn_m = _rotary_tables(np.arange(-N_META, 0))
    m_out = _in_proj(meta_tokens, g1, w_in_bf, cs_m, sn_m, N_META)
    zero_state = jnp.zeros((1, RET_HEADS, RET_DK, RET_DV), F32)
    _, s_meta = _retention(m_out[0], m_out[1], m_out[2], m_out[3], zero_state, 1, N_META,
                           with_output=False)
    mk = m_out[5]
    mv_aug = jnp.concatenate([m_out[7].reshape(N_META, DIFF_HEADS, DIFF_DV),
                              jnp.ones((N_META, DIFF_HEADS, ONES_ROWS), BF16)], axis=2)
    mvt = mv_aug.reshape(N_META, DIFF_HEADS * DV_AUG).T
    subg_b = jnp.broadcast_to(diff_subln_g[0][:, None], (DIFF_DV, TQ))

    cs_p, sn_p = _rotary_tables(np.arange(seq))
    x2d = x_prompt.reshape(bsz * seq, D_MODEL)
    meta_rows = lambda u: u.reshape(N_META * DIFF_HEADS, DIFF_DV)
    (y_ret, s_fin, dqt, dkb, dvt, gr, gd, k_rows, v_rows) = _prompt_proj(
        x2d, g1, w_in_bf, cs_p, sn_p, s_meta, meta_rows(m_out[6]), meta_rows(m_out[8]), bsz,
        TM_PROJ)
    y_diff = _diff_attn(dqt, dkb, dvt, mk, mvt, t_diag, t_sub, t_meta, lam_p, subg_b, bsz)
    y_prompt = _tail(x2d, y_ret, y_diff, gr, gd, *wts, TM_TAIL).reshape(bsz, seq, D_MODEL)

    cs_s, sn_s = _rotary_tables(np.tile(past + np.arange(dseq), dbsz))
    xs2d = x_sample.reshape(dbsz * dseq, D_MODEL)
    (rq, rk, rv, rg, dq, dkb, dkf_s, dvb, dvf_s, gr, gd) = _in_proj(xs2d, g1, w_in_bf, cs_s, sn_s,
                                                                     dbsz * dseq)
    y_ret, s_samp = _retention(rq, rk, rv, rg, state_ret[0], dbsz, dseq)
    ck = cache_k[0].reshape(dbsz, past * DIFF_HEADS, DIFF_DV)
    cv = cache_v[0].reshape(dbsz, past * DIFF_HEADS, DIFF_DV)
    y_diff = _sample_attn(dq, ck, cv, dkb, dvb, t_samp, lam_p, subg, dbsz, past, dseq)
    y_sample = _tail(xs2d, y_ret, y_diff, gr, gd, *wts, dbsz * dseq).reshape(dbsz, dseq, D_MODEL)

    heads = (DIFF_HEADS, DIFF_DV)
    return (y_prompt, y_sample,
            k_rows.reshape((1, bsz, N_META + seq) + heads),
            v_rows.reshape((1, bsz, N_META + seq) + heads),
            s_fin[None],
            dkf_s.reshape((1, dbsz, dseq) + heads),
            dvf_s.reshape((1, dbsz, dseq) + heads),
            s_samp[None])
```

```python
import functools
import math

import numpy as np
import jax
import jax.numpy as jnp
from jax import lax
from jax.experimental import pallas as pl
from jax.experimental.pallas import tpu as pltpu

F32 = jnp.float32
BF16 = jnp.bfloat16

D_MODEL = 1024
CHUNK = 64
N_META = 16
RET_HEADS = 4
RET_DK = D_MODEL // 8
RET_DV = 2 * RET_DK
RET_QK_W = RET_HEADS * RET_DK
RET_V_W = RET_HEADS * RET_DV
ROPE_BASE = 10000.0
RET_EPS = 1e-6
DIFF_HEADS = 8
DIFF_DH = D_MODEL // 16
DIFF_DV = 2 * DIFF_DH
DIFF_W = DIFF_HEADS * DIFF_DV
DIFF_EPS = 1e-5
N_BUCKETS = 32
MAX_DISTANCE = 128
D_FF = ((8 * D_MODEL + 3 * 256 - 1) // (3 * 256)) * 256
NORM_EPS = 1e-6
NEG_INF = -1e30
LAM_INIT = 0.8 - 0.6 * math.exp(-0.3 * 0)
LOG2E = math.log2(math.e)
Q_SCALE = DIFF_DH ** -0.5 * LOG2E
ONES_ROWS = 16
DV_AUG = DIFF_DV + ONES_ROWS

OFF_RQ = 0
OFF_RK = OFF_RQ + RET_QK_W
OFF_RV = OFF_RK + RET_QK_W
OFF_RG = OFF_RV + RET_V_W
OFF_DQ = OFF_RG + RET_V_W
OFF_DK = OFF_DQ + DIFF_W
OFF_DV = OFF_DK + DIFF_W
OFF_GR = OFF_DV + DIFF_W
OFF_GD = OFF_GR + D_MODEL
W_IN = OFF_GD + D_MODEL

LANES = 128
VMEM_LIMIT_BYTES = 56 * 1024 * 1024
TM_PROJ = 256
TM_TAIL = 256
TQ = 256
TK = 256
HEADS_PER_STEP = 4
FF_CHUNK = 1024

NT_DIMS = (((1,), (1,)), ((), ()))
TN_DIMS = (((0,), (0,)), ((), ()))


def _cparams(n_axes):
    return pltpu.CompilerParams(dimension_semantics=("arbitrary",) * n_axes,
                                vmem_limit_bytes=VMEM_LIMIT_BYTES)


def _resident(shape):
    nd = len(shape)
    return pl.BlockSpec(shape, lambda *_: (0,) * nd, pipeline_mode=pl.Buffered(1))


def _t5_bucket_np(rel):
    nb = N_BUCKETS // 2
    max_exact = nb // 2
    ret = np.where(rel > 0, nb, 0)
    n = np.abs(rel)
    nf = np.maximum(n, max_exact).astype(np.float64)
    large = max_exact + (np.log(nf / max_exact) / math.log(MAX_DISTANCE / max_exact)
                         * (nb - max_exact)).astype(np.int32)
    large = np.minimum(large, nb - 1)
    return (ret + np.where(n < max_exact, n, large)).astype(np.int32)


def _bias_idx(qpos, kpos, n_real):
    qpos = np.asarray(qpos)[:, None]
    kpos = np.asarray(kpos)[None, :]
    vis = (np.floor_divide(kpos, CHUNK) <= np.floor_divide(qpos, CHUNK))
    vis = vis & (np.arange(kpos.shape[1])[None, :] < n_real)
    return np.where(vis, _t5_bucket_np(kpos - qpos), -1).astype(np.int32)


def _rotary_tables(pos):
    half = RET_DK // 2
    freq = 1.0 / (ROPE_BASE ** jnp.linspace(0.0, 1.0, half, dtype=F32))
    ang = jnp.asarray(pos, F32)[:, None] * freq[None, :]
    cos, sin = jnp.cos(ang), jnp.sin(ang)
    return jnp.concatenate([cos, cos], axis=1), jnp.concatenate([-sin, sin], axis=1)


def _retention_tables(t):
    gam = 1.0 - 2.0 ** (-5.0 - np.arange(RET_HEADS, dtype=np.float64))
    lg = np.log(gam)[:, None, None]
    n = np.arange(t)[:, None]
    m = np.arange(t)[None, :]
    cn, cm = n // CHUNK, m // CHUNK
    expo = np.where(cm == cn, np.abs(n - m), n - m).astype(np.float64)[None]
    dmat = np.where((cm <= cn)[None], np.exp(lg * expo), 0.0)
    qdec = np.exp(lg[:, :, 0] * (np.arange(t)[None, :] + 1.0))[:, :, None]
    kdec = np.exp(lg[:, :, 0] * (t - 1.0 - np.arange(t)[None, :]))[:, :, None]
    sdec = np.exp(lg[:, 0, 0] * t)
    return (jnp.asarray(dmat, F32), jnp.asarray(qdec, F32), jnp.asarray(kdec, F32),
            tuple(float(s) for s in sdec))


def _bias_kernel(rb_ref, *refs, n_tiles, shifted):
    idx_refs, out_refs = refs[:n_tiles], refs[n_tiles:]
    h = pl.program_id(0)
    far = rb_ref[N_BUCKETS // 2 - 1, h]
    for idx_ref, out_ref, sh in zip(idx_refs, out_refs, shifted):
        idx = idx_ref[...]
        acc = jnp.full(idx.shape, NEG_INF, F32)
        for b in range(N_BUCKETS):
            val = (rb_ref[b, h] - far if sh else rb_ref[b, h]) * LOG2E
            acc = jnp.where(idx == b, val, acc)
        out_ref[0] = acc


def _bias_tiles(rel_bias, idx_list, shifted):
    n = len(idx_list)
    in_specs = [pl.BlockSpec(memory_space=pltpu.SMEM)]
    in_specs += [pl.BlockSpec(ix.shape, lambda h: (0, 0)) for ix in idx_list]
    out_specs = [pl.BlockSpec((1,) + ix.shape, lambda h: (h, 0, 0)) for ix in idx_list]
    out_shape = [jax.ShapeDtypeStruct((DIFF_HEADS,) + ix.shape, F32) for ix in idx_list]
    return pl.pallas_call(
        functools.partial(_bias_kernel, n_tiles=n, shifted=tuple(shifted)),
        grid=(DIFF_HEADS,), in_specs=in_specs, out_specs=out_specs, out_shape=out_shape,
        compiler_params=_cparams(1), name="bias_tiles",
    )(rel_bias, *[jnp.asarray(ix) for ix in idx_list])


def _norm_proj(x_ref, g_ref, w_ref):
    x = x_ref[...]
    xn = x * lax.rsqrt(jnp.mean(x * x, axis=-1, keepdims=True) + NORM_EPS) * g_ref[...]
    xn = xn.astype(BF16)
    return lambda lo, n: jnp.dot(xn, w_ref[:, lo:lo + n], preferred_element_type=F32)


def _rotary(u, cs, sn):
    return u * cs + pltpu.roll(u, RET_DK // 2, 1) * sn


def _retention_head(qf, kf, v, gate, state, dm, qd, kd, sdec, with_output=True):
    y = None
    if with_output:
        s = lax.dot_general(qf.astype(BF16), kf.astype(BF16), NT_DIMS,
                            preferred_element_type=F32) * dm
        o = jnp.dot(s.astype(BF16), v, preferred_element_type=F32)
        o = o + jnp.dot((qf * qd).astype(BF16), state.astype(BF16), preferred_element_type=F32)
        y = o * lax.rsqrt(jnp.mean(o * o, axis=-1, keepdims=True) + RET_EPS)
        y = y * (gate * jax.nn.sigmoid(gate))
    kv = lax.dot_general((kf * kd).astype(BF16), v, TN_DIMS, preferred_element_type=F32)
    return y, sdec * state + kv


def _inproj_kernel(x_ref, g_ref, w_ref, cs_ref, sn_ref,
                   rq_ref, rk_ref, rv_ref, rg_ref, dq_ref, dkb_ref, dkf_ref,
                   dvb_ref, dvf_ref, gr_ref, gd_ref):
    proj = _norm_proj(x_ref, g_ref, w_ref)
    cs = cs_ref[...]
    sn = sn_ref[...]
    uq = proj(OFF_RQ, RET_QK_W)
    uk = proj(OFF_RK, RET_QK_W)
    for hh in range(RET_HEADS):
        sl = slice(hh * RET_DK, (hh + 1) * RET_DK)
        rq_ref[:, sl] = (_rotary(uq[:, sl], cs, sn) * (RET_DK ** -0.5)).astype(BF16)
        rk_ref[:, sl] = _rotary(uk[:, sl], cs, sn).astype(BF16)
    rv_ref[...] = proj(OFF_RV, RET_V_W).astype(BF16)
    rg_ref[...] = proj(OFF_RG, RET_V_W)
    dq_ref[...] = (proj(OFF_DQ, DIFF_W) * Q_SCALE).astype(BF16)
    dk = proj(OFF_DK, DIFF_W)
    dkf_ref[...] = dk
    dkb_ref[...] = dk.astype(BF16)
    dv = proj(OFF_DV, DIFF_W)
    dvf_ref[...] = dv
    dvb_ref[...] = dv.astype(BF16)
    gr_ref[...] = proj(OFF_GR, D_MODEL)
    gd_ref[...] = proj(OFF_GD, D_MODEL)


def _in_proj(x2d, g, w_bf, cs, sn, tm):
    rows = x2d.shape[0]
    n_pos = cs.shape[0] // tm
    row = lambda w: pl.BlockSpec((tm, w), lambda i: (i, 0))
    pos = pl.BlockSpec((tm, LANES), lambda i: (i % n_pos, 0))
    outs = [(RET_QK_W, BF16), (RET_QK_W, BF16), (RET_V_W, BF16), (RET_V_W, F32),
            (DIFF_W, BF16), (DIFF_W, BF16), (DIFF_W, F32), (DIFF_W, BF16), (DIFF_W, F32),
            (D_MODEL, F32), (D_MODEL, F32)]
    return pl.pallas_call(
        _inproj_kernel, grid=(rows // tm,),
        in_specs=[row(D_MODEL), _resident((1, D_MODEL)), _resident((D_MODEL, W_IN)), pos, pos],
        out_specs=[row(w) for w, _ in outs],
        out_shape=[jax.ShapeDtypeStruct((rows, w), dt) for w, dt in outs],
        compiler_params=_cparams(1), name="in_proj",
    )(x2d, g, w_bf, cs, sn)


def _prompt_proj_kernel(x_ref, g_ref, w_ref, cs_ref, sn_ref, s0_ref, dm_ref, qd_ref, kd_ref,
                        mk_ref, mv_ref,
                        yret_ref, sfin_ref, dqt_ref, dkb_ref, dvt_ref, gr_ref, gd_ref,
                        krows_ref, vrows_ref,
                        s_scr, stage_scr, row_sem, meta_sem, *, tiles_per_seq, n_steps, sdec):
    i = pl.program_id(0)
    tm = x_ref.shape[0]
    proj = _norm_proj(x_ref, g_ref, w_ref)
    cs = cs_ref[...]
    sn = sn_ref[...]
    slot = i % 2
    stream = i // tiles_per_seq
    rows_out = (krows_ref, vrows_ref)

    def rows_copy(which, step):
        first = (N_META + (step % tiles_per_seq) * tm) * DIFF_HEADS
        return pltpu.make_async_copy(
            stage_scr.at[step % 2, which],
            rows_out[which].at[step // tiles_per_seq, pl.ds(first, tm * DIFF_HEADS), :],
            row_sem.at[step % 2, which])

    def meta_copy(which):
        return pltpu.make_async_copy(
            (mk_ref, mv_ref)[which],
            rows_out[which].at[stream, pl.ds(0, N_META * DIFF_HEADS), :], meta_sem.at[which])

    @pl.when(i % tiles_per_seq == 0)
    def _():
        s_scr[...] = s0_ref[0]
        for which in range(2):
            meta_copy(which).start()

    @pl.when(i >= 2)
    def _():
        for which in range(2):
            rows_copy(which, i - 2).wait()

    uq = proj(OFF_RQ, RET_QK_W)
    uk = proj(OFF_RK, RET_QK_W)
    rv = proj(OFF_RV, RET_V_W).astype(BF16)
    rg = proj(OFF_RG, RET_V_W)
    for hh in range(RET_HEADS):
        qs = slice(hh * RET_DK, (hh + 1) * RET_DK)
        vs = slice(hh * RET_DV, (hh + 1) * RET_DV)
        y, state = _retention_head(_rotary(uq[:, qs], cs, sn) * (RET_DK ** -0.5),
                                   _rotary(uk[:, qs], cs, sn), rv[:, vs], rg[:, vs],
                                   s_scr[hh], dm_ref[hh], qd_ref[hh], kd_ref[hh], sdec[hh])
        yret_ref[:, vs] = y.astype(BF16)
        s_scr[hh] = state

    @pl.when(i % tiles_per_seq == tiles_per_seq - 1)
    def _():
        sfin_ref[0] = s_scr[...]

    dqt_ref[0] = (proj(OFF_DQ, DIFF_W) * Q_SCALE).T.astype(BF16)
    dk = proj(OFF_DK, DIFF_W)
    dkb_ref[...] = dk.astype(BF16)
    dv = proj(OFF_DV, DIFF_W)
    dvt = dv.T.astype(BF16)
    ones = jnp.ones((ONES_ROWS, dvt.shape[1]), BF16)
    for hh in range(DIFF_HEADS):
        hs = slice(hh * DIFF_DV, (hh + 1) * DIFF_DV)
        dvt_ref[0, hh * DV_AUG:hh * DV_AUG + DIFF_DV, :] = dvt[hs]
        dvt_ref[0, hh * DV_AUG + DIFF_DV:(hh + 1) * DV_AUG, :] = ones
        head_rows = pl.ds(hh, tm, stride=DIFF_HEADS)
        stage_scr[slot, 0, head_rows, :] = dk[:, hs]
        stage_scr[slot, 1, head_rows, :] = dv[:, hs]
    for which in range(2):
        rows_copy(which, i).start()
    gr_ref[...] = proj(OFF_GR, D_MODEL)
    gd_ref[...] = proj(OFF_GD, D_MODEL)

    @pl.when(i % tiles_per_seq == 0)
    def _():
        for which in range(2):
            meta_copy(which).wait()

    @pl.when(i == n_steps - 1)
    def _():
        for which in range(2):
            if n_steps >= 2:
                rows_copy(which, i - 1).wait()
            rows_copy(which, i).wait()


def _prompt_proj(x2d, g, w_bf, cs, sn, s0, mk_rows, mv_rows, n_seq, tm):
    rows = x2d.shape[0]
    seq = rows // n_seq
    tps = seq // tm
    dmat, qdec, kdec, sdec = _retention_tables(tm)
    row = lambda w: pl.BlockSpec((tm, w), lambda i: (i, 0))
    pos = pl.BlockSpec((tm, LANES), lambda i: (i % tps, 0))
    tposed = lambda w: pl.BlockSpec((1, w, tm), lambda i: (i, 0, 0))
    hbm = pl.BlockSpec(memory_space=pl.ANY)
    s_shape = (1, RET_HEADS, RET_DK, RET_DV)
    wv = DIFF_HEADS * DV_AUG
    rows_shape = jax.ShapeDtypeStruct((n_seq, (N_META + seq) * DIFF_HEADS, DIFF_DV), F32)
    return pl.pallas_call(
        functools.partial(_prompt_proj_kernel, tiles_per_seq=tps, n_steps=rows // tm, sdec=sdec),
        grid=(rows // tm,),
        in_specs=[row(D_MODEL), _resident((1, D_MODEL)), _resident((D_MODEL, W_IN)), pos, pos,
                  _resident(s_shape), _resident(dmat.shape), _resident(qdec.shape),
                  _resident(kdec.shape), _resident(mk_rows.shape), _resident(mv_rows.shape)],
        out_specs=[row(RET_V_W), pl.BlockSpec(s_shape, lambda i: (i // tps, 0, 0, 0)),
                   tposed(DIFF_W), row(DIFF_W), tposed(wv), row(D_MODEL), row(D_MODEL),
                   hbm, hbm],
        out_shape=[jax.ShapeDtypeStruct((rows, RET_V_W), BF16),
                   jax.ShapeDtypeStruct((n_seq,) + s_shape[1:], F32),
                   jax.ShapeDtypeStruct((rows // tm, DIFF_W, tm), BF16),
                   jax.ShapeDtypeStruct((rows, DIFF_W), BF16),
                   jax.ShapeDtypeStruct((rows // tm, wv, tm), BF16),
                   jax.ShapeDtypeStruct((rows, D_MODEL), F32),
                   jax.ShapeDtypeStruct((rows, D_MODEL), F32),
                   rows_shape, rows_shape],
        scratch_shapes=[pltpu.VMEM(s_shape[1:], F32),
                        pltpu.VMEM((2, 2, tm * DIFF_HEADS, DIFF_DV), F32),
                        pltpu.SemaphoreType.DMA((2, 2)), pltpu.SemaphoreType.DMA((2,))],
        compiler_params=_cparams(1), name="prompt_proj",
    )(x2d, g, w_bf, cs, sn, s0, dmat, qdec, kdec, mk_rows, mv_rows)


def _retention_kernel(rq_ref, rk_ref, rv_ref, rg_ref, s0_ref, dm_ref, qd_ref, kd_ref,
                      y_ref, sfin_ref, s_scr, *, sdec, with_output):
    t = pl.program_id(1)

    @pl.when(t == 0)
    def _():
        s_scr[...] = s0_ref[0]

    for hh in range(RET_HEADS):
        qs = slice(hh * RET_DK, (hh + 1) * RET_DK)
        vs = slice(hh * RET_DV, (hh + 1) * RET_DV)
        y, state = _retention_head(rq_ref[:, qs].astype(F32), rk_ref[:, qs].astype(F32),
                                   rv_ref[:, vs], rg_ref[:, vs], s_scr[hh], dm_ref[hh],
                                   qd_ref[hh], kd_ref[hh], sdec[hh], with_output)
        if with_output:
            y_ref[:, vs] = y.astype(BF16)
        else:
            y_ref[:, vs] = jnp.zeros((y_ref.shape[0], RET_DV), BF16)
        s_scr[hh] = state

    @pl.when(t == pl.num_programs(1) - 1)
    def _():
        sfin_ref[0] = s_scr[...]


def _retention(rq, rk, rv, rg, s0, n_seq, t, with_output=True):
    rows = rq.shape[0]
    nt = rows // n_seq // t
    dmat, qdec, kdec, sdec = _retention_tables(t)
    row = lambda w: pl.BlockSpec((t, w), lambda b, i: (b * nt + i, 0))
    s_shape = (1, RET_HEADS, RET_DK, RET_DV)
    if s0.shape[0] == 1:
        s0_spec = pl.BlockSpec(s_shape, lambda b, i: (0, 0, 0, 0))
    else:
        s0_spec = pl.BlockSpec(s_shape, lambda b, i: (b, 0, 0, 0))
    y, sfin = pl.pallas_call(
        functools.partial(_retention_kernel, sdec=sdec, with_output=with_output),
        grid=(n_seq, nt),
        in_specs=[row(RET_QK_W), row(RET_QK_W), row(RET_V_W), row(RET_V_W), s0_spec,
                  _resident(dmat.shape), _resident(qdec.shape), _resident(kdec.shape)],
        out_specs=[row(RET_V_W), pl.BlockSpec(s_shape, lambda b, i: (b, 0, 0, 0))],
        out_shape=[jax.ShapeDtypeStruct((rows, RET_V_W), BF16),
                   jax.ShapeDtypeStruct((n_seq,) + s_shape[1:], F32)],
        scratch_shapes=[pltpu.VMEM(s_shape[1:], F32)],
        compiler_params=_cparams(2), name="retention",
    )(rq, rk, rv, rg, s0, dmat, qdec, kdec)
    return y, sfin


def _lambda(lam_ref):
    lp = lam_ref[...]
    a = jnp.exp(jnp.sum(lp[0:1] * lp[1:2], axis=-1, keepdims=True))
    b = jnp.exp(jnp.sum(lp[2:3] * lp[3:4], axis=-1, keepdims=True))
    return a - b + LAM_INIT


def _stack_maps(q):
    qf = q.astype(F32)
    lane = lax.broadcasted_iota(jnp.int32, q.shape, 1)
    return jnp.concatenate([jnp.where(lane < DIFF_DH, qf, 0.0),
                            jnp.where(lane < DIFF_DH, 0.0, qf)], axis=0).astype(BF16)


def _mix_and_norm(o_all, lam, subg, n):
    o = o_all[:n] - lam * o_all[n:]
    y = o * lax.rsqrt(jnp.mean(o * o, axis=-1, keepdims=True) + DIFF_EPS) * subg
    return (y * (1.0 - LAM_INIT)).astype(BF16)


def _diff_attn_kernel(qt_ref, k_ref, vt_ref, mk_ref, mvt_ref, tdiag_ref, tsub_ref, tmeta_ref,
                      lam_ref, subg_ref, y_ref, qq_scr, m_scr, acc_scr, s_scr):
    i = pl.program_id(2)
    heads = range(HEADS_PER_STEP)
    hs = lambda h: slice(h * DIFF_DV, (h + 1) * DIFF_DV)
    vs = lambda h: slice(h * DV_AUG, (h + 1) * DV_AUG)

    for h in heads:
        qt = qt_ref[0, hs(h), :].astype(F32)
        row = lax.broadcasted_iota(jnp.int32, qt.shape, 0)
        qq_scr[h] = jnp.concatenate([jnp.where(row < DIFF_DH, qt, 0.0),
                                     jnp.where(row < DIFF_DH, 0.0, qt)], axis=1).astype(BF16)

    def scores(h, k_blk, tile):
        s = jnp.dot(k_blk, qq_scr[h], preferred_element_type=F32)
        return s if tile is None else s + tile

    for h in heads:
        m_scr[h] = jnp.full((1, 2 * TQ), NEG_INF, F32)
        acc_scr[h] = jnp.zeros((DV_AUG, 2 * TQ), F32)

    def meta_scores():
        return [scores(h, mk_ref[:, hs(h)], tmeta_ref[h, jnp.minimum(i, 1)]) for h in heads]

    def consume_blocks(blocks, s_meta=None):
        parts = [s for _, s in blocks] + ([s_meta] if s_meta is not None else [])
        s = [jnp.concatenate([part[h] for part in parts], axis=0) if len(parts) > 1
             else parts[0][h] for h in heads]
        m_prev = [m_scr[h] for h in heads]
        m_new = [jnp.maximum(m_prev[h], jnp.max(s[h], axis=0, keepdims=True)) for h in heads]
        p = [jnp.exp2(s[h] - m_new[h]).astype(BF16) for h in heads]
        alpha = [jnp.exp2(m_prev[h] - m_new[h]) for h in heads]
        for h in heads:
            m_scr[h] = m_new[h]
        pv = []
        for h in heads:
            terms = [jnp.dot(vt_ref[j, vs(h), :], p[h][n * TK:(n + 1) * TK],
                             preferred_element_type=F32) for n, (j, _) in enumerate(blocks)]
            if s_meta is not None:
                terms.append(jnp.dot(mvt_ref[vs(h), :], p[h][len(blocks) * TK:],
                                     preferred_element_type=F32))
            pv.append(functools.reduce(lambda a, b: a + b, terms))
        for h in heads:
            acc_scr[h] = alpha[h] * acc_scr[h] + pv[h]

    def consume(j, s):
        consume_blocks([(j, s)])

    def qk_to_scratch(j, buf, tiles=None):
        off = pl.multiple_of(j * TK, TK)
        for h in heads:
            s_scr[buf, h] = scores(h, k_ref[pl.ds(off, TK), hs(h)],
                                   None if tiles is None else tiles[h])

    def from_scratch(buf):
        return [s_scr[buf, h] for h in heads]

    def pair_steps(j):
        qk_to_scratch(j + 1, 1)
        consume(j, from_scratch(0))
        qk_to_scratch(j + 2, 0)
        consume(j + 1, from_scratch(1))

    def far_pairs(n):
        def body(t, carry):
            for k in range(n):
                pair_steps(2 * (n * t + k))
            return carry
        return body

    j_sub = jnp.maximum(i - 1, 0)
    n_pairs = j_sub // 2
    qk_to_scratch(0, 0)
    lax.fori_loop(0, n_pairs // 4, far_pairs(4), 0)
    lax.fori_loop(2 * (n_pairs // 4), n_pairs // 2, far_pairs(2), 0)
    lax.fori_loop(2 * (n_pairs // 2), n_pairs, far_pairs(1), 0)

    def finish():
        lam = _lambda(lam_ref)
        for h in heads:
            o_all = acc_scr[h, :DIFF_DV, :] / acc_scr[h, DIFF_DV:DIFF_DV + 1, :]
            o = o_all[:, :TQ] - lam * o_all[:, TQ:]
            y = o * lax.rsqrt(jnp.mean(o * o, axis=0, keepdims=True) + DIFF_EPS) * subg_ref[...]
            y_ref[:, hs(h)] = (y * (1.0 - LAM_INIT)).T.astype(BF16)

    @pl.when(j_sub % 2 == 1)
    def _():
        s_meta = meta_scores()
        qk_to_scratch(j_sub, 1, tsub_ref)
        consume(j_sub - 1, from_scratch(0))
        qk_to_scratch(i, 0, tdiag_ref)
        consume_blocks([(j_sub, from_scratch(1)), (i, from_scratch(0))], s_meta)
        finish()

    @pl.when(j_sub % 2 == 0)
    def _():
        s_meta = meta_scores()
        qk_to_scratch(i, 1, tdiag_ref)
        pen = jnp.where(i == 0, 3.0 * NEG_INF, 0.0)
        consume_blocks([(j_sub, [s_scr[0, h] + (tsub_ref[h] + pen) for h in heads]),
                        (i, from_scratch(1))], s_meta)
        finish()


def _diff_attn(dqt, dk, dvt, mk, mvt, tdiag, tsub, tmeta, lam_p, subg_b, n_seq):
    rows = dk.shape[0]
    seq = rows // n_seq
    nq = seq // TQ
    hb = HEADS_PER_STEP
    wid = hb * DIFF_DV
    wid_v = hb * DV_AUG
    tile3 = lambda b, g, i: (g, 0, 0)
    slow = lambda shape, imap: pl.BlockSpec(shape, imap, pipeline_mode=pl.Buffered(1))
    return pl.pallas_call(
        _diff_attn_kernel, grid=(n_seq, DIFF_HEADS // hb, nq),
        in_specs=[pl.BlockSpec((1, wid, TQ), lambda b, g, i: (b * nq + i, g, 0)),
                  pl.BlockSpec((seq, wid), lambda b, g, i: (b, g)),
                  pl.BlockSpec((nq, wid_v, TK), lambda b, g, i: (b, g, 0)),
                  slow((N_META, wid), lambda b, g, i: (0, g)),
                  slow((wid_v, N_META), lambda b, g, i: (g, 0)),
                  slow((hb, TK, 2 * TQ), tile3),
                  slow((hb, TK, 2 * TQ), tile3),
                  slow((hb, 2, N_META, 2 * TQ), lambda b, g, i: (g, 0, 0, 0)),
                  slow((4, DIFF_DH), lambda b, g, i: (0, 0)),
                  slow((DIFF_DV, TQ), lambda b, g, i: (0, 0))],
        out_specs=pl.BlockSpec((TQ, wid), lambda b, g, i: (b * nq + i, g)),
        out_shape=jax.ShapeDtypeStruct((rows, DIFF_W), BF16),
        scratch_shapes=[pltpu.VMEM((hb, DIFF_DV, 2 * TQ), BF16),
                        pltpu.VMEM((hb, 1, 2 * TQ), F32),
                        pltpu.VMEM((hb, DV_AUG, 2 * TQ), F32),
                        pltpu.VMEM((2, hb, TK, 2 * TQ), F32)],
        compiler_params=_cparams(3), name="diff_attn",
    )(dqt, dk, dvt, mk, mvt, tdiag, tsub, tmeta, lam_p, subg_b)


def _sample_attn_kernel(q_ref, ck_ref, cv_ref, nk_ref, nv_ref, tile_ref, lam_ref, subg_ref,
                        y_ref, k_scr, v_scr, *, past, n_new):
    pad = k_scr.shape[1] - past - n_new
    lam = _lambda(lam_ref)
    for h in range(DIFF_HEADS):
        hs = slice(h * DIFF_DV, (h + 1) * DIFF_DV)
        k_scr[h, 0:past] = ck_ref[0, pl.ds(h, past, stride=DIFF_HEADS), :].astype(BF16)
        v_scr[h, 0:past] = cv_ref[0, pl.ds(h, past, stride=DIFF_HEADS), :].astype(BF16)
        k_scr[h, past:past + n_new] = nk_ref[:, hs]
        v_scr[h, past:past + n_new] = nv_ref[:, hs]
        k_scr[h, past + n_new:] = jnp.zeros((pad, LANES), BF16)
        v_scr[h, past + n_new:] = jnp.zeros((pad, LANES), BF16)
        qq = _stack_maps(q_ref[:, hs])
        s = lax.dot_general(qq, k_scr[h], NT_DIMS, preferred_element_type=F32)
        s = (s.reshape(2, n_new, s.shape[-1]) + tile_ref[h][None]).reshape(2 * n_new, s.shape[-1])
        m = jnp.max(s, axis=1, keepdims=True)
        p = jnp.exp2(s - m)
        l = jnp.sum(p, axis=1, keepdims=True)
        o_all = jnp.dot(p.astype(BF16), v_scr[h], preferred_element_type=F32) / l
        y_ref[:, hs] = _mix_and_norm(o_all, lam, subg_ref[...], n_new)


def _sample_attn(dq, ck, cv, nk, nv, tile, lam_p, subg, n_seq, past, n_new):
    kpad = tile.shape[-1]
    row = pl.BlockSpec((n_new, DIFF_W), lambda b: (b, 0))
    cache = pl.BlockSpec((1, past * DIFF_HEADS, DIFF_DV), lambda b: (b, 0, 0))
    return pl.pallas_call(
        functools.partial(_sample_attn_kernel, past=past, n_new=n_new),
        grid=(n_seq,),
        in_specs=[row, cache, cache, row, row, _resident(tile.shape),
                  _resident((4, DIFF_DH)), _resident((1, DIFF_DV))],
        out_specs=row,
        out_shape=jax.ShapeDtypeStruct((n_seq * n_new, DIFF_W), BF16),
        scratch_shapes=[pltpu.VMEM((DIFF_HEADS, kpad, LANES), BF16),
                        pltpu.VMEM((DIFF_HEADS, kpad, LANES), BF16)],
        compiler_params=_cparams(1), name="sample_attn",
    )(dq, ck, cv, nk, nv, tile, lam_p, subg)


def _tail_kernel(h_ref, yr_ref, yd_ref, gr_ref, gd_ref, wrb_ref, wdb_ref, wo_ref, n2_ref,
                 wup_ref, wdn_ref, nf_ref, out_ref):
    a = jnp.dot(yr_ref[...], wrb_ref[...], preferred_element_type=F32)
    b = jnp.dot(yd_ref[...], wdb_ref[...], preferred_element_type=F32)
    merged = jax.nn.sigmoid(gr_ref[...]) * a + jax.nn.sigmoid(gd_ref[...]) * b
    h = h_ref[...] + jnp.dot(merged.astype(BF16), wo_ref[...], preferred_element_type=F32)
    xn = h * lax.rsqrt(jnp.mean(h * h, axis=-1, keepdims=True) + NORM_EPS) * n2_ref[...]
    xn = xn.astype(BF16)
    acc = jnp.zeros_like(h)
    for lo in range(0, D_FF, FF_CHUNK):
        n = min(FF_CHUNK, D_FF - lo)
        gate = jnp.dot(xn, wup_ref[:, lo:lo + n], preferred_element_type=F32)
        up = jnp.dot(xn, wup_ref[:, D_FF + lo:D_FF + lo + n], preferred_element_type=F32)
        act = (gate * jax.nn.sigmoid(gate) * up).astype(BF16)
        acc = acc + jnp.dot(act, wdn_ref[lo:lo + n, :], preferred_element_type=F32)
    h = h + acc
    out_ref[...] = h * lax.rsqrt(jnp.mean(h * h, axis=-1, keepdims=True) + NORM_EPS) * nf_ref[...]


def _tail(h2d, y_ret, y_diff, gr, gd, wrb, wdb, wo, n2, wup, wdn, nf, tm):
    rows = h2d.shape[0]
    row = pl.BlockSpec((tm, D_MODEL), lambda i: (i, 0))
    return pl.pallas_call(
        _tail_kernel, grid=(rows // tm,),
        in_specs=[row, row, row, row, row,
                  _resident(wrb.shape), _resident(wdb.shape), _resident(wo.shape),
                  _resident(n2.shape), _resident(wup.shape), _resident(wdn.shape),
                  _resident(nf.shape)],
        out_specs=row,
        out_shape=jax.ShapeDtypeStruct((rows, D_MODEL), F32),
        compiler_params=_cparams(1), name="tail",
    )(h2d, y_ret, y_diff, gr, gd, wrb, wdb, wo, n2, wup, wdn, nf)


def kernel(x_prompt, x_sample, cache_k, cache_v, state_ret, meta_tokens, rel_bias, norm1_g, w_in,
           lambda_q1, lambda_k1, lambda_q2, lambda_k2, diff_subln_g, w_ret_branch, w_diff_branch,
           w_o, norm2_g, w_ffn_up, w_ffn_down, normf_g):
    assert w_in.shape[0] == 1, "single-layer step only"
    bsz, seq, _ = x_prompt.shape
    dbsz, dseq, _ = x_sample.shape
    past = cache_k.shape[2]
    assert TQ == TK == TM_PROJ and TQ % CHUNK == 0 and DIFF_HEADS % HEADS_PER_STEP == 0
    assert seq % TQ == 0
    assert dseq <= CHUNK and past % CHUNK == 0 and meta_tokens.shape[0] == N_META

    w_in_bf = w_in[0].astype(BF16)
    g1 = norm1_g[0][None, :]
    lam_p = jnp.stack([lambda_q1[0], lambda_k1[0], lambda_q2[0], lambda_k2[0]])
    subg = diff_subln_g[0][None, :]
    wts = (w_ret_branch[0].astype(BF16), w_diff_branch[0].astype(BF16), w_o[0].astype(BF16),
           norm2_g[0][None, :], w_ffn_up[0].astype(BF16), w_ffn_down[0].astype(BF16),
           normf_g[None, :])

    kpad = ((past + dseq + LANES - 1) // LANES) * LANES
    both_maps = lambda ix: np.concatenate([ix.T, ix.T], axis=1)
    idx_diag = both_maps(_bias_idx(np.arange(TQ), np.arange(TK), TK))
    idx_sub = both_maps(_bias_idx(np.arange(TQ) + TK, np.arange(TK), TK))
    idx_meta = np.concatenate(
        [both_maps(_bias_idx(np.arange(TQ) + b * TQ, np.arange(N_META) - N_META, N_META))
         for b in range(2)], axis=0)
    idx_samp = _bias_idx(past + np.arange(dseq), np.arange(kpad), past + dseq)
    t_diag, t_sub, t_meta, t_samp = _bias_tiles(rel_bias, [idx_diag, idx_sub, idx_meta, idx_samp],
                                                [True, True, True, False])
    t_meta = t_meta.reshape(DIFF_HEADS, 2, N_META, 2 * TQ)

    cs_m, sn_m = _rotary_tables(np.arange(-N_META, 0))
    m_out = _in_proj(meta_tokens, g1, w_in_bf, cs_m, sn_m, N_META)
    zero_state = jnp.zeros((1, RET_HEADS, RET_DK, RET_DV), F32)
    _, s_meta = _retention(m_out[0], m_out[1], m_out[2], m_out[3], zero_state, 1, N_META,
                           with_output=False)
    mk = m_out[5]
    mv_aug = jnp.concatenate([m_out[7].reshape(N_META, DIFF_HEADS, DIFF_DV),
                              jnp.ones((N_META, DIFF_HEADS, ONES_ROWS), BF16)], axis=2)
    mvt = mv_aug.reshape(N_META, DIFF_HEADS * DV_AUG).T
    subg_b = jnp.broadcast_to(diff_subln_g[0][:, None], (DIFF_DV, TQ))

    cs_p, sn_p = _rotary_tables(np.arange(seq))
    x2d = x_prompt.reshape(bsz * seq, D_MODEL)
    meta_rows = lambda u: u.reshape(N_META * DIFF_HEADS, DIFF_DV)
    (y_ret, s_fin, dqt, dkb, dvt, gr, gd, k_rows, v_rows) = _prompt_proj(
        x2d, g1, w_in_bf, cs_p, sn_p, s_meta, meta_rows(m_out[6]), meta_rows(m_out[8]), bsz,
        TM_PROJ)
    y_diff = _diff_attn(dqt, dkb, dvt, mk, mvt, t_diag, t_sub, t_meta, lam_p, subg_b, bsz)
    y_prompt = _tail(x2d, y_ret, y_diff, gr, gd, *wts, TM_TAIL).reshape(bsz, seq, D_MODEL)

    cs_s, sn_s = _rotary_tables(np.tile(past + np.arange(dseq), dbsz))
    xs2d = x_sample.reshape(dbsz * dseq, D_MODEL)
    (rq, rk, rv, rg, dq, dkb, dkf_s, dvb, dvf_s, gr, gd) = _in_proj(xs2d, g1, w_in_bf, cs_s, sn_s,
                                                                     dbsz * dseq)
    y_ret, s_samp = _retention(rq, rk, rv, rg, state_ret[0], dbsz, dseq)
    ck = cache_k[0].reshape(dbsz, past * DIFF_HEADS, DIFF_DV)
    cv = cache_v[0].reshape(dbsz, past * DIFF_HEADS, DIFF_DV)
    y_diff = _sample_attn(dq, ck, cv, dkb, dvb, t_samp, lam_p, subg, dbsz, past, dseq)
    y_sample = _tail(xs2d, y_ret, y_diff, gr, gd, *wts, dbsz * dseq).reshape(dbsz, dseq, D_MODEL)

    heads = (DIFF_HEADS, DIFF_DV)
    return (y_prompt, y_sample,
            k_rows.reshape((1, bsz, N_META + seq) + heads),
            v_rows.reshape((1, bsz, N_META + seq) + heads),
            s_fin[None],
            dkf_s.reshape((1, dbsz, dseq) + heads),
            dvf_s.reshape((1, dbsz, dseq) + heads),
            s_samp[None])
```

```python
import functools
import math

import numpy as np
import jax
import jax.numpy as jnp
from jax import lax
from jax.experimental import pallas as pl
from jax.experimental.pallas import tpu as pltpu

F32 = jnp.float32
BF16 = jnp.bfloat16

D_MODEL = 1024
CHUNK = 64
N_META = 16
RET_HEADS = 4
RET_DK = D_MODEL // 8
RET_DV = 2 * RET_DK
RET_QK_W = RET_HEADS * RET_DK
RET_V_W = RET_HEADS * RET_DV
ROPE_BASE = 10000.0
RET_EPS = 1e-6
DIFF_HEADS = 8
DIFF_DH = D_MODEL // 16
DIFF_DV = 2 * DIFF_DH
DIFF_W = DIFF_HEADS * DIFF_DV
DIFF_EPS = 1e-5
N_BUCKETS = 32
MAX_DISTANCE = 128
D_FF = ((8 * D_MODEL + 3 * 256 - 1) // (3 * 256)) * 256
NORM_EPS = 1e-6
NEG_INF = -1e30
LAM_INIT = 0.8 - 0.6 * math.exp(-0.3 * 0)
LOG2E = math.log2(math.e)
Q_SCALE = DIFF_DH ** -0.5 * LOG2E
ONES_ROWS = 16
DV_AUG = DIFF_DV + ONES_ROWS

OFF_RQ = 0
OFF_RK = OFF_RQ + RET_QK_W
OFF_RV = OFF_RK + RET_QK_W
OFF_RG = OFF_RV + RET_V_W
OFF_DQ = OFF_RG + RET_V_W
OFF_DK = OFF_DQ + DIFF_W
OFF_DV = OFF_DK + DIFF_W
OFF_GR = OFF_DV + DIFF_W
OFF_GD = OFF_GR + D_MODEL
W_IN = OFF_GD + D_MODEL

LANES = 128
VMEM_LIMIT_BYTES = 56 * 1024 * 1024
TM_PROJ = 256
TM_TAIL = 256
TQ = 256
TK = 256
HEADS_PER_STEP = 4
FAR_LOOP_PAIRS = (8, 4, 2, 1)
FF_CHUNK = 1024

NT_DIMS = (((1,), (1,)), ((), ()))
TN_DIMS = (((0,), (0,)), ((), ()))


def _cparams(n_axes):
    return pltpu.CompilerParams(dimension_semantics=("arbitrary",) * n_axes,
                                vmem_limit_bytes=VMEM_LIMIT_BYTES)


def _resident(shape):
    nd = len(shape)
    return pl.BlockSpec(shape, lambda *_: (0,) * nd, pipeline_mode=pl.Buffered(1))


def _t5_bucket_np(rel):
    nb = N_BUCKETS // 2
    max_exact = nb // 2
    ret = np.where(rel > 0, nb, 0)
    n = np.abs(rel)
    nf = np.maximum(n, max_exact).astype(np.float64)
    large = max_exact + (np.log(nf / max_exact) / math.log(MAX_DISTANCE / max_exact)
                         * (nb - max_exact)).astype(np.int32)
    large = np.minimum(large, nb - 1)
    return (ret + np.where(n < max_exact, n, large)).astype(np.int32)


def _bias_idx(qpos, kpos, n_real):
    qpos = np.asarray(qpos)[:, None]
    kpos = np.asarray(kpos)[None, :]
    vis = (np.floor_divide(kpos, CHUNK) <= np.floor_divide(qpos, CHUNK))
    vis = vis & (np.arange(kpos.shape[1])[None, :] < n_real)
    return np.where(vis, _t5_bucket_np(kpos - qpos), -1).astype(np.int32)


def _rotary_tables(pos):
    half = RET_DK // 2
    freq = 1.0 / (ROPE_BASE ** jnp.linspace(0.0, 1.0, half, dtype=F32))
    ang = jnp.asarray(pos, F32)[:, None] * freq[None, :]
    cos, sin = jnp.cos(ang), jnp.sin(ang)
    return jnp.concatenate([cos, cos], axis=1), jnp.concatenate([-sin, sin], axis=1)


def _retention_tables(t):
    gam = 1.0 - 2.0 ** (-5.0 - np.arange(RET_HEADS, dtype=np.float64))
    lg = np.log(gam)[:, None, None]
    n = np.arange(t)[:, None]
    m = np.arange(t)[None, :]
    cn, cm = n // CHUNK, m // CHUNK
    expo = np.where(cm == cn, np.abs(n - m), n - m).astype(np.float64)[None]
    dmat = np.where((cm <= cn)[None], np.exp(lg * expo), 0.0)
    qdec = np.exp(lg[:, :, 0] * (np.arange(t)[None, :] + 1.0))[:, :, None]
    kdec = np.exp(lg[:, :, 0] * (t - 1.0 - np.arange(t)[None, :]))[:, :, None]
    sdec = np.exp(lg[:, 0, 0] * t)
    return (jnp.asarray(dmat, F32), jnp.asarray(qdec, F32), jnp.asarray(kdec, F32),
            tuple(float(s) for s in sdec))


def _bias_kernel(rb_ref, *refs, n_tiles, shifted):
    idx_refs, out_refs = refs[:n_tiles], refs[n_tiles:]
    h = pl.program_id(0)
    far = rb_ref[N_BUCKETS // 2 - 1, h]
    for idx_ref, out_ref, sh in zip(idx_refs, out_refs, shifted):
        idx = idx_ref[...]
        acc = jnp.full(idx.shape, NEG_INF, F32)
        for b in range(N_BUCKETS):
            val = (rb_ref[b, h] - far if sh else rb_ref[b, h]) * LOG2E
            acc = jnp.where(idx == b, val, acc)
        out_ref[0] = acc


def _bias_tiles(rel_bias, idx_list, shifted):
    n = len(idx_list)
    in_specs = [pl.BlockSpec(memory_space=pltpu.SMEM)]
    in_specs += [pl.BlockSpec(ix.shape, lambda h: (0, 0)) for ix in idx_list]
    out_specs = [pl.BlockSpec((1,) + ix.shape, lambda h: (h, 0, 0)) for ix in idx_list]
    out_shape = [jax.ShapeDtypeStruct((DIFF_HEADS,) + ix.shape, F32) for ix in idx_list]
    return pl.pallas_call(
        functools.partial(_bias_kernel, n_tiles=n, shifted=tuple(shifted)),
        grid=(DIFF_HEADS,), in_specs=in_specs, out_specs=out_specs, out_shape=out_shape,
        compiler_params=_cparams(1), name="bias_tiles",
    )(rel_bias, *[jnp.asarray(ix) for ix in idx_list])


def _norm_proj(x_ref, g_ref, w_ref):
    x = x_ref[...]
    xn = x * lax.rsqrt(jnp.mean(x * x, axis=-1, keepdims=True) + NORM_EPS) * g_ref[...]
    xn = xn.astype(BF16)
    return lambda lo, n: jnp.dot(xn, w_ref[:, lo:lo + n], preferred_element_type=F32)


def _rotary(u, cs, sn):
    return u * cs + pltpu.roll(u, RET_DK // 2, 1) * sn


def _retention_head(qf, kf, v, gate, state, dm, qd, kd, sdec, with_output=True):
    y = None
    if with_output:
        s = lax.dot_general(qf.astype(BF16), kf.astype(BF16), NT_DIMS,
                            preferred_element_type=F32) * dm
        o = jnp.dot(s.astype(BF16), v, preferred_element_type=F32)
        o = o + jnp.dot((qf * qd).astype(BF16), state.astype(BF16), preferred_element_type=F32)
        y = o * lax.rsqrt(jnp.mean(o * o, axis=-1, keepdims=True) + RET_EPS)
        y = y * (gate * jax.nn.sigmoid(gate))
    kv = lax.dot_general((kf * kd).astype(BF16), v, TN_DIMS, preferred_element_type=F32)
    return y, sdec * state + kv


def _inproj_kernel(x_ref, g_ref, w_ref, cs_ref, sn_ref,
                   rq_ref, rk_ref, rv_ref, rg_ref, dq_ref, dkb_ref, dkf_ref,
                   dvb_ref, dvf_ref, gr_ref, gd_ref):
    proj = _norm_proj(x_ref, g_ref, w_ref)
    cs = cs_ref[...]
    sn = sn_ref[...]
    uq = proj(OFF_RQ, RET_QK_W)
    uk = proj(OFF_RK, RET_QK_W)
    for hh in range(RET_HEADS):
        sl = slice(hh * RET_DK, (hh + 1) * RET_DK)
        rq_ref[:, sl] = (_rotary(uq[:, sl], cs, sn) * (RET_DK ** -0.5)).astype(BF16)
        rk_ref[:, sl] = _rotary(uk[:, sl], cs, sn).astype(BF16)
    rv_ref[...] = proj(OFF_RV, RET_V_W).astype(BF16)
    rg_ref[...] = proj(OFF_RG, RET_V_W)
    dq_ref[...] = (proj(OFF_DQ, DIFF_W) * Q_SCALE).astype(BF16)
    dk = proj(OFF_DK, DIFF_W)
    dkf_ref[...] = dk
    dkb_ref[...] = dk.astype(BF16)
    dv = proj(OFF_DV, DIFF_W)
    dvf_ref[...] = dv
    dvb_ref[...] = dv.astype(BF16)
    gr_ref[...] = proj(OFF_GR, D_MODEL)
    gd_ref[...] = proj(OFF_GD, D_MODEL)


def _in_proj(x2d, g, w_bf, cs, sn, tm):
    rows = x2d.shape[0]
    n_pos = cs.shape[0] // tm
    row = lambda w: pl.BlockSpec((tm, w), lambda i: (i, 0))
    pos = pl.BlockSpec((tm, LANES), lambda i: (i % n_pos, 0))
    outs = [(RET_QK_W, BF16), (RET_QK_W, BF16), (RET_V_W, BF16), (RET_V_W, F32),
            (DIFF_W, BF16), (DIFF_W, BF16), (DIFF_W, F32), (DIFF_W, BF16), (DIFF_W, F32),
            (D_MODEL, F32), (D_MODEL, F32)]
    return pl.pallas_call(
        _inproj_kernel, grid=(rows // tm,),
        in_specs=[row(D_MODEL), _resident((1, D_MODEL)), _resident((D_MODEL, W_IN)), pos, pos],
        out_specs=[row(w) for w, _ in outs],
        out_shape=[jax.ShapeDtypeStruct((rows, w), dt) for w, dt in outs],
        compiler_params=_cparams(1), name="in_proj",
    )(x2d, g, w_bf, cs, sn)


def _prompt_proj_kernel(x_ref, g_ref, w_ref, cs_ref, sn_ref, s0_ref, dm_ref, qd_ref, kd_ref,
                        mk_ref, mv_ref,
                        yret_ref, sfin_ref, dqt_ref, dkb_ref, dvt_ref, gr_ref, gd_ref,
                        krows_ref, vrows_ref,
                        s_scr, stage_scr, row_sem, meta_sem, *, tiles_per_seq, n_steps, sdec):
    i = pl.program_id(0)
    tm = x_ref.shape[0]
    proj = _norm_proj(x_ref, g_ref, w_ref)
    cs = cs_ref[...]
    sn = sn_ref[...]
    slot = i % 2
    stream = i // tiles_per_seq
    rows_out = (krows_ref, vrows_ref)

    def rows_copy(which, step):
        first = (N_META + (step % tiles_per_seq) * tm) * DIFF_HEADS
        return pltpu.make_async_copy(
            stage_scr.at[step % 2, which],
            rows_out[which].at[step // tiles_per_seq, pl.ds(first, tm * DIFF_HEADS), :],
            row_sem.at[step % 2, which])

    def meta_copy(which):
        return pltpu.make_async_copy(
            (mk_ref, mv_ref)[which],
            rows_out[which].at[stream, pl.ds(0, N_META * DIFF_HEADS), :], meta_sem.at[which])

    @pl.when(i % tiles_per_seq == 0)
    def _():
        s_scr[...] = s0_ref[0]
        for which in range(2):
            meta_copy(which).start()

    @pl.when(i >= 2)
    def _():
        for which in range(2):
            rows_copy(which, i - 2).wait()

    uq = proj(OFF_RQ, RET_QK_W)
    uk = proj(OFF_RK, RET_QK_W)
    rv = proj(OFF_RV, RET_V_W).astype(BF16)
    rg = proj(OFF_RG, RET_V_W)
    for hh in range(RET_HEADS):
        qs = slice(hh * RET_DK, (hh + 1) * RET_DK)
        vs = slice(hh * RET_DV, (hh + 1) * RET_DV)
        y, state = _retention_head(_rotary(uq[:, qs], cs, sn) * (RET_DK ** -0.5),
                                   _rotary(uk[:, qs], cs, sn), rv[:, vs], rg[:, vs],
                                   s_scr[hh], dm_ref[hh], qd_ref[hh], kd_ref[hh], sdec[hh])
        yret_ref[:, vs] = y.astype(BF16)
        s_scr[hh] = state

    @pl.when(i % tiles_per_seq == tiles_per_seq - 1)
    def _():
        sfin_ref[0] = s_scr[...]

    dqt_ref[0] = (proj(OFF_DQ, DIFF_W) * Q_SCALE).T.astype(BF16)
    dk = proj(OFF_DK, DIFF_W)
    dkb_ref[...] = dk.astype(BF16)
    dv = proj(OFF_DV, DIFF_W)
    dvt = dv.T.astype(BF16)
    ones = jnp.ones((ONES_ROWS, dvt.shape[1]), BF16)
    for hh in range(DIFF_HEADS):
        hs = slice(hh * DIFF_DV, (hh + 1) * DIFF_DV)
        dvt_ref[0, hh * DV_AUG:hh * DV_AUG + DIFF_DV, :] = dvt[hs]
        dvt_ref[0, hh * DV_AUG + DIFF_DV:(hh + 1) * DV_AUG, :] = ones
        head_rows = pl.ds(hh, tm, stride=DIFF_HEADS)
        stage_scr[slot, 0, head_rows, :] = dk[:, hs]
        stage_scr[slot, 1, head_rows, :] = dv[:, hs]
    for which in range(2):
        rows_copy(which, i).start()
    gr_ref[...] = proj(OFF_GR, D_MODEL)
    gd_ref[...] = proj(OFF_GD, D_MODEL)

    @pl.when(i % tiles_per_seq == 0)
    def _():
        for which in range(2):
            meta_copy(which).wait()

    @pl.when(i == n_steps - 1)
    def _():
        for which in range(2):
            if n_steps >= 2:
                rows_copy(which, i - 1).wait()
            rows_copy(which, i).wait()


def _prompt_proj(x2d, g, w_bf, cs, sn, s0, mk_rows, mv_rows, n_seq, tm):
    rows = x2d.shape[0]
    seq = rows // n_seq
    tps = seq // tm
    dmat, qdec, kdec, sdec = _retention_tables(tm)
    row = lambda w: pl.BlockSpec((tm, w), lambda i: (i, 0))
    pos = pl.BlockSpec((tm, LANES), lambda i: (i % tps, 0))
    tposed = lambda w: pl.BlockSpec((1, w, tm), lambda i: (i, 0, 0))
    hbm = pl.BlockSpec(memory_space=pl.ANY)
    s_shape = (1, RET_HEADS, RET_DK, RET_DV)
    wv = DIFF_HEADS * DV_AUG
    rows_shape = jax.ShapeDtypeStruct((n_seq, (N_META + seq) * DIFF_HEADS, DIFF_DV), F32)
    return pl.pallas_call(
        functools.partial(_prompt_proj_kernel, tiles_per_seq=tps, n_steps=rows // tm, sdec=sdec),
        grid=(rows // tm,),
        in_specs=[row(D_MODEL), _resident((1, D_MODEL)), _resident((D_MODEL, W_IN)), pos, pos,
                  _resident(s_shape), _resident(dmat.shape), _resident(qdec.shape),
                  _resident(kdec.shape), _resident(mk_rows.shape), _resident(mv_rows.shape)],
        out_specs=[row(RET_V_W), pl.BlockSpec(s_shape, lambda i: (i // tps, 0, 0, 0)),
                   tposed(DIFF_W), row(DIFF_W), tposed(wv), row(D_MODEL), row(D_MODEL),
                   hbm, hbm],
        out_shape=[jax.ShapeDtypeStruct((rows, RET_V_W), BF16),
                   jax.ShapeDtypeStruct((n_seq,) + s_shape[1:], F32),
                   jax.ShapeDtypeStruct((rows // tm, DIFF_W, tm), BF16),
                   jax.ShapeDtypeStruct((rows, DIFF_W), BF16),
                   jax.ShapeDtypeStruct((rows // tm, wv, tm), BF16),
                   jax.ShapeDtypeStruct((rows, D_MODEL), F32),
                   jax.ShapeDtypeStruct((rows, D_MODEL), F32),
                   rows_shape, rows_shape],
        scratch_shapes=[pltpu.VMEM(s_shape[1:], F32),
                        pltpu.VMEM((2, 2, tm * DIFF_HEADS, DIFF_DV), F32),
                        pltpu.SemaphoreType.DMA((2, 2)), pltpu.SemaphoreType.DMA((2,))],
        compiler_params=_cparams(1), name="prompt_proj",
    )(x2d, g, w_bf, cs, sn, s0, dmat, qdec, kdec, mk_rows, mv_rows)


def _retention_kernel(rq_ref, rk_ref, rv_ref, rg_ref, s0_ref, dm_ref, qd_ref, kd_ref,
                      y_ref, sfin_ref, s_scr, *, sdec, with_output):
    t = pl.program_id(1)

    @pl.when(t == 0)
    def _():
        s_scr[...] = s0_ref[0]

    for hh in range(RET_HEADS):
        qs = slice(hh * RET_DK, (hh + 1) * RET_DK)
        vs = slice(hh * RET_DV, (hh + 1) * RET_DV)
        y, state = _retention_head(rq_ref[:, qs].astype(F32), rk_ref[:, qs].astype(F32),
                                   rv_ref[:, vs], rg_ref[:, vs], s_scr[hh], dm_ref[hh],
                                   qd_ref[hh], kd_ref[hh], sdec[hh], with_output)
        if with_output:
            y_ref[:, vs] = y.astype(BF16)
        else:
            y_ref[:, vs] = jnp.zeros((y_ref.shape[0], RET_DV), BF16)
        s_scr[hh] = state

    @pl.when(t == pl.num_programs(1) - 1)
    def _():
        sfin_ref[0] = s_scr[...]


def _retention(rq, rk, rv, rg, s0, n_seq, t, with_output=True):
    rows = rq.shape[0]
    nt = rows // n_seq // t
    dmat, qdec, kdec, sdec = _retention_tables(t)
    row = lambda w: pl.BlockSpec((t, w), lambda b, i: (b * nt + i, 0))
    s_shape = (1, RET_HEADS, RET_DK, RET_DV)
    if s0.shape[0] == 1:
        s0_spec = pl.BlockSpec(s_shape, lambda b, i: (0, 0, 0, 0))
    else:
        s0_spec = pl.BlockSpec(s_shape, lambda b, i: (b, 0, 0, 0))
    y, sfin = pl.pallas_call(
        functools.partial(_retention_kernel, sdec=sdec, with_output=with_output),
        grid=(n_seq, nt),
        in_specs=[row(RET_QK_W), row(RET_QK_W), row(RET_V_W), row(RET_V_W), s0_spec,
                  _resident(dmat.shape), _resident(qdec.shape), _resident(kdec.shape)],
        out_specs=[row(RET_V_W), pl.BlockSpec(s_shape, lambda b, i: (b, 0, 0, 0))],
        out_shape=[jax.ShapeDtypeStruct((rows, RET_V_W), BF16),
                   jax.ShapeDtypeStruct((n_seq,) + s_shape[1:], F32)],
        scratch_shapes=[pltpu.VMEM(s_shape[1:], F32)],
        compiler_params=_cparams(2), name="retention",
    )(rq, rk, rv, rg, s0, dmat, qdec, kdec)
    return y, sfin


def _lambda(lam_ref):
    lp = lam_ref[...]
    a = jnp.exp(jnp.sum(lp[0:1] * lp[1:2], axis=-1, keepdims=True))
    b = jnp.exp(jnp.sum(lp[2:3] * lp[3:4], axis=-1, keepdims=True))
    return a - b + LAM_INIT


def _stack_maps(q):
    qf = q.astype(F32)
    lane = lax.broadcasted_iota(jnp.int32, q.shape, 1)
    return jnp.concatenate([jnp.where(lane < DIFF_DH, qf, 0.0),
                            jnp.where(lane < DIFF_DH, 0.0, qf)], axis=0).astype(BF16)


def _mix_and_norm(o_all, lam, subg, n):
    o = o_all[:n] - lam * o_all[n:]
    y = o * lax.rsqrt(jnp.mean(o * o, axis=-1, keepdims=True) + DIFF_EPS) * subg
    return (y * (1.0 - LAM_INIT)).astype(BF16)


def _diff_attn_kernel(qt_ref, k_ref, vt_ref, mk_ref, mvt_ref, tdiag_ref, tsub_ref, tmeta_ref,
                      lam_ref, subg_ref, y_ref, qq_scr, m_scr, acc_scr, s_scr):
    i = pl.program_id(2)
    heads = range(HEADS_PER_STEP)
    hs = lambda h: slice(h * DIFF_DV, (h + 1) * DIFF_DV)
    vs = lambda h: slice(h * DV_AUG, (h + 1) * DV_AUG)

    for h in heads:
        qt = qt_ref[0, hs(h), :].astype(F32)
        row = lax.broadcasted_iota(jnp.int32, qt.shape, 0)
        qq_scr[h] = jnp.concatenate([jnp.where(row < DIFF_DH, qt, 0.0),
                                     jnp.where(row < DIFF_DH, 0.0, qt)], axis=1).astype(BF16)

    def scores(h, k_blk, tile):
        s = jnp.dot(k_blk, qq_scr[h], preferred_element_type=F32)
        return s if tile is None else s + tile

    for h in heads:
        m_scr[h] = jnp.full((1, 2 * TQ), NEG_INF, F32)
        acc_scr[h] = jnp.zeros((DV_AUG, 2 * TQ), F32)

    def meta_scores():
        return [scores(h, mk_ref[:, hs(h)], tmeta_ref[h, jnp.minimum(i, 1)]) for h in heads]

    def consume_blocks(blocks, s_meta=None):
        parts = [s for _, s in blocks] + ([s_meta] if s_meta is not None else [])
        s = [jnp.concatenate([part[h] for part in parts], axis=0) if len(parts) > 1
             else parts[0][h] for h in heads]
        for h in heads:
            m_prev = m_scr[h]
            m_new = jnp.maximum(m_prev, jnp.max(s[h], axis=0, keepdims=True))
            p = jnp.exp2(s[h] - m_new).astype(BF16)
            alpha = jnp.exp2(m_prev - m_new)
            m_scr[h] = m_new
            terms = [jnp.dot(vt_ref[j, vs(h), :], p[n * TK:(n + 1) * TK],
                             preferred_element_type=F32) for n, (j, _) in enumerate(blocks)]
            if s_meta is not None:
                terms.append(jnp.dot(mvt_ref[vs(h), :], p[len(blocks) * TK:],
                                     preferred_element_type=F32))
            acc_scr[h] = alpha * acc_scr[h] + functools.reduce(lambda a, b: a + b, terms)

    def consume(j, s):
        consume_blocks([(j, s)])

    def qk_to_scratch(j, buf, tiles=None):
        off = pl.multiple_of(j * TK, TK)
        for h in heads:
            s_scr[buf, h] = scores(h, k_ref[pl.ds(off, TK), hs(h)],
                                   None if tiles is None else tiles[h])

    def from_scratch(buf):
        return [s_scr[buf, h] for h in heads]

    def pair_steps(j):
        qk_to_scratch(j + 1, 1)
        consume(j, from_scratch(0))
        qk_to_scratch(j + 2, 0)
        consume(j + 1, from_scratch(1))

    def far_pairs(n, first_pair):
        def body(t, carry):
            for k in range(n):
                pair_steps(2 * (first_pair + n * t + k))
            return carry
        return body

    j_sub = jnp.maximum(i - 1, 0)
    n_pairs = j_sub // 2
    qk_to_scratch(0, 0)
    done = 0
    for n in FAR_LOOP_PAIRS:
        trips = (n_pairs - done) // n
        lax.fori_loop(0, trips, far_pairs(n, done), 0)
        done = done + trips * n

    def finish():
        lam = _lambda(lam_ref)
        for h in heads:
            o_all = acc_scr[h, :DIFF_DV, :] / acc_scr[h, DIFF_DV:DIFF_DV + 1, :]
            o = o_all[:, :TQ] - lam * o_all[:, TQ:]
            y = o * lax.rsqrt(jnp.mean(o * o, axis=0, keepdims=True) + DIFF_EPS) * subg_ref[...]
            y_ref[:, hs(h)] = (y * (1.0 - LAM_INIT)).T.astype(BF16)

    @pl.when(j_sub % 2 == 1)
    def _():
        s_meta = meta_scores()
        qk_to_scratch(j_sub, 1, tsub_ref)
        consume(j_sub - 1, from_scratch(0))
        qk_to_scratch(i, 0, tdiag_ref)
        consume_blocks([(j_sub, from_scratch(1)), (i, from_scratch(0))], s_meta)
        finish()

    @pl.when(j_sub % 2 == 0)
    def _():
        s_meta = meta_scores()
        qk_to_scratch(i, 1, tdiag_ref)
        pen = jnp.where(i == 0, 3.0 * NEG_INF, 0.0)
        consume_blocks([(j_sub, [s_scr[0, h] + (tsub_ref[h] + pen) for h in heads]),
                        (i, from_scratch(1))], s_meta)
        finish()


def _diff_attn(dqt, dk, dvt, mk, mvt, tdiag, tsub, tmeta, lam_p, subg_b, n_seq):
    rows = dk.shape[0]
    seq = rows // n_seq
    nq = seq // TQ
    hb = HEADS_PER_STEP
    wid = hb * DIFF_DV
    wid_v = hb * DV_AUG
    tile3 = lambda b, g, i: (g, 0, 0)
    slow = lambda shape, imap: pl.BlockSpec(shape, imap, pipeline_mode=pl.Buffered(1))
    return pl.pallas_call(
        _diff_attn_kernel, grid=(n_seq, DIFF_HEADS // hb, nq),
        in_specs=[pl.BlockSpec((1, wid, TQ), lambda b, g, i: (b * nq + i, g, 0)),
                  pl.BlockSpec((seq, wid), lambda b, g, i: (b, g)),
                  pl.BlockSpec((nq, wid_v, TK), lambda b, g, i: (b, g, 0)),
                  slow((N_META, wid), lambda b, g, i: (0, g)),
                  slow((wid_v, N_META), lambda b, g, i: (g, 0)),
                  slow((hb, TK, 2 * TQ), tile3),
                  slow((hb, TK, 2 * TQ), tile3),
                  slow((hb, 2, N_META, 2 * TQ), lambda b, g, i: (g, 0, 0, 0)),
                  slow((4, DIFF_DH), lambda b, g, i: (0, 0)),
                  slow((DIFF_DV, TQ), lambda b, g, i: (0, 0))],
        out_specs=pl.BlockSpec((TQ, wid), lambda b, g, i: (b * nq + i, g)),
        out_shape=jax.ShapeDtypeStruct((rows, DIFF_W), BF16),
        scratch_shapes=[pltpu.VMEM((hb, DIFF_DV, 2 * TQ), BF16),
                        pltpu.VMEM((hb, 1, 2 * TQ), F32),
                        pltpu.VMEM((hb, DV_AUG, 2 * TQ), F32),
                        pltpu.VMEM((2, hb, TK, 2 * TQ), F32)],
        compiler_params=_cparams(3), name="diff_attn",
    )(dqt, dk, dvt, mk, mvt, tdiag, tsub, tmeta, lam_p, subg_b)


def _sample_attn_kernel(q_ref, ck_ref, cv_ref, nk_ref, nv_ref, tile_ref, lam_ref, subg_ref,
                        y_ref, k_scr, v_scr, *, past, n_new):
    pad = k_scr.shape[1] - past - n_new
    lam = _lambda(lam_ref)
    for h in range(DIFF_HEADS):
        hs = slice(h * DIFF_DV, (h + 1) * DIFF_DV)
        k_scr[h, 0:past] = ck_ref[0, pl.ds(h, past, stride=DIFF_HEADS), :].astype(BF16)
        v_scr[h, 0:past] = cv_ref[0, pl.ds(h, past, stride=DIFF_HEADS), :].astype(BF16)
        k_scr[h, past:past + n_new] = nk_ref[:, hs]
        v_scr[h, past:past + n_new] = nv_ref[:, hs]
        k_scr[h, past + n_new:] = jnp.zeros((pad, LANES), BF16)
        v_scr[h, past + n_new:] = jnp.zeros((pad, LANES), BF16)
        qq = _stack_maps(q_ref[:, hs])
        s = lax.dot_general(qq, k_scr[h], NT_DIMS, preferred_element_type=F32)
        s = (s.reshape(2, n_new, s.shape[-1]) + tile_ref[h][None]).reshape(2 * n_new, s.shape[-1])
        m = jnp.max(s, axis=1, keepdims=True)
        p = jnp.exp2(s - m)
        l = jnp.sum(p, axis=1, keepdims=True)
        o_all = jnp.dot(p.astype(BF16), v_scr[h], preferred_element_type=F32) / l
        y_ref[:, hs] = _mix_and_norm(o_all, lam, subg_ref[...], n_new)


def _sample_attn(dq, ck, cv, nk, nv, tile, lam_p, subg, n_seq, past, n_new):
    kpad = tile.shape[-1]
    row = pl.BlockSpec((n_new, DIFF_W), lambda b: (b, 0))
    cache = pl.BlockSpec((1, past * DIFF_HEADS, DIFF_DV), lambda b: (b, 0, 0))
    return pl.pallas_call(
        functools.partial(_sample_attn_kernel, past=past, n_new=n_new),
        grid=(n_seq,),
        in_specs=[row, cache, cache, row, row, _resident(tile.shape),
                  _resident((4, DIFF_DH)), _resident((1, DIFF_DV))],
        out_specs=row,
        out_shape=jax.ShapeDtypeStruct((n_seq * n_new, DIFF_W), BF16),
        scratch_shapes=[pltpu.VMEM((DIFF_HEADS, kpad, LANES), BF16),
                        pltpu.VMEM((DIFF_HEADS, kpad, LANES), BF16)],
        compiler_params=_cparams(1), name="sample_attn",
    )(dq, ck, cv, nk, nv, tile, lam_p, subg)


def _tail_kernel(h_ref, yr_ref, yd_ref, gr_ref, gd_ref, wrb_ref, wdb_ref, wo_ref, n2_ref,
                 wup_ref, wdn_ref, nf_ref, out_ref):
    a = jnp.dot(yr_ref[...], wrb_ref[...], preferred_element_type=F32)
    b = jnp.dot(yd_ref[...], wdb_ref[...], preferred_element_type=F32)
    merged = jax.nn.sigmoid(gr_ref[...]) * a + jax.nn.sigmoid(gd_ref[...]) * b
    h = h_ref[...] + jnp.dot(merged.astype(BF16), wo_ref[...], preferred_element_type=F32)
    xn = h * lax.rsqrt(jnp.mean(h * h, axis=-1, keepdims=True) + NORM_EPS) * n2_ref[...]
    xn = xn.astype(BF16)
    acc = jnp.zeros_like(h)
    for lo in range(0, D_FF, FF_CHUNK):
        n = min(FF_CHUNK, D_FF - lo)
        gate = jnp.dot(xn, wup_ref[:, lo:lo + n], preferred_element_type=F32)
        up = jnp.dot(xn, wup_ref[:, D_FF + lo:D_FF + lo + n], preferred_element_type=F32)
        act = (gate * jax.nn.sigmoid(gate) * up).astype(BF16)
        acc = acc + jnp.dot(act, wdn_ref[lo:lo + n, :], preferred_element_type=F32)
    h = h + acc
    out_ref[...] = h * lax.rsqrt(jnp.mean(h * h, axis=-1, keepdims=True) + NORM_EPS) * nf_ref[...]


def _tail(h2d, y_ret, y_diff, gr, gd, wrb, wdb, wo, n2, wup, wdn, nf, tm):
    rows = h2d.shape[0]
    row = pl.BlockSpec((tm, D_MODEL), lambda i: (i, 0))
    return pl.pallas_call(
        _tail_kernel, grid=(rows // tm,),
        in_specs=[row, row, row, row, row,
                  _resident(wrb.shape), _resident(wdb.shape), _resident(wo.shape),
                  _resident(n2.shape), _resident(wup.shape), _resident(wdn.shape),
                  _resident(nf.shape)],
        out_specs=row,
        out_shape=jax.ShapeDtypeStruct((rows, D_MODEL), F32),
        compiler_params=_cparams(1), name="tail",
    )(h2d, y_ret, y_diff, gr, gd, wrb, wdb, wo, n2, wup, wdn, nf)


def kernel(x_prompt, x_sample, cache_k, cache_v, state_ret, meta_tokens, rel_bias, norm1_g, w_in,
           lambda_q1, lambda_k1, lambda_q2, lambda_k2, diff_subln_g, w_ret_branch, w_diff_branch,
           w_o, norm2_g, w_ffn_up, w_ffn_down, normf_g):
    assert w_in.shape[0] == 1, "single-layer step only"
    bsz, seq, _ = x_prompt.shape
    dbsz, dseq, _ = x_sample.shape
    past = cache_k.shape[2]
    assert TQ == TK == TM_PROJ and TQ % CHUNK == 0 and DIFF_HEADS % HEADS_PER_STEP == 0
    assert seq % TQ == 0
    assert dseq <= CHUNK and past % CHUNK == 0 and meta_tokens.shape[0] == N_META

    w_in_bf = w_in[0].astype(BF16)
    g1 = norm1_g[0][None, :]
    lam_p = jnp.stack([lambda_q1[0], lambda_k1[0], lambda_q2[0], lambda_k2[0]])
    subg = diff_subln_g[0][None, :]
    wts = (w_ret_branch[0].astype(BF16), w_diff_branch[0].astype(BF16), w_o[0].astype(BF16),
           norm2_g[0][None, :], w_ffn_up[0].astype(BF16), w_ffn_down[0].astype(BF16),
           normf_g[None, :])

    kpad = ((past + dseq + LANES - 1) // LANES) * LANES
    both_maps = lambda ix: np.concatenate([ix.T, ix.T], axis=1)
    idx_diag = both_maps(_bias_idx(np.arange(TQ), np.arange(TK), TK))
    idx_sub = both_maps(_bias_idx(np.arange(TQ) + TK, np.arange(TK), TK))
    idx_meta = np.concatenate(
        [both_maps(_bias_idx(np.arange(TQ) + b * TQ, np.arange(N_META) - N_META, N_META))
         for b in range(2)], axis=0)
    idx_samp = _bias_idx(past + np.arange(dseq), np.arange(kpad), past + dseq)
    t_diag, t_sub, t_meta, t_samp = _bias_tiles(rel_bias, [idx_diag, idx_sub, idx_meta, idx_samp],
                                                [True, True, True, False])
    t_meta = t_meta.reshape(DIFF_HEADS, 2, N_META, 2 * TQ)

    cs_m, sn_m = _rotary_tables(np.arange(-N_META, 0))
    m_out = _in_proj(meta_tokens, g1, w_in_bf, cs_m, sn_m, N_META)
    zero_state = jnp.zeros((1, RET_HEADS, RET_DK, RET_DV), F32)
    _, s_meta = _retention(m_out[0], m_out[1], m_out[2], m_out[3], zero_state, 1, N_META,
                           with_output=False)
    mk = m_out[5]
    mv_aug = jnp.concatenate([m_out[7].reshape(N_META, DIFF_HEADS, DIFF_DV),
                              jnp.ones((N_META, DIFF_HEADS, ONES_ROWS), BF16)], axis=2)
    mvt = mv_aug.reshape(N_META, DIFF_HEADS * DV_AUG).T
    subg_b = jnp.broadcast_to(diff_subln_g[0][:, None], (DIFF_DV, TQ))

    cs_p, sn_p = _rotary_tables(np.arange(seq))
    x2d = x_prompt.reshape(bsz * seq, D_MODEL)
    meta_rows = lambda u: u.reshape(N_META * DIFF_HEADS, DIFF_DV)
    (y_ret, s_fin, dqt, dkb, dvt, gr, gd, k_rows, v_rows) = _prompt_proj(
        x2d, g1, w_in_bf, cs_p, sn_p, s_meta, meta_rows(m_out[6]), meta_rows(m_out[8]), bsz,
        TM_PROJ)
    y_diff = _diff_attn(dqt, dkb, dvt, mk, mvt, t_diag, t_sub, t_meta, lam_p, subg_b, bsz)
    y_prompt = _tail(x2d, y_ret, y_diff, gr, gd, *wts, TM_TAIL).reshape(bsz, seq, D_MODEL)

    cs_s, sn_s = _rotary_tables(np.tile(past + np.arange(dseq), dbsz))
    xs2d = x_sample.reshape(dbsz * dseq, D_MODEL)
    (rq, rk, rv, rg, dq, dkb, dkf_s, dvb, dvf_s, gr, gd) = _in_proj(xs2d, g1, w_in_bf, cs_s, sn_s,
                                                                     dbsz * dseq)
    y_ret, s_samp = _retention(rq, rk, rv, rg, state_ret[0], dbsz, dseq)
    ck = cache_k[0].reshape(dbsz, past * DIFF_HEADS, DIFF_DV)
    cv = cache_v[0].reshape(dbsz, past * DIFF_HEADS, DIFF_DV)
    y_diff = _sample_attn(dq, ck, cv, dkb, dvb, t_samp, lam_p, subg, dbsz, past, dseq)
    y_sample = _tail(xs2d, y_ret, y_diff, gr, gd, *wts, dbsz * dseq).reshape(dbsz, dseq, D_MODEL)

    heads = (DIFF_HEADS, DIFF_DV)
    return (y_prompt, y_sample,
            k_rows.reshape((1, bsz, N_META + seq) + heads),
            v_rows.reshape((1, bsz, N_META + seq) + heads),
            s_fin[None],
            dkf_s.reshape((1, dbsz, dseq) + heads),
            dvf_s.reshape((1, dbsz, dseq) + heads),
            s_samp[None])
```

```python
import functools
import math

import numpy as np
import jax
import jax.numpy as jnp
from jax import lax
from jax.experimental import pallas as pl
from jax.experimental.pallas import tpu as pltpu

F32 = jnp.float32
BF16 = jnp.bfloat16

D_MODEL = 1024
CHUNK = 64
N_META = 16
RET_HEADS = 4
RET_DK = D_MODEL // 8
RET_DV = 2 * RET_DK
RET_QK_W = RET_HEADS * RET_DK
RET_V_W = RET_HEADS * RET_DV
ROPE_BASE = 10000.0
RET_EPS = 1e-6
DIFF_HEADS = 8
DIFF_DH = D_MODEL // 16
DIFF_DV = 2 * DIFF_DH
DIFF_W = DIFF_HEADS * DIFF_DV
DIFF_EPS = 1e-5
N_BUCKETS = 32
MAX_DISTANCE = 128
D_FF = ((8 * D_MODEL + 3 * 256 - 1) // (3 * 256)) * 256
NORM_EPS = 1e-6
NEG_INF = -1e30
LAM_INIT = 0.8 - 0.6 * math.exp(-0.3 * 0)
LOG2E = math.log2(math.e)
Q_SCALE = DIFF_DH ** -0.5 * LOG2E
ONES_ROWS = 16
DV_AUG = DIFF_DV + ONES_ROWS

OFF_RQ = 0
OFF_RK = OFF_RQ + RET_QK_W
OFF_RV = OFF_RK + RET_QK_W
OFF_RG = OFF_RV + RET_V_W
OFF_DQ = OFF_RG + RET_V_W
OFF_DK = OFF_DQ + DIFF_W
OFF_DV = OFF_DK + DIFF_W
OFF_GR = OFF_DV + DIFF_W
OFF_GD = OFF_GR + D_MODEL
W_IN = OFF_GD + D_MODEL

LANES = 128
VMEM_LIMIT_BYTES = 56 * 1024 * 1024
TM_PROJ = 256
TM_TAIL = 256
TQ = 256
TK = 256
HEADS_PER_STEP = 4
FAR_LOOP_PAIRS = (8, 4, 2, 1)
FF_CHUNK = 1024

NT_DIMS = (((1,), (1,)), ((), ()))
TN_DIMS = (((0,), (0,)), ((), ()))


def _cparams(n_axes):
    return pltpu.CompilerParams(dimension_semantics=("arbitrary",) * n_axes,
                                vmem_limit_bytes=VMEM_LIMIT_BYTES)


def _resident(shape):
    nd = len(shape)
    return pl.BlockSpec(shape, lambda *_: (0,) * nd, pipeline_mode=pl.Buffered(1))


def _t5_bucket_np(rel):
    nb = N_BUCKETS // 2
    max_exact = nb // 2
    ret = np.where(rel > 0, nb, 0)
    n = np.abs(rel)
    nf = np.maximum(n, max_exact).astype(np.float64)
    large = max_exact + (np.log(nf / max_exact) / math.log(MAX_DISTANCE / max_exact)
                         * (nb - max_exact)).astype(np.int32)
    large = np.minimum(large, nb - 1)
    return (ret + np.where(n < max_exact, n, large)).astype(np.int32)


def _bias_idx(qpos, kpos, n_real):
    qpos = np.asarray(qpos)[:, None]
    kpos = np.asarray(kpos)[None, :]
    vis = (np.floor_divide(kpos, CHUNK) <= np.floor_divide(qpos, CHUNK))
    vis = vis & (np.arange(kpos.shape[1])[None, :] < n_real)
    return np.where(vis, _t5_bucket_np(kpos - qpos), -1).astype(np.int32)


def _rotary_tables(pos):
    half = RET_DK // 2
    freq = 1.0 / (ROPE_BASE ** jnp.linspace(0.0, 1.0, half, dtype=F32))
    ang = jnp.asarray(pos, F32)[:, None] * freq[None, :]
    cos, sin = jnp.cos(ang), jnp.sin(ang)
    return jnp.concatenate([cos, cos], axis=1), jnp.concatenate([-sin, sin], axis=1)


def _retention_tables(t):
    gam = 1.0 - 2.0 ** (-5.0 - np.arange(RET_HEADS, dtype=np.float64))
    lg = np.log(gam)[:, None, None]
    n = np.arange(t)[:, None]
    m = np.arange(t)[None, :]
    cn, cm = n // CHUNK, m // CHUNK
    expo = np.where(cm == cn, np.abs(n - m), n - m).astype(np.float64)[None]
    dmat = np.where((cm <= cn)[None], np.exp(lg * expo), 0.0)
    qdec = np.exp(lg[:, :, 0] * (np.arange(t)[None, :] + 1.0))[:, :, None]
    kdec = np.exp(lg[:, :, 0] * (t - 1.0 - np.arange(t)[None, :]))[:, :, None]
    sdec = np.exp(lg[:, 0, 0] * t)
    return (jnp.asarray(dmat, F32), jnp.asarray(qdec, F32), jnp.asarray(kdec, F32),
            tuple(float(s) for s in sdec))


def _bias_kernel(rb_ref, *refs, n_tiles, shifted):
    idx_refs, out_refs = refs[:n_tiles], refs[n_tiles:]
    h = pl.program_id(0)
    far = rb_ref[N_BUCKETS // 2 - 1, h]
    for idx_ref, out_ref, sh in zip(idx_refs, out_refs, shifted):
        idx = idx_ref[...]
        acc = jnp.full(idx.shape, NEG_INF, F32)
        for b in range(N_BUCKETS):
            val = (rb_ref[b, h] - far if sh else rb_ref[b, h]) * LOG2E
            acc = jnp.where(idx == b, val, acc)
        out_ref[0] = acc


def _bias_tiles(rel_bias, idx_list, shifted):
    n = len(idx_list)
    in_specs = [pl.BlockSpec(memory_space=pltpu.SMEM)]
    in_specs += [pl.BlockSpec(ix.shape, lambda h: (0, 0)) for ix in idx_list]
    out_specs = [pl.BlockSpec((1,) + ix.shape, lambda h: (h, 0, 0)) for ix in idx_list]
    out_shape = [jax.ShapeDtypeStruct((DIFF_HEADS,) + ix.shape, F32) for ix in idx_list]
    return pl.pallas_call(
        functools.partial(_bias_kernel, n_tiles=n, shifted=tuple(shifted)),
        grid=(DIFF_HEADS,), in_specs=in_specs, out_specs=out_specs, out_shape=out_shape,
        compiler_params=_cparams(1), name="bias_tiles",
    )(rel_bias, *[jnp.asarray(ix) for ix in idx_list])


def _norm_proj(x_ref, g_ref, w_ref):
    x = x_ref[...]
    xn = x * lax.rsqrt(jnp.mean(x * x, axis=-1, keepdims=True) + NORM_EPS) * g_ref[...]
    xn = xn.astype(BF16)
    return lambda lo, n: jnp.dot(xn, w_ref[:, lo:lo + n], preferred_element_type=F32)


def _rotary(u, cs, sn):
    return u * cs + pltpu.roll(u, RET_DK // 2, 1) * sn


def _retention_head(qf, kf, v, gate, state, dm, qd, kd, sdec, with_output=True):
    y = None
    if with_output:
        s = lax.dot_general(qf.astype(BF16), kf.astype(BF16), NT_DIMS,
                            preferred_element_type=F32) * dm
        o = jnp.dot(s.astype(BF16), v, preferred_element_type=F32)
        o = o + jnp.dot((qf * qd).astype(BF16), state.astype(BF16), preferred_element_type=F32)
        y = o * lax.rsqrt(jnp.mean(o * o, axis=-1, keepdims=True) + RET_EPS)
        y = y * (gate * jax.nn.sigmoid(gate))
    kv = lax.dot_general((kf * kd).astype(BF16), v, TN_DIMS, preferred_element_type=F32)
    return y, sdec * state + kv


def _inproj_kernel(x_ref, g_ref, w_ref, cs_ref, sn_ref,
                   rq_ref, rk_ref, rv_ref, rg_ref, dq_ref, dkb_ref, dkf_ref,
                   dvb_ref, dvf_ref, gr_ref, gd_ref):
    proj = _norm_proj(x_ref, g_ref, w_ref)
    cs = cs_ref[...]
    sn = sn_ref[...]
    uq = proj(OFF_RQ, RET_QK_W)
    uk = proj(OFF_RK, RET_QK_W)
    for hh in range(RET_HEADS):
        sl = slice(hh * RET_DK, (hh + 1) * RET_DK)
        rq_ref[:, sl] = (_rotary(uq[:, sl], cs, sn) * (RET_DK ** -0.5)).astype(BF16)
        rk_ref[:, sl] = _rotary(uk[:, sl], cs, sn).astype(BF16)
    rv_ref[...] = proj(OFF_RV, RET_V_W).astype(BF16)
    rg_ref[...] = proj(OFF_RG, RET_V_W)
    dq_ref[...] = (proj(OFF_DQ, DIFF_W) * Q_SCALE).astype(BF16)
    dk = proj(OFF_DK, DIFF_W)
    dkf_ref[...] = dk
    dkb_ref[...] = dk.astype(BF16)
    dv = proj(OFF_DV, DIFF_W)
    dvf_ref[...] = dv
    dvb_ref[...] = dv.astype(BF16)
    gr_ref[...] = proj(OFF_GR, D_MODEL)
    gd_ref[...] = proj(OFF_GD, D_MODEL)


def _in_proj(x2d, g, w_bf, cs, sn, tm):
    rows = x2d.shape[0]
    n_pos = cs.shape[0] // tm
    row = lambda w: pl.BlockSpec((tm, w), lambda i: (i, 0))
    pos = pl.BlockSpec((tm, LANES), lambda i: (i % n_pos, 0))
    outs = [(RET_QK_W, BF16), (RET_QK_W, BF16), (RET_V_W, BF16), (RET_V_W, F32),
            (DIFF_W, BF16), (DIFF_W, BF16), (DIFF_W, F32), (DIFF_W, BF16), (DIFF_W, F32),
            (D_MODEL, F32), (D_MODEL, F32)]
    return pl.pallas_call(
        _inproj_kernel, grid=(rows // tm,),
        in_specs=[row(D_MODEL), _resident((1, D_MODEL)), _resident((D_MODEL, W_IN)), pos, pos],
        out_specs=[row(w) for w, _ in outs],
        out_shape=[jax.ShapeDtypeStruct((rows, w), dt) for w, dt in outs],
        compiler_params=_cparams(1), name="in_proj",
    )(x2d, g, w_bf, cs, sn)


def _prompt_proj_kernel(x_ref, g_ref, w_ref, cs_ref, sn_ref, s0_ref, dm_ref, qd_ref, kd_ref,
                        mk_ref, mv_ref,
                        yret_ref, sfin_ref, dqt_ref, dkb_ref, dvt_ref, gr_ref, gd_ref,
                        krows_ref, vrows_ref,
                        s_scr, stage_scr, row_sem, meta_sem, *, tiles_per_seq, n_steps, sdec):
    i = pl.program_id(0)
    tm = x_ref.shape[0]
    proj = _norm_proj(x_ref, g_ref, w_ref)
    cs = cs_ref[...]
    sn = sn_ref[...]
    slot = i % 2
    stream = i // tiles_per_seq
    rows_out = (krows_ref, vrows_ref)

    def rows_copy(which, step):
        first = (N_META + (step % tiles_per_seq) * tm) * DIFF_HEADS
        return pltpu.make_async_copy(
            stage_scr.at[step % 2, which],
            rows_out[which].at[step // tiles_per_seq, pl.ds(first, tm * DIFF_HEADS), :],
            row_sem.at[step % 2, which])

    def meta_copy(which):
        return pltpu.make_async_copy(
            (mk_ref, mv_ref)[which],
            rows_out[which].at[stream, pl.ds(0, N_META * DIFF_HEADS), :], meta_sem.at[which])

    @pl.when(i % tiles_per_seq == 0)
    def _():
        s_scr[...] = s0_ref[0]
        for which in range(2):
            meta_copy(which).start()

    @pl.when(i >= 2)
    def _():
        for which in range(2):
            rows_copy(which, i - 2).wait()

    uq = proj(OFF_RQ, RET_QK_W)
    uk = proj(OFF_RK, RET_QK_W)
    rv = proj(OFF_RV, RET_V_W).astype(BF16)
    rg = proj(OFF_RG, RET_V_W)
    for hh in range(RET_HEADS):
        qs = slice(hh * RET_DK, (hh + 1) * RET_DK)
        vs = slice(hh * RET_DV, (hh + 1) * RET_DV)
        y, state = _retention_head(_rotary(uq[:, qs], cs, sn) * (RET_DK ** -0.5),
                                   _rotary(uk[:, qs], cs, sn), rv[:, vs], rg[:, vs],
                                   s_scr[hh], dm_ref[hh], qd_ref[hh], kd_ref[hh], sdec[hh])
        yret_ref[:, vs] = y.astype(BF16)
        s_scr[hh] = state

    dqt_ref[0] = (proj(OFF_DQ, DIFF_W) * Q_SCALE).T.astype(BF16)
    dk = proj(OFF_DK, DIFF_W)
    dkb_ref[...] = dk.astype(BF16)
    dv = proj(OFF_DV, DIFF_W)
    dvt = dv.T.astype(BF16)
    ones = jnp.ones((ONES_ROWS, dvt.shape[1]), BF16)
    for hh in range(DIFF_HEADS):
        hs = slice(hh * DIFF_DV, (hh + 1) * DIFF_DV)
        dvt_ref[0, hh * DV_AUG:hh * DV_AUG + DIFF_DV, :] = dvt[hs]
        dvt_ref[0, hh * DV_AUG + DIFF_DV:(hh + 1) * DV_AUG, :] = ones
        head_rows = pl.ds(hh, tm, stride=DIFF_HEADS)
        stage_scr[slot, 0, head_rows, :] = dk[:, hs]
        stage_scr[slot, 1, head_rows, :] = dv[:, hs]
    for which in range(2):
        rows_copy(which, i).start()
    gr_ref[...] = proj(OFF_GR, D_MODEL)
    gd_ref[...] = proj(OFF_GD, D_MODEL)

    @pl.when(i % tiles_per_seq == 0)
    def _():
        for which in range(2):
            meta_copy(which).wait()

    @pl.when(i % tiles_per_seq == tiles_per_seq - 1)
    def _():
        sfin_ref[0] = s_scr[...]

    @pl.when(i == n_steps - 1)
    def _():
        for which in range(2):
            if n_steps >= 2:
                rows_copy(which, i - 1).wait()
            rows_copy(which, i).wait()


def _prompt_proj(x2d, g, w_bf, cs, sn, s0, mk_rows, mv_rows, n_seq, tm):
    rows = x2d.shape[0]
    seq = rows // n_seq
    tps = seq // tm
    dmat, qdec, kdec, sdec = _retention_tables(tm)
    row = lambda w: pl.BlockSpec((tm, w), lambda i: (i, 0))
    pos = pl.BlockSpec((tm, LANES), lambda i: (i % tps, 0))
    tposed = lambda w: pl.BlockSpec((1, w, tm), lambda i: (i, 0, 0))
    hbm = pl.BlockSpec(memory_space=pl.ANY)
    s_shape = (1, RET_HEADS, RET_DK, RET_DV)
    wv = DIFF_HEADS * DV_AUG
    rows_shape = jax.ShapeDtypeStruct((n_seq, (N_META + seq) * DIFF_HEADS, DIFF_DV), F32)
    return pl.pallas_call(
        functools.partial(_prompt_proj_kernel, tiles_per_seq=tps, n_steps=rows // tm, sdec=sdec),
        grid=(rows // tm,),
        in_specs=[row(D_MODEL), _resident((1, D_MODEL)), _resident((D_MODEL, W_IN)), pos, pos,
                  _resident(s_shape), _resident(dmat.shape), _resident(qdec.shape),
                  _resident(kdec.shape), _resident(mk_rows.shape), _resident(mv_rows.shape)],
        out_specs=[row(RET_V_W), pl.BlockSpec(s_shape, lambda i: (i // tps, 0, 0, 0)),
                   tposed(DIFF_W), row(DIFF_W), tposed(wv), row(D_MODEL), row(D_MODEL),
                   hbm, hbm],
        out_shape=[jax.ShapeDtypeStruct((rows, RET_V_W), BF16),
                   jax.ShapeDtypeStruct((n_seq,) + s_shape[1:], F32),
                   jax.ShapeDtypeStruct((rows // tm, DIFF_W, tm), BF16),
                   jax.ShapeDtypeStruct((rows, DIFF_W), BF16),
                   jax.ShapeDtypeStruct((rows // tm, wv, tm), BF16),
                   jax.ShapeDtypeStruct((rows, D_MODEL), F32),
                   jax.ShapeDtypeStruct((rows, D_MODEL), F32),
                   rows_shape, rows_shape],
        scratch_shapes=[pltpu.VMEM(s_shape[1:], F32),
                        pltpu.VMEM((2, 2, tm * DIFF_HEADS, DIFF_DV), F32),
                        pltpu.SemaphoreType.DMA((2, 2)), pltpu.SemaphoreType.DMA((2,))],
        compiler_params=_cparams(1), name="prompt_proj",
    )(x2d, g, w_bf, cs, sn, s0, dmat, qdec, kdec, mk_rows, mv_rows)


def _retention_kernel(rq_ref, rk_ref, rv_ref, rg_ref, s0_ref, dm_ref, qd_ref, kd_ref,
                      y_ref, sfin_ref, s_scr, *, sdec, with_output):
    t = pl.program_id(1)

    @pl.when(t == 0)
    def _():
        s_scr[...] = s0_ref[0]

    for hh in range(RET_HEADS):
        qs = slice(hh * RET_DK, (hh + 1) * RET_DK)
        vs = slice(hh * RET_DV, (hh + 1) * RET_DV)
        y, state = _retention_head(rq_ref[:, qs].astype(F32), rk_ref[:, qs].astype(F32),
                                   rv_ref[:, vs], rg_ref[:, vs], s_scr[hh], dm_ref[hh],
                                   qd_ref[hh], kd_ref[hh], sdec[hh], with_output)
        if with_output:
            y_ref[:, vs] = y.astype(BF16)
        else:
            y_ref[:, vs] = jnp.zeros((y_ref.shape[0], RET_DV), BF16)
        s_scr[hh] = state

    @pl.when(t == pl.num_programs(1) - 1)
    def _():
        sfin_ref[0] = s_scr[...]


def _retention(rq, rk, rv, rg, s0, n_seq, t, with_output=True):
    rows = rq.shape[0]
    nt = rows // n_seq // t
    dmat, qdec, kdec, sdec = _retention_tables(t)
    row = lambda w: pl.BlockSpec((t, w), lambda b, i: (b * nt + i, 0))
    s_shape = (1, RET_HEADS, RET_DK, RET_DV)
    if s0.shape[0] == 1:
        s0_spec = pl.BlockSpec(s_shape, lambda b, i: (0, 0, 0, 0))
    else:
        s0_spec = pl.BlockSpec(s_shape, lambda b, i: (b, 0, 0, 0))
    y, sfin = pl.pallas_call(
        functools.partial(_retention_kernel, sdec=sdec, with_output=with_output),
        grid=(n_seq, nt),
        in_specs=[row(RET_QK_W), row(RET_QK_W), row(RET_V_W), row(RET_V_W), s0_spec,
                  _resident(dmat.shape), _resident(qdec.shape), _resident(kdec.shape)],
        out_specs=[row(RET_V_W), pl.BlockSpec(s_shape, lambda b, i: (b, 0, 0, 0))],
        out_shape=[jax.ShapeDtypeStruct((rows, RET_V_W), BF16),
                   jax.ShapeDtypeStruct((n_seq,) + s_shape[1:], F32)],
        scratch_shapes=[pltpu.VMEM(s_shape[1:], F32)],
        compiler_params=_cparams(2), name="retention",
    )(rq, rk, rv, rg, s0, dmat, qdec, kdec)
    return y, sfin


def _lambda(lam_ref):
    lp = lam_ref[...]
    a = jnp.exp(jnp.sum(lp[0:1] * lp[1:2], axis=-1, keepdims=True))
    b = jnp.exp(jnp.sum(lp[2:3] * lp[3:4], axis=-1, keepdims=True))
    return a - b + LAM_INIT


def _stack_maps(q):
    qf = q.astype(F32)
    lane = lax.broadcasted_iota(jnp.int32, q.shape, 1)
    return jnp.concatenate([jnp.where(lane < DIFF_DH, qf, 0.0),
                            jnp.where(lane < DIFF_DH, 0.0, qf)], axis=0).astype(BF16)


def _mix_and_norm(o_all, lam, subg, n):
    o = o_all[:n] - lam * o_all[n:]
    y = o * lax.rsqrt(jnp.mean(o * o, axis=-1, keepdims=True) + DIFF_EPS) * subg
    return (y * (1.0 - LAM_INIT)).astype(BF16)


def _diff_attn_kernel(qt_ref, k_ref, vt_ref, mk_ref, mvt_ref, tdiag_ref, tsub_ref, tmeta_ref,
                      lam_ref, subg_ref, y_ref, qq_scr, m_scr, acc_scr, s_scr):
    i = pl.program_id(2)
    heads = range(HEADS_PER_STEP)
    hs = lambda h: slice(h * DIFF_DV, (h + 1) * DIFF_DV)
    vs = lambda h: slice(h * DV_AUG, (h + 1) * DV_AUG)

    for h in heads:
        qt = qt_ref[0, hs(h), :].astype(F32)
        row = lax.broadcasted_iota(jnp.int32, qt.shape, 0)
        qq_scr[h] = jnp.concatenate([jnp.where(row < DIFF_DH, qt, 0.0),
                                     jnp.where(row < DIFF_DH, 0.0, qt)], axis=1).astype(BF16)

    def scores(h, k_blk, tile):
        s = jnp.dot(k_blk, qq_scr[h], preferred_element_type=F32)
        return s if tile is None else s + tile

    for h in heads:
        m_scr[h] = jnp.full((1, 2 * TQ), NEG_INF, F32)
        acc_scr[h] = jnp.zeros((DV_AUG, 2 * TQ), F32)

    def meta_scores():
        return [scores(h, mk_ref[:, hs(h)], tmeta_ref[h, jnp.minimum(i, 1)]) for h in heads]

    def consume_blocks(blocks, s_meta=None):
        parts = [s for _, s in blocks] + ([s_meta] if s_meta is not None else [])
        s = [jnp.concatenate([part[h] for part in parts], axis=0) if len(parts) > 1
             else parts[0][h] for h in heads]
        for h in heads:
            m_prev = m_scr[h]
            m_new = jnp.maximum(m_prev, jnp.max(s[h], axis=0, keepdims=True))
            p = jnp.exp2(s[h] - m_new).astype(BF16)
            alpha = jnp.exp2(m_prev - m_new)
            m_scr[h] = m_new
            terms = [jnp.dot(vt_ref[j, vs(h), :], p[n * TK:(n + 1) * TK],
                             preferred_element_type=F32) for n, (j, _) in enumerate(blocks)]
            if s_meta is not None:
                terms.append(jnp.dot(mvt_ref[vs(h), :], p[len(blocks) * TK:],
                                     preferred_element_type=F32))
            acc_scr[h] = alpha * acc_scr[h] + functools.reduce(lambda a, b: a + b, terms)

    def consume(j, s):
        consume_blocks([(j, s)])

    def qk_to_scratch(j, buf, tiles=None):
        off = pl.multiple_of(j * TK, TK)
        for h in heads:
            s_scr[buf, h] = scores(h, k_ref[pl.ds(off, TK), hs(h)],
                                   None if tiles is None else tiles[h])

    def from_scratch(buf):
        return [s_scr[buf, h] for h in heads]

    def pair_steps(j):
        qk_to_scratch(j + 1, 1)
        consume(j, from_scratch(0))
        qk_to_scratch(j + 2, 0)
        consume(j + 1, from_scratch(1))

    def far_pairs(n, first_pair):
        def body(t, carry):
            for k in range(n):
                pair_steps(2 * (first_pair + n * t + k))
            return carry
        return body

    j_sub = jnp.maximum(i - 1, 0)
    n_pairs = j_sub // 2
    qk_to_scratch(0, 0)
    done = 0
    for n in FAR_LOOP_PAIRS:
        trips = (n_pairs - done) // n
        lax.fori_loop(0, trips, far_pairs(n, done), 0)
        done = done + trips * n

    def finish():
        lam = _lambda(lam_ref)
        for h in heads:
            o_all = acc_scr[h, :DIFF_DV, :] / acc_scr[h, DIFF_DV:DIFF_DV + 1, :]
            o = o_all[:, :TQ] - lam * o_all[:, TQ:]
            y = o * lax.rsqrt(jnp.mean(o * o, axis=0, keepdims=True) + DIFF_EPS) * subg_ref[...]
            y_ref[:, hs(h)] = (y * (1.0 - LAM_INIT)).T.astype(BF16)

    @pl.when(j_sub % 2 == 1)
    def _():
        s_meta = meta_scores()
        qk_to_scratch(j_sub, 1, tsub_ref)
        consume(j_sub - 1, from_scratch(0))
        qk_to_scratch(i, 0, tdiag_ref)
        consume_blocks([(j_sub, from_scratch(1)), (i, from_scratch(0))], s_meta)
        finish()

    @pl.when(j_sub % 2 == 0)
    def _():
        s_meta = meta_scores()
        qk_to_scratch(i, 1, tdiag_ref)
        pen = jnp.where(i == 0, 3.0 * NEG_INF, 0.0)
        consume_blocks([(j_sub, [s_scr[0, h] + (tsub_ref[h] + pen) for h in heads]),
                        (i, from_scratch(1))], s_meta)
        finish()


def _diff_attn(dqt, dk, dvt, mk, mvt, tdiag, tsub, tmeta, lam_p, subg_b, n_seq):
    rows = dk.shape[0]
    seq = rows // n_seq
    nq = seq // TQ
    hb = HEADS_PER_STEP
    wid = hb * DIFF_DV
    wid_v = hb * DV_AUG
    tile3 = lambda b, g, i: (g, 0, 0)
    slow = lambda shape, imap: pl.BlockSpec(shape, imap, pipeline_mode=pl.Buffered(1))
    return pl.pallas_call(
        _diff_attn_kernel, grid=(n_seq, DIFF_HEADS // hb, nq),
        in_specs=[pl.BlockSpec((1, wid, TQ), lambda b, g, i: (b * nq + i, g, 0)),
                  pl.BlockSpec((seq, wid), lambda b, g, i: (b, g)),
                  pl.BlockSpec((nq, wid_v, TK), lambda b, g, i: (b, g, 0)),
                  slow((N_META, wid), lambda b, g, i: (0, g)),
                  slow((wid_v, N_META), lambda b, g, i: (g, 0)),
                  slow((hb, TK, 2 * TQ), tile3),
                  slow((hb, TK, 2 * TQ), tile3),
                  slow((hb, 2, N_META, 2 * TQ), lambda b, g, i: (g, 0, 0, 0)),
                  slow((4, DIFF_DH), lambda b, g, i: (0, 0)),
                  slow((DIFF_DV, TQ), lambda b, g, i: (0, 0))],
        out_specs=pl.BlockSpec((TQ, wid), lambda b, g, i: (b * nq + i, g)),
        out_shape=jax.ShapeDtypeStruct((rows, DIFF_W), BF16),
        scratch_shapes=[pltpu.VMEM((hb, DIFF_DV, 2 * TQ), BF16),
                        pltpu.VMEM((hb, 1, 2 * TQ), F32),
                        pltpu.VMEM((hb, DV_AUG, 2 * TQ), F32),
                        pltpu.VMEM((2, hb, TK, 2 * TQ), F32)],
        compiler_params=_cparams(3), name="diff_attn",
    )(dqt, dk, dvt, mk, mvt, tdiag, tsub, tmeta, lam_p, subg_b)


def _sample_attn_kernel(q_ref, ck_ref, cv_ref, nk_ref, nv_ref, tile_ref, lam_ref, subg_ref,
                        y_ref, k_scr, v_scr, *, past, n_new):
    pad = k_scr.shape[1] - past - n_new
    lam = _lambda(lam_ref)
    for h in range(DIFF_HEADS):
        hs = slice(h * DIFF_DV, (h + 1) * DIFF_DV)
        k_scr[h, 0:past] = ck_ref[0, pl.ds(h, past, stride=DIFF_HEADS), :].astype(BF16)
        v_scr[h, 0:past] = cv_ref[0, pl.ds(h, past, stride=DIFF_HEADS), :].astype(BF16)
        k_scr[h, past:past + n_new] = nk_ref[:, hs]
        v_scr[h, past:past + n_new] = nv_ref[:, hs]
        k_scr[h, past + n_new:] = jnp.zeros((pad, LANES), BF16)
        v_scr[h, past + n_new:] = jnp.zeros((pad, LANES), BF16)
        qq = _stack_maps(q_ref[:, hs])
        s = lax.dot_general(qq, k_scr[h], NT_DIMS, preferred_element_type=F32)
        s = (s.reshape(2, n_new, s.shape[-1]) + tile_ref[h][None]).reshape(2 * n_new, s.shape[-1])
        m = jnp.max(s, axis=1, keepdims=True)
        p = jnp.exp2(s - m)
        l = jnp.sum(p, axis=1, keepdims=True)
        o_all = jnp.dot(p.astype(BF16), v_scr[h], preferred_element_type=F32) / l
        y_ref[:, hs] = _mix_and_norm(o_all, lam, subg_ref[...], n_new)


def _sample_attn(dq, ck, cv, nk, nv, tile, lam_p, subg, n_seq, past, n_new):
    kpad = tile.shape[-1]
    row = pl.BlockSpec((n_new, DIFF_W), lambda b: (b, 0))
    cache = pl.BlockSpec((1, past * DIFF_HEADS, DIFF_DV), lambda b: (b, 0, 0))
    return pl.pallas_call(
        functools.partial(_sample_attn_kernel, past=past, n_new=n_new),
        grid=(n_seq,),
        in_specs=[row, cache, cache, row, row, _resident(tile.shape),
                  _resident((4, DIFF_DH)), _resident((1, DIFF_DV))],
        out_specs=row,
        out_shape=jax.ShapeDtypeStruct((n_seq * n_new, DIFF_W), BF16),
        scratch_shapes=[pltpu.VMEM((DIFF_HEADS, kpad, LANES), BF16),
                        pltpu.VMEM((DIFF_HEADS, kpad, LANES), BF16)],
        compiler_params=_cparams(1), name="sample_attn",
    )(dq, ck, cv, nk, nv, tile, lam_p, subg)


def _tail_kernel(h_ref, yr_ref, yd_ref, gr_ref, gd_ref, wrb_ref, wdb_ref, wo_ref, n2_ref,
                 wup_ref, wdn_ref, nf_ref, out_ref):
    a = jnp.dot(yr_ref[...], wrb_ref[...], preferred_element_type=F32)
    b = jnp.dot(yd_ref[...], wdb_ref[...], preferred_element_type=F32)
    merged = jax.nn.sigmoid(gr_ref[...]) * a + jax.nn.sigmoid(gd_ref[...]) * b
    h = h_ref[...] + jnp.dot(merged.astype(BF16), wo_ref[...], preferred_element_type=F32)
    xn = h * lax.rsqrt(jnp.mean(h * h, axis=-1, keepdims=True) + NORM_EPS) * n2_ref[...]
    xn = xn.astype(BF16)
    acc = jnp.zeros_like(h)
    for lo in range(0, D_FF, FF_CHUNK):
        n = min(FF_CHUNK, D_FF - lo)
        gate = jnp.dot(xn, wup_ref[:, lo:lo + n], preferred_element_type=F32)
        up = jnp.dot(xn, wup_ref[:, D_FF + lo:D_FF + lo + n], preferred_element_type=F32)
        act = (gate * jax.nn.sigmoid(gate) * up).astype(BF16)
        acc = acc + jnp.dot(act, wdn_ref[lo:lo + n, :], preferred_element_type=F32)
    h = h + acc
    out_ref[...] = h * lax.rsqrt(jnp.mean(h * h, axis=-1, keepdims=True) + NORM_EPS) * nf_ref[...]


def _tail(h2d, y_ret, y_diff, gr, gd, wrb, wdb, wo, n2, wup, wdn, nf, tm):
    rows = h2d.shape[0]
    row = pl.BlockSpec((tm, D_MODEL), lambda i: (i, 0))
    return pl.pallas_call(
        _tail_kernel, grid=(rows // tm,),
        in_specs=[row, row, row, row, row,
                  _resident(wrb.shape), _resident(wdb.shape), _resident(wo.shape),
                  _resident(n2.shape), _resident(wup.shape), _resident(wdn.shape),
                  _resident(nf.shape)],
        out_specs=row,
        out_shape=jax.ShapeDtypeStruct((rows, D_MODEL), F32),
        compiler_params=_cparams(1), name="tail",
    )(h2d, y_ret, y_diff, gr, gd, wrb, wdb, wo, n2, wup, wdn, nf)


def kernel(x_prompt, x_sample, cache_k, cache_v, state_ret, meta_tokens, rel_bias, norm1_g, w_in,
           lambda_q1, lambda_k1, lambda_q2, lambda_k2, diff_subln_g, w_ret_branch, w_diff_branch,
           w_o, norm2_g, w_ffn_up, w_ffn_down, normf_g):
    assert w_in.shape[0] == 1, "single-layer step only"
    bsz, seq, _ = x_prompt.shape
    dbsz, dseq, _ = x_sample.shape
    past = cache_k.shape[2]
    assert TQ == TK == TM_PROJ and TQ % CHUNK == 0 and DIFF_HEADS % HEADS_PER_STEP == 0
    assert seq % TQ == 0
    assert dseq <= CHUNK and past % CHUNK == 0 and meta_tokens.shape[0] == N_META

    w_in_bf = w_in[0].astype(BF16)
    g1 = norm1_g[0][None, :]
    lam_p = jnp.stack([lambda_q1[0], lambda_k1[0], lambda_q2[0], lambda_k2[0]])
    subg = diff_subln_g[0][None, :]
    wts = (w_ret_branch[0].astype(BF16), w_diff_branch[0].astype(BF16), w_o[0].astype(BF16),
           norm2_g[0][None, :], w_ffn_up[0].astype(BF16), w_ffn_down[0].astype(BF16),
           normf_g[None, :])

    kpad = ((past + dseq + LANES - 1) // LANES) * LANES
    both_maps = lambda ix: np.concatenate([ix.T, ix.T], axis=1)
    idx_diag = both_maps(_bias_idx(np.arange(TQ), np.arange(TK), TK))
    idx_sub = both_maps(_bias_idx(np.arange(TQ) + TK, np.arange(TK), TK))
    idx_meta = np.concatenate(
        [both_maps(_bias_idx(np.arange(TQ) + b * TQ, np.arange(N_META) - N_META, N_META))
         for b in range(2)], axis=0)
    idx_samp = _bias_idx(past + np.arange(dseq), np.arange(kpad), past + dseq)
    t_diag, t_sub, t_meta, t_samp = _bias_tiles(rel_bias, [idx_diag, idx_sub, idx_meta, idx_samp],
                                                [True, True, True, False])
    t_meta = t_meta.reshape(DIFF_HEADS, 2, N_META, 2 * TQ)

    cs_m, sn_m = _rotary_tables(np.arange(-N_META, 0))
    m_out = _in_proj(meta_tokens, g1, w_in_bf, cs_m, sn_m, N_META)
    zero_state = jnp.zeros((1, RET_HEADS, RET_DK, RET_DV), F32)
    _, s_meta = _retention(m_out[0], m_out[1], m_out[2], m_out[3], zero_state, 1, N_META,
                           with_output=False)
    mk = m_out[5]
    mv_aug = jnp.concatenate([m_out[7].reshape(N_META, DIFF_HEADS, DIFF_DV),
                              jnp.ones((N_META, DIFF_HEADS, ONES_ROWS), BF16)], axis=2)
    mvt = mv_aug.reshape(N_META, DIFF_HEADS * DV_AUG).T
    subg_b = jnp.broadcast_to(diff_subln_g[0][:, None], (DIFF_DV, TQ))

    cs_p, sn_p = _rotary_tables(np.arange(seq))
    x2d = x_prompt.reshape(bsz * seq, D_MODEL)
    meta_rows = lambda u: u.reshape(N_META * DIFF_HEADS, DIFF_DV)
    (y_ret, s_fin, dqt, dkb, dvt, gr, gd, k_rows, v_rows) = _prompt_proj(
        x2d, g1, w_in_bf, cs_p, sn_p, s_meta, meta_rows(m_out[6]), meta_rows(m_out[8]), bsz,
        TM_PROJ)
    y_diff = _diff_attn(dqt, dkb, dvt, mk, mvt, t_diag, t_sub, t_meta, lam_p, subg_b, bsz)
    y_prompt = _tail(x2d, y_ret, y_diff, gr, gd, *wts, TM_TAIL).reshape(bsz, seq, D_MODEL)

    cs_s, sn_s = _rotary_tables(np.tile(past + np.arange(dseq), dbsz))
    xs2d = x_sample.reshape(dbsz * dseq, D_MODEL)
    (rq, rk, rv, rg, dq, dkb, dkf_s, dvb, dvf_s, gr, gd) = _in_proj(xs2d, g1, w_in_bf, cs_s, sn_s,
                                                                     dbsz * dseq)
    y_ret, s_samp = _retention(rq, rk, rv, rg, state_ret[0], dbsz, dseq)
    ck = cache_k[0].reshape(dbsz, past * DIFF_HEADS, DIFF_DV)
    cv = cache_v[0].reshape(dbsz, past * DIFF_HEADS, DIFF_DV)
    y_diff = _sample_attn(dq, ck, cv, dkb, dvb, t_samp, lam_p, subg, dbsz, past, dseq)
    y_sample = _tail(xs2d, y_ret, y_diff, gr, gd, *wts, dbsz * dseq).reshape(dbsz, dseq, D_MODEL)

    heads = (DIFF_HEADS, DIFF_DV)
    return (y_prompt, y_sample,
            k_rows.reshape((1, bsz, N_META + seq) + heads),
            v_rows.reshape((1, bsz, N_META + seq) + heads),
            s_fin[None],
            dkf_s.reshape((1, dbsz, dseq) + heads),
            dvf_s.reshape((1, dbsz, dseq) + heads),
            s_samp[None])
```

```python
import functools
import math

import numpy as np
import jax
import jax.numpy as jnp
from jax import lax
from jax.experimental import pallas as pl
from jax.experimental.pallas import tpu as pltpu

F32 = jnp.float32
BF16 = jnp.bfloat16

D_MODEL = 1024
CHUNK = 64
N_META = 16
RET_HEADS = 4
RET_DK = D_MODEL // 8
RET_DV = 2 * RET_DK
RET_QK_W = RET_HEADS * RET_DK
RET_V_W = RET_HEADS * RET_DV
ROPE_BASE = 10000.0
RET_EPS = 1e-6
DIFF_HEADS = 8
DIFF_DH = D_MODEL // 16
DIFF_DV = 2 * DIFF_DH
DIFF_W = DIFF_HEADS * DIFF_DV
DIFF_EPS = 1e-5
N_BUCKETS = 32
MAX_DISTANCE = 128
D_FF = ((8 * D_MODEL + 3 * 256 - 1) // (3 * 256)) * 256
NORM_EPS = 1e-6
NEG_INF = -1e30
LAM_INIT = 0.8 - 0.6 * math.exp(-0.3 * 0)
LOG2E = math.log2(math.e)
Q_SCALE = DIFF_DH ** -0.5 * LOG2E
ONES_ROWS = 16
DV_AUG = DIFF_DV + ONES_ROWS

OFF_RQ = 0
OFF_RK = OFF_RQ + RET_QK_W
OFF_RV = OFF_RK + RET_QK_W
OFF_RG = OFF_RV + RET_V_W
OFF_DQ = OFF_RG + RET_V_W
OFF_DK = OFF_DQ + DIFF_W
OFF_DV = OFF_DK + DIFF_W
OFF_GR = OFF_DV + DIFF_W
OFF_GD = OFF_GR + D_MODEL
W_IN = OFF_GD + D_MODEL

LANES = 128
VMEM_LIMIT_BYTES = 56 * 1024 * 1024
TM_PROJ = 256
TM_TAIL = 512
TQ = 256
TK = 256
HEADS_PER_STEP = 4
FAR_LOOP_PAIRS = (8, 4, 2, 1)
FF_CHUNK = 1024

NT_DIMS = (((1,), (1,)), ((), ()))
TN_DIMS = (((0,), (0,)), ((), ()))


def _cparams(n_axes):
    return pltpu.CompilerParams(dimension_semantics=("arbitrary",) * n_axes,
                                vmem_limit_bytes=VMEM_LIMIT_BYTES)


def _resident(shape):
    nd = len(shape)
    return pl.BlockSpec(shape, lambda *_: (0,) * nd, pipeline_mode=pl.Buffered(1))


def _t5_bucket_np(rel):
    nb = N_BUCKETS // 2
    max_exact = nb // 2
    ret = np.where(rel > 0, nb, 0)
    n = np.abs(rel)
    nf = np.maximum(n, max_exact).astype(np.float64)
    large = max_exact + (np.log(nf / max_exact) / math.log(MAX_DISTANCE / max_exact)
                         * (nb - max_exact)).astype(np.int32)
    large = np.minimum(large, nb - 1)
    return (ret + np.where(n < max_exact, n, large)).astype(np.int32)


def _bias_idx(qpos, kpos, n_real):
    qpos = np.asarray(qpos)[:, None]
    kpos = np.asarray(kpos)[None, :]
    vis = (np.floor_divide(kpos, CHUNK) <= np.floor_divide(qpos, CHUNK))
    vis = vis & (np.arange(kpos.shape[1])[None, :] < n_real)
    return np.where(vis, _t5_bucket_np(kpos - qpos), -1).astype(np.int32)


def _rotary_tables(pos):
    half = RET_DK // 2
    freq = 1.0 / (ROPE_BASE ** jnp.linspace(0.0, 1.0, half, dtype=F32))
    ang = jnp.asarray(pos, F32)[:, None] * freq[None, :]
    cos, sin = jnp.cos(ang), jnp.sin(ang)
    return jnp.concatenate([cos, cos], axis=1), jnp.concatenate([-sin, sin], axis=1)


def _retention_tables(t):
    gam = 1.0 - 2.0 ** (-5.0 - np.arange(RET_HEADS, dtype=np.float64))
    lg = np.log(gam)[:, None, None]
    n = np.arange(t)[:, None]
    m = np.arange(t)[None, :]
    cn, cm = n // CHUNK, m // CHUNK
    expo = np.where(cm == cn, np.abs(n - m), n - m).astype(np.float64)[None]
    dmat = np.where((cm <= cn)[None], np.exp(lg * expo), 0.0)
    qdec = np.exp(lg[:, :, 0] * (np.arange(t)[None, :] + 1.0))[:, :, None]
    kdec = np.exp(lg[:, :, 0] * (t - 1.0 - np.arange(t)[None, :]))[:, :, None]
    sdec = np.exp(lg[:, 0, 0] * t)
    return (jnp.asarray(dmat, F32), jnp.asarray(qdec, F32), jnp.asarray(kdec, F32),
            tuple(float(s) for s in sdec))


def _bias_kernel(rb_ref, *refs, n_tiles, shifted):
    idx_refs, out_refs = refs[:n_tiles], refs[n_tiles:]
    h = pl.program_id(0)
    far = rb_ref[N_BUCKETS // 2 - 1, h]
    for idx_ref, out_ref, sh in zip(idx_refs, out_refs, shifted):
        idx = idx_ref[...]
        acc = jnp.full(idx.shape, NEG_INF, F32)
        for b in range(N_BUCKETS):
            val = (rb_ref[b, h] - far if sh else rb_ref[b, h]) * LOG2E
            acc = jnp.where(idx == b, val, acc)
        out_ref[0] = acc


def _bias_tiles(rel_bias, idx_list, shifted):
    n = len(idx_list)
    in_specs = [pl.BlockSpec(memory_space=pltpu.SMEM)]
    in_specs += [pl.BlockSpec(ix.shape, lambda h: (0, 0)) for ix in idx_list]
    out_specs = [pl.BlockSpec((1,) + ix.shape, lambda h: (h, 0, 0)) for ix in idx_list]
    out_shape = [jax.ShapeDtypeStruct((DIFF_HEADS,) + ix.shape, F32) for ix in idx_list]
    return pl.pallas_call(
        functools.partial(_bias_kernel, n_tiles=n, shifted=tuple(shifted)),
        grid=(DIFF_HEADS,), in_specs=in_specs, out_specs=out_specs, out_shape=out_shape,
        compiler_params=_cparams(1), name="bias_tiles",
    )(rel_bias, *[jnp.asarray(ix) for ix in idx_list])


def _norm_proj(x_ref, g_ref, w_ref):
    x = x_ref[...]
    xn = x * lax.rsqrt(jnp.mean(x * x, axis=-1, keepdims=True) + NORM_EPS) * g_ref[...]
    xn = xn.astype(BF16)
    return lambda lo, n: jnp.dot(xn, w_ref[:, lo:lo + n], preferred_element_type=F32)


def _rotary(u, cs, sn):
    return u * cs + pltpu.roll(u, RET_DK // 2, 1) * sn


def _retention_head(qf, kf, v, gate, state, dm, qd, kd, sdec, with_output=True):
    y = None
    if with_output:
        s = lax.dot_general(qf.astype(BF16), kf.astype(BF16), NT_DIMS,
                            preferred_element_type=F32) * dm
        o = jnp.dot(s.astype(BF16), v, preferred_element_type=F32)
        o = o + jnp.dot((qf * qd).astype(BF16), state.astype(BF16), preferred_element_type=F32)
        y = o * lax.rsqrt(jnp.mean(o * o, axis=-1, keepdims=True) + RET_EPS)
        y = y * (gate * jax.nn.sigmoid(gate))
    kv = lax.dot_general((kf * kd).astype(BF16), v, TN_DIMS, preferred_element_type=F32)
    return y, sdec * state + kv


def _inproj_kernel(x_ref, g_ref, w_ref, cs_ref, sn_ref,
                   rq_ref, rk_ref, rv_ref, rg_ref, dq_ref, dkb_ref, dkf_ref,
                   dvb_ref, dvf_ref, gr_ref, gd_ref):
    proj = _norm_proj(x_ref, g_ref, w_ref)
    cs = cs_ref[...]
    sn = sn_ref[...]
    uq = proj(OFF_RQ, RET_QK_W)
    uk = proj(OFF_RK, RET_QK_W)
    for hh in range(RET_HEADS):
        sl = slice(hh * RET_DK, (hh + 1) * RET_DK)
        rq_ref[:, sl] = (_rotary(uq[:, sl], cs, sn) * (RET_DK ** -0.5)).astype(BF16)
        rk_ref[:, sl] = _rotary(uk[:, sl], cs, sn).astype(BF16)
    rv_ref[...] = proj(OFF_RV, RET_V_W).astype(BF16)
    rg_ref[...] = proj(OFF_RG, RET_V_W)
    dq_ref[...] = (proj(OFF_DQ, DIFF_W) * Q_SCALE).astype(BF16)
    dk = proj(OFF_DK, DIFF_W)
    dkf_ref[...] = dk
    dkb_ref[...] = dk.astype(BF16)
    dv = proj(OFF_DV, DIFF_W)
    dvf_ref[...] = dv
    dvb_ref[...] = dv.astype(BF16)
    gr_ref[...] = proj(OFF_GR, D_MODEL)
    gd_ref[...] = proj(OFF_GD, D_MODEL)


def _in_proj(x2d, g, w_bf, cs, sn, tm):
    rows = x2d.shape[0]
    n_pos = cs.shape[0] // tm
    row = lambda w: pl.BlockSpec((tm, w), lambda i: (i, 0))
    pos = pl.BlockSpec((tm, LANES), lambda i: (i % n_pos, 0))
    outs = [(RET_QK_W, BF16), (RET_QK_W, BF16), (RET_V_W, BF16), (RET_V_W, F32),
            (DIFF_W, BF16), (DIFF_W, BF16), (DIFF_W, F32), (DIFF_W, BF16), (DIFF_W, F32),
            (D_MODEL, F32), (D_MODEL, F32)]
    return pl.pallas_call(
        _inproj_kernel, grid=(rows // tm,),
        in_specs=[row(D_MODEL), _resident((1, D_MODEL)), _resident((D_MODEL, W_IN)), pos, pos],
        out_specs=[row(w) for w, _ in outs],
        out_shape=[jax.ShapeDtypeStruct((rows, w), dt) for w, dt in outs],
        compiler_params=_cparams(1), name="in_proj",
    )(x2d, g, w_bf, cs, sn)


def _prompt_proj_kernel(x_ref, g_ref, w_ref, cs_ref, sn_ref, s0_ref, dm_ref, qd_ref, kd_ref,
                        mk_ref, mv_ref,
                        yret_ref, sfin_ref, dqt_ref, dkb_ref, dvt_ref, gr_ref, gd_ref,
                        krows_ref, vrows_ref,
                        s_scr, stage_scr, row_sem, meta_sem, *, tiles_per_seq, n_steps, sdec):
    i = pl.program_id(0)
    tm = x_ref.shape[0]
    proj = _norm_proj(x_ref, g_ref, w_ref)
    cs = cs_ref[...]
    sn = sn_ref[...]
    slot = i % 2
    stream = i // tiles_per_seq
    rows_out = (krows_ref, vrows_ref)

    def rows_copy(which, step):
        first = (N_META + (step % tiles_per_seq) * tm) * DIFF_HEADS
        return pltpu.make_async_copy(
            stage_scr.at[step % 2, which],
            rows_out[which].at[step // tiles_per_seq, pl.ds(first, tm * DIFF_HEADS), :],
            row_sem.at[step % 2, which])

    def meta_copy(which):
        return pltpu.make_async_copy(
            (mk_ref, mv_ref)[which],
            rows_out[which].at[stream, pl.ds(0, N_META * DIFF_HEADS), :], meta_sem.at[which])

    @pl.when(i % tiles_per_seq == 0)
    def _():
        s_scr[...] = s0_ref[0]
        for which in range(2):
            meta_copy(which).start()

    @pl.when(i >= 2)
    def _():
        for which in range(2):
            rows_copy(which, i - 2).wait()

    uq = proj(OFF_RQ, RET_QK_W)
    uk = proj(OFF_RK, RET_QK_W)
    rv = proj(OFF_RV, RET_V_W).astype(BF16)
    rg = proj(OFF_RG, RET_V_W)
    for hh in range(RET_HEADS):
        qs = slice(hh * RET_DK, (hh + 1) * RET_DK)
        vs = slice(hh * RET_DV, (hh + 1) * RET_DV)
        y, state = _retention_head(_rotary(uq[:, qs], cs, sn) * (RET_DK ** -0.5),
                                   _rotary(uk[:, qs], cs, sn), rv[:, vs], rg[:, vs],
                                   s_scr[hh], dm_ref[hh], qd_ref[hh], kd_ref[hh], sdec[hh])
        yret_ref[:, vs] = y.astype(BF16)
        s_scr[hh] = state

    dqt_ref[0] = (proj(OFF_DQ, DIFF_W) * Q_SCALE).T.astype(BF16)
    dk = proj(OFF_DK, DIFF_W)
    dkb_ref[...] = dk.astype(BF16)
    dv = proj(OFF_DV, DIFF_W)
    dvt = dv.T.astype(BF16)
    ones = jnp.ones((ONES_ROWS, dvt.shape[1]), BF16)
    for hh in range(DIFF_HEADS):
        hs = slice(hh * DIFF_DV, (hh + 1) * DIFF_DV)
        dvt_ref[0, hh * DV_AUG:hh * DV_AUG + DIFF_DV, :] = dvt[hs]
        dvt_ref[0, hh * DV_AUG + DIFF_DV:(hh + 1) * DV_AUG, :] = ones
        head_rows = pl.ds(hh, tm, stride=DIFF_HEADS)
        stage_scr[slot, 0, head_rows, :] = dk[:, hs]
        stage_scr[slot, 1, head_rows, :] = dv[:, hs]
    for which in range(2):
        rows_copy(which, i).start()
    gr_ref[...] = proj(OFF_GR, D_MODEL)
    gd_ref[...] = proj(OFF_GD, D_MODEL)

    @pl.when(i % tiles_per_seq == 0)
    def _():
        for which in range(2):
            meta_copy(which).wait()

    @pl.when(i % tiles_per_seq == tiles_per_seq - 1)
    def _():
        sfin_ref[0] = s_scr[...]

    @pl.when(i == n_steps - 1)
    def _():
        for which in range(2):
            if n_steps >= 2:
                rows_copy(which, i - 1).wait()
            rows_copy(which, i).wait()


def _prompt_proj(x2d, g, w_bf, cs, sn, s0, mk_rows, mv_rows, n_seq, tm):
    rows = x2d.shape[0]
    seq = rows // n_seq
    tps = seq // tm
    dmat, qdec, kdec, sdec = _retention_tables(tm)
    row = lambda w: pl.BlockSpec((tm, w), lambda i: (i, 0))
    pos = pl.BlockSpec((tm, LANES), lambda i: (i % tps, 0))
    tposed = lambda w: pl.BlockSpec((1, w, tm), lambda i: (i, 0, 0))
    hbm = pl.BlockSpec(memory_space=pl.ANY)
    s_shape = (1, RET_HEADS, RET_DK, RET_DV)
    wv = DIFF_HEADS * DV_AUG
    rows_shape = jax.ShapeDtypeStruct((n_seq, (N_META + seq) * DIFF_HEADS, DIFF_DV), F32)
    return pl.pallas_call(
        functools.partial(_prompt_proj_kernel, tiles_per_seq=tps, n_steps=rows // tm, sdec=sdec),
        grid=(rows // tm,),
        in_specs=[row(D_MODEL), _resident((1, D_MODEL)), _resident((D_MODEL, W_IN)), pos, pos,
                  _resident(s_shape), _resident(dmat.shape), _resident(qdec.shape),
                  _resident(kdec.shape), _resident(mk_rows.shape), _resident(mv_rows.shape)],
        out_specs=[row(RET_V_W), pl.BlockSpec(s_shape, lambda i: (i // tps, 0, 0, 0)),
                   tposed(DIFF_W), row(DIFF_W), tposed(wv), row(D_MODEL), row(D_MODEL),
                   hbm, hbm],
        out_shape=[jax.ShapeDtypeStruct((rows, RET_V_W), BF16),
                   jax.ShapeDtypeStruct((n_seq,) + s_shape[1:], F32),
                   jax.ShapeDtypeStruct((rows // tm, DIFF_W, tm), BF16),
                   jax.ShapeDtypeStruct((rows, DIFF_W), BF16),
                   jax.ShapeDtypeStruct((rows // tm, wv, tm), BF16),
                   jax.ShapeDtypeStruct((rows, D_MODEL), F32),
                   jax.ShapeDtypeStruct((rows, D_MODEL), F32),
                   rows_shape, rows_shape],
        scratch_shapes=[pltpu.VMEM(s_shape[1:], F32),
                        pltpu.VMEM((2, 2, tm * DIFF_HEADS, DIFF_DV), F32),
                        pltpu.SemaphoreType.DMA((2, 2)), pltpu.SemaphoreType.DMA((2,))],
        compiler_params=_cparams(1), name="prompt_proj",
    )(x2d, g, w_bf, cs, sn, s0, dmat, qdec, kdec, mk_rows, mv_rows)


def _retention_kernel(rq_ref, rk_ref, rv_ref, rg_ref, s0_ref, dm_ref, qd_ref, kd_ref,
                      y_ref, sfin_ref, s_scr, *, sdec, with_output):
    t = pl.program_id(1)

    @pl.when(t == 0)
    def _():
        s_scr[...] = s0_ref[0]

    for hh in range(RET_HEADS):
        qs = slice(hh * RET_DK, (hh + 1) * RET_DK)
        vs = slice(hh * RET_DV, (hh + 1) * RET_DV)
        y, state = _retention_head(rq_ref[:, qs].astype(F32), rk_ref[:, qs].astype(F32),
                                   rv_ref[:, vs], rg_ref[:, vs], s_scr[hh], dm_ref[hh],
                                   qd_ref[hh], kd_ref[hh], sdec[hh], with_output)
        if with_output:
            y_ref[:, vs] = y.astype(BF16)
        else:
            y_ref[:, vs] = jnp.zeros((y_ref.shape[0], RET_DV), BF16)
        s_scr[hh] = state

    @pl.when(t == pl.num_programs(1) - 1)
    def _():
        sfin_ref[0] = s_scr[...]


def _retention(rq, rk, rv, rg, s0, n_seq, t, with_output=True):
    rows = rq.shape[0]
    nt = rows // n_seq // t
    dmat, qdec, kdec, sdec = _retention_tables(t)
    row = lambda w: pl.BlockSpec((t, w), lambda b, i: (b * nt + i, 0))
    s_shape = (1, RET_HEADS, RET_DK, RET_DV)
    if s0.shape[0] == 1:
        s0_spec = pl.BlockSpec(s_shape, lambda b, i: (0, 0, 0, 0))
    else:
        s0_spec = pl.BlockSpec(s_shape, lambda b, i: (b, 0, 0, 0))
    y, sfin = pl.pallas_call(
        functools.partial(_retention_kernel, sdec=sdec, with_output=with_output),
        grid=(n_seq, nt),
        in_specs=[row(RET_QK_W), row(RET_QK_W), row(RET_V_W), row(RET_V_W), s0_spec,
                  _resident(dmat.shape), _resident(qdec.shape), _resident(kdec.shape)],
        out_specs=[row(RET_V_W), pl.BlockSpec(s_shape, lambda b, i: (b, 0, 0, 0))],
        out_shape=[jax.ShapeDtypeStruct((rows, RET_V_W), BF16),
                   jax.ShapeDtypeStruct((n_seq,) + s_shape[1:], F32)],
        scratch_shapes=[pltpu.VMEM(s_shape[1:], F32)],
        compiler_params=_cparams(2), name="retention",
    )(rq, rk, rv, rg, s0, dmat, qdec, kdec)
    return y, sfin


def _lambda(lam_ref):
    lp = lam_ref[...]
    a = jnp.exp(jnp.sum(lp[0:1] * lp[1:2], axis=-1, keepdims=True))
    b = jnp.exp(jnp.sum(lp[2:3] * lp[3:4], axis=-1, keepdims=True))
    return a - b + LAM_INIT


def _stack_maps(q):
    qf = q.astype(F32)
    lane = lax.broadcasted_iota(jnp.int32, q.shape, 1)
    return jnp.concatenate([jnp.where(lane < DIFF_DH, qf, 0.0),
                            jnp.where(lane < DIFF_DH, 0.0, qf)], axis=0).astype(BF16)


def _mix_and_norm(o_all, lam, subg, n):
    o = o_all[:n] - lam * o_all[n:]
    y = o * lax.rsqrt(jnp.mean(o * o, axis=-1, keepdims=True) + DIFF_EPS) * subg
    return (y * (1.0 - LAM_INIT)).astype(BF16)


def _diff_attn_kernel(qt_ref, k_ref, vt_ref, mk_ref, mvt_ref, tdiag_ref, tsub_ref, tmeta_ref,
                      lam_ref, subg_ref, y_ref, qq_scr, m_scr, acc_scr, s_scr):
    i = pl.program_id(2)
    heads = range(HEADS_PER_STEP)
    hs = lambda h: slice(h * DIFF_DV, (h + 1) * DIFF_DV)
    vs = lambda h: slice(h * DV_AUG, (h + 1) * DV_AUG)

    for h in heads:
        qt = qt_ref[0, hs(h), :].astype(F32)
        row = lax.broadcasted_iota(jnp.int32, qt.shape, 0)
        qq_scr[h] = jnp.concatenate([jnp.where(row < DIFF_DH, qt, 0.0),
                                     jnp.where(row < DIFF_DH, 0.0, qt)], axis=1).astype(BF16)

    def scores(h, k_blk, tile):
        s = jnp.dot(k_blk, qq_scr[h], preferred_element_type=F32)
        return s if tile is None else s + tile

    for h in heads:
        m_scr[h] = jnp.full((1, 2 * TQ), NEG_INF, F32)
        acc_scr[h] = jnp.zeros((DV_AUG, 2 * TQ), F32)

    def meta_scores():
        return [scores(h, mk_ref[:, hs(h)], tmeta_ref[h, jnp.minimum(i, 1)]) for h in heads]

    def consume_blocks(blocks, s_meta=None):
        parts = [s for _, s in blocks] + ([s_meta] if s_meta is not None else [])
        s = [jnp.concatenate([part[h] for part in parts], axis=0) if len(parts) > 1
             else parts[0][h] for h in heads]
        for h in heads:
            m_prev = m_scr[h]
            m_new = jnp.maximum(m_prev, jnp.max(s[h], axis=0, keepdims=True))
            p = jnp.exp2(s[h] - m_new).astype(BF16)
            alpha = jnp.exp2(m_prev - m_new)
            m_scr[h] = m_new
            terms = [jnp.dot(vt_ref[j, vs(h), :], p[n * TK:(n + 1) * TK],
                             preferred_element_type=F32) for n, (j, _) in enumerate(blocks)]
            if s_meta is not None:
                terms.append(jnp.dot(mvt_ref[vs(h), :], p[len(blocks) * TK:],
                                     preferred_element_type=F32))
            acc_scr[h] = alpha * acc_scr[h] + functools.reduce(lambda a, b: a + b, terms)

    def consume(j, s):
        consume_blocks([(j, s)])

    def qk_to_scratch(j, buf, tiles=None):
        off = pl.multiple_of(j * TK, TK)
        for h in heads:
            s_scr[buf, h] = scores(h, k_ref[pl.ds(off, TK), hs(h)],
                                   None if tiles is None else tiles[h])

    def from_scratch(buf):
        return [s_scr[buf, h] for h in heads]

    def pair_steps(j):
        qk_to_scratch(j + 1, 1)
        consume(j, from_scratch(0))
        qk_to_scratch(j + 2, 0)
        consume(j + 1, from_scratch(1))

    def far_pairs(n, first_pair):
        def body(t, carry):
            for k in range(n):
                pair_steps(2 * (first_pair + n * t + k))
            return carry
        return body

    j_sub = jnp.maximum(i - 1, 0)
    n_pairs = j_sub // 2
    qk_to_scratch(0, 0)
    done = 0
    for n in FAR_LOOP_PAIRS:
        trips = (n_pairs - done) // n
        lax.fori_loop(0, trips, far_pairs(n, done), 0)
        done = done + trips * n

    def finish():
        lam = _lambda(lam_ref)
        for h in heads:
            o_all = acc_scr[h, :DIFF_DV, :] / acc_scr[h, DIFF_DV:DIFF_DV + 1, :]
            o = o_all[:, :TQ] - lam * o_all[:, TQ:]
            y = o * lax.rsqrt(jnp.mean(o * o, axis=0, keepdims=True) + DIFF_EPS) * subg_ref[...]
            y_ref[:, hs(h)] = (y * (1.0 - LAM_INIT)).T.astype(BF16)

    @pl.when(j_sub % 2 == 1)
    def _():
        s_meta = meta_scores()
        qk_to_scratch(j_sub, 1, tsub_ref)
        consume(j_sub - 1, from_scratch(0))
        qk_to_scratch(i, 0, tdiag_ref)
        consume_blocks([(j_sub, from_scratch(1)), (i, from_scratch(0))], s_meta)
        finish()

    @pl.when(j_sub % 2 == 0)
    def _():
        s_meta = meta_scores()
        qk_to_scratch(i, 1, tdiag_ref)
        pen = jnp.where(i == 0, 3.0 * NEG_INF, 0.0)
        consume_blocks([(j_sub, [s_scr[0, h] + (tsub_ref[h] + pen) for h in heads]),
                        (i, from_scratch(1))], s_meta)
        finish()


def _diff_attn(dqt, dk, dvt, mk, mvt, tdiag, tsub, tmeta, lam_p, subg_b, n_seq):
    rows = dk.shape[0]
    seq = rows // n_seq
    nq = seq // TQ
    hb = HEADS_PER_STEP
    wid = hb * DIFF_DV
    wid_v = hb * DV_AUG
    tile3 = lambda b, g, i: (g, 0, 0)
    slow = lambda shape, imap: pl.BlockSpec(shape, imap, pipeline_mode=pl.Buffered(1))
    return pl.pallas_call(
        _diff_attn_kernel, grid=(n_seq, DIFF_HEADS // hb, nq),
        in_specs=[pl.BlockSpec((1, wid, TQ), lambda b, g, i: (b * nq + i, g, 0)),
                  pl.BlockSpec((seq, wid), lambda b, g, i: (b, g)),
                  pl.BlockSpec((nq, wid_v, TK), lambda b, g, i: (b, g, 0)),
                  slow((N_META, wid), lambda b, g, i: (0, g)),
                  slow((wid_v, N_META), lambda b, g, i: (g, 0)),
                  slow((hb, TK, 2 * TQ), tile3),
                  slow((hb, TK, 2 * TQ), tile3),
                  slow((hb, 2, N_META, 2 * TQ), lambda b, g, i: (g, 0, 0, 0)),
                  slow((4, DIFF_DH), lambda b, g, i: (0, 0)),
                  slow((DIFF_DV, TQ), lambda b, g, i: (0, 0))],
        out_specs=pl.BlockSpec((TQ, wid), lambda b, g, i: (b * nq + i, g)),
        out_shape=jax.ShapeDtypeStruct((rows, DIFF_W), BF16),
        scratch_shapes=[pltpu.VMEM((hb, DIFF_DV, 2 * TQ), BF16),
                        pltpu.VMEM((hb, 1, 2 * TQ), F32),
                        pltpu.VMEM((hb, DV_AUG, 2 * TQ), F32),
                        pltpu.VMEM((2, hb, TK, 2 * TQ), F32)],
        compiler_params=_cparams(3), name="diff_attn",
    )(dqt, dk, dvt, mk, mvt, tdiag, tsub, tmeta, lam_p, subg_b)


def _sample_attn_kernel(q_ref, ck_ref, cv_ref, nk_ref, nv_ref, tile_ref, lam_ref, subg_ref,
                        y_ref, k_scr, v_scr, *, past, n_new):
    pad = k_scr.shape[1] - past - n_new
    lam = _lambda(lam_ref)
    for h in range(DIFF_HEADS):
        hs = slice(h * DIFF_DV, (h + 1) * DIFF_DV)
        k_scr[h, 0:past] = ck_ref[0, pl.ds(h, past, stride=DIFF_HEADS), :].astype(BF16)
        v_scr[h, 0:past] = cv_ref[0, pl.ds(h, past, stride=DIFF_HEADS), :].astype(BF16)
        k_scr[h, past:past + n_new] = nk_ref[:, hs]
        v_scr[h, past:past + n_new] = nv_ref[:, hs]
        k_scr[h, past + n_new:] = jnp.zeros((pad, LANES), BF16)
        v_scr[h, past + n_new:] = jnp.zeros((pad, LANES), BF16)
        qq = _stack_maps(q_ref[:, hs])
        s = lax.dot_general(qq, k_scr[h], NT_DIMS, preferred_element_type=F32)
        s = (s.reshape(2, n_new, s.shape[-1]) + tile_ref[h][None]).reshape(2 * n_new, s.shape[-1])
        m = jnp.max(s, axis=1, keepdims=True)
        p = jnp.exp2(s - m)
        l = jnp.sum(p, axis=1, keepdims=True)
        o_all = jnp.dot(p.astype(BF16), v_scr[h], preferred_element_type=F32) / l
        y_ref[:, hs] = _mix_and_norm(o_all, lam, subg_ref[...], n_new)


def _sample_attn(dq, ck, cv, nk, nv, tile, lam_p, subg, n_seq, past, n_new):
    kpad = tile.shape[-1]
    row = pl.BlockSpec((n_new, DIFF_W), lambda b: (b, 0))
    cache = pl.BlockSpec((1, past * DIFF_HEADS, DIFF_DV), lambda b: (b, 0, 0))
    return pl.pallas_call(
        functools.partial(_sample_attn_kernel, past=past, n_new=n_new),
        grid=(n_seq,),
        in_specs=[row, cache, cache, row, row, _resident(tile.shape),
                  _resident((4, DIFF_DH)), _resident((1, DIFF_DV))],
        out_specs=row,
        out_shape=jax.ShapeDtypeStruct((n_seq * n_new, DIFF_W), BF16),
        scratch_shapes=[pltpu.VMEM((DIFF_HEADS, kpad, LANES), BF16),
                        pltpu.VMEM((DIFF_HEADS, kpad, LANES), BF16)],
        compiler_params=_cparams(1), name="sample_attn",
    )(dq, ck, cv, nk, nv, tile, lam_p, subg)


def _tail_kernel(h_ref, yr_ref, yd_ref, gr_ref, gd_ref, wrb_ref, wdb_ref, wo_ref, n2_ref,
                 wup_ref, wdn_ref, nf_ref, out_ref):
    a = jnp.dot(yr_ref[...], wrb_ref[...], preferred_element_type=F32)
    b = jnp.dot(yd_ref[...], wdb_ref[...], preferred_element_type=F32)
    merged = jax.nn.sigmoid(gr_ref[...]) * a + jax.nn.sigmoid(gd_ref[...]) * b
    h = h_ref[...] + jnp.dot(merged.astype(BF16), wo_ref[...], preferred_element_type=F32)
    xn = h * lax.rsqrt(jnp.mean(h * h, axis=-1, keepdims=True) + NORM_EPS) * n2_ref[...]
    xn = xn.astype(BF16)
    acc = jnp.zeros_like(h)
    for lo in range(0, D_FF, FF_CHUNK):
        n = min(FF_CHUNK, D_FF - lo)
        gate = jnp.dot(xn, wup_ref[:, lo:lo + n], preferred_element_type=F32)
        up = jnp.dot(xn, wup_ref[:, D_FF + lo:D_FF + lo + n], preferred_element_type=F32)
        act = (gate * jax.nn.sigmoid(gate) * up).astype(BF16)
        acc = acc + jnp.dot(act, wdn_ref[lo:lo + n, :], preferred_element_type=F32)
    h = h + acc
    out_ref[...] = h * lax.rsqrt(jnp.mean(h * h, axis=-1, keepdims=True) + NORM_EPS) * nf_ref[...]


def _tail(h2d, y_ret, y_diff, gr, gd, wrb, wdb, wo, n2, wup, wdn, nf, tm):
    rows = h2d.shape[0]
    row = pl.BlockSpec((tm, D_MODEL), lambda i: (i, 0))
    return pl.pallas_call(
        _tail_kernel, grid=(rows // tm,),
        in_specs=[row, row, row, row, row,
                  _resident(wrb.shape), _resident(wdb.shape), _resident(wo.shape),
                  _resident(n2.shape), _resident(wup.shape), _resident(wdn.shape),
                  _resident(nf.shape)],
        out_specs=row,
        out_shape=jax.ShapeDtypeStruct((rows, D_MODEL), F32),
        compiler_params=_cparams(1), name="tail",
    )(h2d, y_ret, y_diff, gr, gd, wrb, wdb, wo, n2, wup, wdn, nf)


def kernel(x_prompt, x_sample, cache_k, cache_v, state_ret, meta_tokens, rel_bias, norm1_g, w_in,
           lambda_q1, lambda_k1, lambda_q2, lambda_k2, diff_subln_g, w_ret_branch, w_diff_branch,
           w_o, norm2_g, w_ffn_up, w_ffn_down, normf_g):
    assert w_in.shape[0] == 1, "single-layer step only"
    bsz, seq, _ = x_prompt.shape
    dbsz, dseq, _ = x_sample.shape
    past = cache_k.shape[2]
    assert TQ == TK == TM_PROJ and TQ % CHUNK == 0 and DIFF_HEADS % HEADS_PER_STEP == 0
    assert seq % TQ == 0
    assert dseq <= CHUNK and past % CHUNK == 0 and meta_tokens.shape[0] == N_META

    w_in_bf = w_in[0].astype(BF16)
    g1 = norm1_g[0][None, :]
    lam_p = jnp.stack([lambda_q1[0], lambda_k1[0], lambda_q2[0], lambda_k2[0]])
    subg = diff_subln_g[0][None, :]
    wts = (w_ret_branch[0].astype(BF16), w_diff_branch[0].astype(BF16), w_o[0].astype(BF16),
           norm2_g[0][None, :], w_ffn_up[0].astype(BF16), w_ffn_down[0].astype(BF16),
           normf_g[None, :])

    kpad = ((past + dseq + LANES - 1) // LANES) * LANES
    both_maps = lambda ix: np.concatenate([ix.T, ix.T], axis=1)
    idx_diag = both_maps(_bias_idx(np.arange(TQ), np.arange(TK), TK))
    idx_sub = both_maps(_bias_idx(np.arange(TQ) + TK, np.arange(TK), TK))
    idx_meta = np.concatenate(
        [both_maps(_bias_idx(np.arange(TQ) + b * TQ, np.arange(N_META) - N_META, N_META))
         for b in range(2)], axis=0)
    idx_samp = _bias_idx(past + np.arange(dseq), np.arange(kpad), past + dseq)
    t_diag, t_sub, t_meta, t_samp = _bias_tiles(rel_bias, [idx_diag, idx_sub, idx_meta, idx_samp],
                                                [True, True, True, False])
    t_meta = t_meta.reshape(DIFF_HEADS, 2, N_META, 2 * TQ)

    cs_m, sn_m = _rotary_tables(np.arange(-N_META, 0))
    m_out = _in_proj(meta_tokens, g1, w_in_bf, cs_m, sn_m, N_META)
    zero_state = jnp.zeros((1, RET_HEADS, RET_DK, RET_DV), F32)
    _, s_meta = _retention(m_out[0], m_out[1], m_out[2], m_out[3], zero_state, 1, N_META,
                           with_output=False)
    mk = m_out[5]
    mv_aug = jnp.concatenate([m_out[7].reshape(N_META, DIFF_HEADS, DIFF_DV),
                              jnp.ones((N_META, DIFF_HEADS, ONES_ROWS), BF16)], axis=2)
    mvt = mv_aug.reshape(N_META, DIFF_HEADS * DV_AUG).T
    subg_b = jnp.broadcast_to(diff_subln_g[0][:, None], (DIFF_DV, TQ))

    cs_p, sn_p = _rotary_tables(np.arange(seq))
    x2d = x_prompt.reshape(bsz * seq, D_MODEL)
    meta_rows = lambda u: u.reshape(N_META * DIFF_HEADS, DIFF_DV)
    (y_ret, s_fin, dqt, dkb, dvt, gr, gd, k_rows, v_rows) = _prompt_proj(
        x2d, g1, w_in_bf, cs_p, sn_p, s_meta, meta_rows(m_out[6]), meta_rows(m_out[8]), bsz,
        TM_PROJ)
    y_diff = _diff_attn(dqt, dkb, dvt, mk, mvt, t_diag, t_sub, t_meta, lam_p, subg_b, bsz)
    y_prompt = _tail(x2d, y_ret, y_diff, gr, gd, *wts, TM_TAIL).reshape(bsz, seq, D_MODEL)

    cs_s, sn_s = _rotary_tables(np.tile(past + np.arange(dseq), dbsz))
    xs2d = x_sample.reshape(dbsz * dseq, D_MODEL)
    (rq, rk, rv, rg, dq, dkb, dkf_s, dvb, dvf_s, gr, gd) = _in_proj(xs2d, g1, w_in_bf, cs_s, sn_s,
                                                                     dbsz * dseq)
    y_ret, s_samp = _retention(rq, rk, rv, rg, state_ret[0], dbsz, dseq)
    ck = cache_k[0].reshape(dbsz, past * DIFF_HEADS, DIFF_DV)
    cv = cache_v[0].reshape(dbsz, past * DIFF_HEADS, DIFF_DV)
    y_diff = _sample_attn(dq, ck, cv, dkb, dvb, t_samp, lam_p, subg, dbsz, past, dseq)
    y_sample = _tail(xs2d, y_ret, y_diff, gr, gd, *wts, dbsz * dseq).reshape(dbsz, dseq, D_MODEL)

    heads = (DIFF_HEADS, DIFF_DV)
    return (y_prompt, y_sample,
            k_rows.reshape((1, bsz, N_META + seq) + heads),
            v_rows.reshape((1, bsz, N_META + seq) + heads),
            s_fin[None],
            dkf_s.reshape((1, dbsz, dseq) + heads),
            dvf_s.reshape((1, dbsz, dseq) + heads),
            s_samp[None])
```
